```python
import math
import jax, jax.numpy as jnp
from jax import lax
import numpy as np

D_MODEL = 1024
BATCH = 16
SEQ = 4096
DEPTH = 2

HEAD_DIM = 64
SWA_HEADS = 4
SWA_KV_HEADS = 2
SWA_WINDOW = 128
SWA_BLOCK = 128
MOBA_HEADS = 4
MOBA_BLOCK = 256
MOBA_TOPK = 3
MOBA_Q_CHUNK = 64
SB_HEADS = 4
SB_Q_BLOCK = 128
RET_HEADS = 4
RET_CHUNK = 128
D_FF = 4 * D_MODEL
ROPE_THETA = 10000.0
LN_EPS = 1e-5
NEG = -1e30
DEEPNORM_ALPHA = (2 * DEPTH) ** 0.25
DEEPNORM_BETA = (8 * DEPTH) ** -0.25

IN_SPLIT_SIZES = (
    SWA_HEADS * HEAD_DIM, SWA_KV_HEADS * HEAD_DIM, SWA_KV_HEADS * HEAD_DIM,
    MOBA_HEADS * HEAD_DIM, MOBA_HEADS * HEAD_DIM, MOBA_HEADS * HEAD_DIM,
    SB_HEADS * HEAD_DIM, SB_HEADS * HEAD_DIM, SB_HEADS * HEAD_DIM,
    RET_HEADS * HEAD_DIM, RET_HEADS * HEAD_DIM, RET_HEADS * HEAD_DIM, RET_HEADS * HEAD_DIM,
)
IN_IS_VALUE = (False, False, True, False, False, True, False, False, True, False, False, True, False)
IN_WIDTH = sum(IN_SPLIT_SIZES)
MIX_WIDTH = (SWA_HEADS + MOBA_HEADS + SB_HEADS + RET_HEADS) * HEAD_DIM

kernel_name = "hymba_style_four_mixer_deepnorm_block"


def layer_norm(x, w, b):
    xf = x.astype(jnp.float32)
    mu = jnp.mean(xf, -1, keepdims=True)
    var = jnp.mean(jnp.square(xf - mu), -1, keepdims=True)
    return ((xf - mu) * lax.rsqrt(var + LN_EPS) * w + b).astype(x.dtype)


def rope(x):
    s, d = x.shape[1], x.shape[-1]
    inv = 1.0 / (ROPE_THETA ** (jnp.arange(0, d, 2, dtype=jnp.float32) / d))
    ang = jnp.arange(s, dtype=jnp.float32)[:, None] * inv[None, :]
    cos = jnp.cos(ang)[None, :, None, :]
    sin = jnp.sin(ang)[None, :, None, :]
    x1, x2 = x[..., : d // 2], x[..., d // 2:]
    return jnp.concatenate([x1 * cos - x2 * sin, x1 * sin + x2 * cos], -1).astype(x.dtype)


def swa_sink_attention(q, k, v, sinks):
    b, s, hq, d = q.shape
    hkv = k.shape[2]
    g = hq // hkv
    w = SWA_BLOCK
    nb = s // w
    qb = q.reshape(b, nb, w, hkv, g, d)
    kb = k.reshape(b, nb, w, hkv, d)
    vb = v.reshape(b, nb, w, hkv, d)
    kk = jnp.concatenate([jnp.concatenate([jnp.zeros_like(kb[:, :1]), kb[:, :-1]], 1), kb], 2)
    vv = jnp.concatenate([jnp.concatenate([jnp.zeros_like(vb[:, :1]), vb[:, :-1]], 1), vb], 2)
    logits = jnp.einsum('bnqhgd,bnkhd->bnhgqk', qb, kk, preferred_element_type=jnp.float32) * (d ** -0.5)
    blk = jnp.arange(nb)[:, None, None]
    qabs = blk * w + jnp.arange(w)[None, :, None]
    kabs = blk * w + jnp.arange(2 * w)[None, None, :] - w
    mask = (kabs >= 0) & (kabs <= qabs) & (qabs - kabs < SWA_WINDOW)
    logits = jnp.where(mask[None, :, None, None], logits, NEG)
    sink = jnp.broadcast_to(sinks.astype(jnp.float32).reshape(1, 1, hkv, g, 1, 1), logits.shape[:-1] + (1,))
    probs = jax.nn.softmax(jnp.concatenate([logits, sink], -1), axis=-1)[..., :-1]
    out = jnp.einsum('bnhgqk,bnkhd->bnqhgd', probs.astype(v.dtype), vv)
    return out.reshape(b, s, hq, d)


def moba_attention(q, k, v):
    b, s, h, d = q.shape
    lb = MOBA_BLOCK
    sp = ((s + lb - 1) // lb) * lb
    pad = ((0, 0), (0, sp - s), (0, 0), (0, 0))
    qp, kp, vp = jnp.pad(q, pad), jnp.pad(k, pad), jnp.pad(v, pad)
    nb = sp // lb
    k_sel = min(MOBA_TOPK, nb - 1)
    kblk_t = kp.reshape(b, nb, lb, h, d).transpose(0, 3, 1, 2, 4)
    vblk_t = vp.reshape(b, nb, lb, h, d).transpose(0, 3, 1, 2, 4)
    kmean = jnp.mean(kblk_t.astype(jnp.float32), axis=3)
    scale = d ** -0.5
    bi = jnp.arange(b)[:, None, None, None]
    hi = jnp.arange(h)[None, :, None, None]
    nq = MOBA_Q_CHUNK

    def chunk_fn(c):
        start = c * nq
        own = start // lb
        qc = lax.dynamic_slice_in_dim(qp, start, nq, axis=1)
        qpos = start + jnp.arange(nq)
        kpos = own * lb + jnp.arange(lb)
        own_k = lax.dynamic_slice_in_dim(kp, own * lb, lb, axis=1)
        own_v = lax.dynamic_slice_in_dim(vp, own * lb, lb, axis=1)
        s_own = jnp.einsum('bqhd,blhd->bhql', qc, own_k, preferred_element_type=jnp.float32) * scale
        s_own = jnp.where(kpos[None, :] <= qpos[:, None], s_own, NEG)
        if k_sel == 0:
            p = jax.nn.softmax(s_own, axis=-1)
            return jnp.einsum('bhql,blhd->bqhd', p.astype(v.dtype), own_v)
        gate = jnp.einsum('bqhd,bhnd->bhqn', qc.astype(jnp.float32), kmean)
        gate = jnp.where(jnp.arange(nb) < own, gate, -jnp.inf)
        _, top_i = lax.top_k(gate, k_sel)
        sel_valid = top_i < own
        kg = kblk_t[bi, hi, top_i]
        vg = vblk_t[bi, hi, top_i]
        s_sel = jnp.einsum('bqhd,bhqkld->bhqkl', qc, kg, preferred_element_type=jnp.float32) * scale
        s_sel = jnp.where(sel_valid[..., None], s_sel, NEG).reshape(b, h, nq, k_sel * lb)
        p = jax.nn.softmax(jnp.concatenate([s_sel, s_own], -1), axis=-1).astype(v.dtype)
        p_sel = p[..., : k_sel * lb].reshape(b, h, nq, k_sel, lb)
        p_own = p[..., k_sel * lb:]
        return (jnp.einsum('bhqkl,bhqkld->bqhd', p_sel, vg)
                + jnp.einsum('bhql,blhd->bqhd', p_own, own_v))

    outs = lax.map(chunk_fn, jnp.arange(sp // nq))
    return outs.transpose(1, 0, 2, 3, 4).reshape(b, sp, h, d)[:, :s]


def stick_breaking_attention(q, k, v):
    b, s, h, d = q.shape
    nq = SB_Q_BLOCK
    kpos = jnp.arange(s)

    def block_fn(c):
        start = c * nq
        qc = lax.dynamic_slice_in_dim(q, start, nq, axis=1)
        z = jnp.einsum('bqhd,bkhd->bhqk', qc, k, preferred_element_type=jnp.float32) * (d ** -0.5)
        qpos = start + jnp.arange(nq)
        mask = kpos[None, :] < qpos[:, None]
        log_fail = jnp.where(mask, jax.nn.log_sigmoid(-z), 0.0)
        after = lax.cumsum(log_fail, axis=3, reverse=True) - log_fail
        a = jnp.where(mask, jnp.exp(jax.nn.log_sigmoid(z) + after), 0.0)
        return jnp.einsum('bhqk,bkhd->bqhd', a.astype(v.dtype), v)

    outs = lax.map(block_fn, jnp.arange(s // nq))
    return outs.transpose(1, 0, 2, 3, 4).reshape(b, s, h, d)


def retention_chunkwise(q, k, v):
    b, s, h, d = q.shape
    c = RET_CHUNK
    n = s // c
    log_g = jnp.log(1.0 - 2.0 ** (-5.0 - jnp.arange(h, dtype=jnp.float32)))
    idx = jnp.arange(c, dtype=jnp.float32)
    diff = idx[:, None] - idx[None, :]
    decay_in = jnp.where(diff >= 0, jnp.exp(jnp.maximum(diff, 0.0)[None] * log_g[:, None, None]), 0.0)
    qc = q.astype(jnp.float32).reshape(b, n, c, h, d)
    kc = (k.astype(jnp.float32) * (d ** -0.5)).reshape(b, n, c, h, d)
    vc = v.astype(jnp.float32).reshape(b, n, c, h, d)
    inner_s = jnp.einsum('bnihd,bnjhd->bnhij', qc, kc) * decay_in[None, None]
    inner = jnp.einsum('bnhij,bnjhe->bnihe', inner_s, vc)
    k_w = jnp.exp((c - 1.0 - idx)[:, None] * log_g[None, :])
    kv = jnp.einsum('bnjhd,bnjhe->nbhde', kc * k_w[None, None, :, :, None], vc)
    g_c = jnp.exp(c * log_g)[None, :, None, None]

    def step(state, kv_n):
        return state * g_c + kv_n, state

    _, states = lax.scan(step, jnp.zeros((b, h, d, d), jnp.float32), kv)
    q_w = jnp.exp((idx + 1.0)[:, None] * log_g[None, :])
    cross = jnp.einsum('bnihd,nbhde->bnihe', qc * q_w[None, None, :, :, None], states)
    return (inner + cross).reshape(b, s, h, d)


def head_group_norm(y):
    yf = y.astype(jnp.float32)
    mu = jnp.mean(yf, -1, keepdims=True)
    var = jnp.mean(jnp.square(yf - mu), -1, keepdims=True)
    return (yf - mu) * lax.rsqrt(var + LN_EPS)


def hybrid_layer(x, w_in, sinks, w_out, ln1_w, ln1_b, w_up, w_down, ln2_w, ln2_b):
    b, s, _ = x.shape
    proj = x @ w_in
    points, acc = [], 0
    for n_cols in IN_SPLIT_SIZES[:-1]:
        acc += n_cols
        points.append(acc)
    aq, ak, av, bq, bk, bv, cq, ck, cv, dq, dk, dv, dg = jnp.split(proj, points, axis=-1)
    heads = lambda t: t.reshape(b, s, -1, HEAD_DIM)
    ya = swa_sink_attention(rope(heads(aq)), rope(heads(ak)), heads(av), sinks)
    yb = moba_attention(rope(heads(bq)), rope(heads(bk)), heads(bv))
    yc = stick_breaking_attention(heads(cq), heads(ck), heads(cv))
    yd = retention_chunkwise(rope(heads(dq)), rope(heads(dk)), heads(dv))
    yd = (head_group_norm(yd).reshape(b, s, -1) * jax.nn.silu(dg.astype(jnp.float32))).astype(x.dtype)
    mixed = jnp.concatenate([ya.reshape(b, s, -1).astype(x.dtype), yb.reshape(b, s, -1).astype(x.dtype),
                             yc.reshape(b, s, -1).astype(x.dtype), yd], axis=-1)
    x = layer_norm(DEEPNORM_ALPHA * x + mixed @ w_out, ln1_w, ln1_b)
    hid = jnp.square(jax.nn.relu(x @ w_up))
    return layer_norm(DEEPNORM_ALPHA * x + hid @ w_down, ln2_w, ln2_b)


def setup_inputs(seed: int = 0) -> dict:
    key = jax.random.key(seed)
    ks = jax.random.split(key, 11)
    x = jax.random.normal(ks[0], (BATCH, SEQ, D_MODEL), jnp.float32)
    col_scale = jnp.concatenate([jnp.full((n_cols,), DEEPNORM_BETA if is_v else 1.0, jnp.float32)
                                 for n_cols, is_v in zip(IN_SPLIT_SIZES, IN_IS_VALUE)])
    w_in = jax.random.normal(ks[1], (DEPTH, D_MODEL, IN_WIDTH), jnp.float32) * (D_MODEL ** -0.5) * col_scale
    swa_sinks = 0.5 * jax.random.normal(ks[2], (DEPTH, SWA_HEADS), jnp.float32)
    w_out = jax.random.normal(ks[3], (DEPTH, MIX_WIDTH, D_MODEL), jnp.float32) * (MIX_WIDTH ** -0.5) * DEEPNORM_BETA
    ln1_w = 1.0 + 0.02 * jax.random.normal(ks[4], (DEPTH, D_MODEL), jnp.float32)
    ln1_b = 0.02 * jax.random.normal(ks[5], (DEPTH, D_MODEL), jnp.float32)
    w_up = jax.random.normal(ks[6], (DEPTH, D_MODEL, D_FF), jnp.float32) * (D_MODEL ** -0.5)
    w_down = jax.random.normal(ks[7], (DEPTH, D_FF, D_MODEL), jnp.float32) * (D_FF ** -0.5) * DEEPNORM_BETA
    ln2_w = 1.0 + 0.02 * jax.random.normal(ks[8], (DEPTH, D_MODEL), jnp.float32)
    ln2_b = 0.02 * jax.random.normal(ks[9], (DEPTH, D_MODEL), jnp.float32)
    return {"x": x, "w_in": w_in, "swa_sinks": swa_sinks, "w_out": w_out, "ln1_w": ln1_w, "ln1_b": ln1_b,
            "w_up": w_up, "w_down": w_down, "ln2_w": ln2_w, "ln2_b": ln2_b}


def reference(x, w_in, swa_sinks, w_out, ln1_w, ln1_b, w_up, w_down, ln2_w, ln2_b):
    for layer in range(DEPTH):
        x = hybrid_layer(x, w_in[layer], swa_sinks[layer], w_out[layer], ln1_w[layer], ln1_b[layer],
                         w_up[layer], w_down[layer], ln2_w[layer], ln2_b[layer])
    return x
```

```python
import functools
import math

import jax
import jax.numpy as jnp
from jax import lax
from jax.experimental import pallas as pl
from jax.experimental.pallas import tpu as pltpu

F32 = jnp.float32
BF16 = jnp.bfloat16

D_MODEL = 1024
HEAD_DIM = 64
LANES = 128
D_FF = 4 * D_MODEL
DEPTH = 2
ROPE_THETA = 10000.0
LN_EPS = 1e-5
NEG = -1e30
DEEPNORM_ALPHA = (2 * DEPTH) ** 0.25
QK_SCALE = HEAD_DIM ** -0.5

SWA_BLOCK = 128
SWA_WINDOW = 128
MOBA_BLOCK = 256
MOBA_TOPK = 3
SB_BLOCK = 128
RET_CHUNK = 128
RET_HEADS = 4

BLK_A_Q, BLK_A_K, BLK_A_V = 0, 2, 3
BLK_B_Q, BLK_B_K, BLK_B_V = 4, 6, 8
BLK_C_Q, BLK_C_K, BLK_C_V = 10, 12, 14
BLK_D_Q, BLK_D_K, BLK_D_V, BLK_D_G = 16, 18, 20, 22
N_QKV_BLKS = 22
ROPED_BLKS = frozenset([0, 1, 2, 4, 5, 6, 7, 16, 17, 18, 19])
SCALED_BLKS = frozenset([0, 1, 4, 5, 10, 11, 18, 19])

SB_EXIT = -110.0

VMEM_LIMIT = 56 * 1024 * 1024


def _dot(a, b, **kw):
    return jnp.dot(a, b, preferred_element_type=F32, **kw)


def _dot_nt(a, b, **kw):
    return lax.dot_general(a, b, (((1,), (1,)), ((), ())), preferred_element_type=F32, **kw)


def _lane_iota(shape):
    return lax.broadcasted_iota(jnp.int32, shape, len(shape) - 1)


def _row_iota(shape):
    return lax.broadcasted_iota(jnp.int32, shape, len(shape) - 2)


def _in_proj_kernel(x_ref, w_ref, cos_ref, sin_ref, qkv_ref, bq_ref, kmean_ref, g_ref, *, tm):
    xb = x_ref[0].astype(BF16)
    cos = cos_ref[...]
    sin = sin_ref[...]
    first_half = (_lane_iota((tm, LANES)) % HEAD_DIM) < (HEAD_DIM // 2)
    n_pairs = w_ref.shape[1] // (2 * LANES)
    for c in range(n_pairs):
        acc2 = _dot(xb, w_ref[:, c * 2 * LANES:(c + 1) * 2 * LANES])
        for half in range(2):
            blk = 2 * c + half
            acc = acc2[:, half * LANES:(half + 1) * LANES]
            if blk in ROPED_BLKS:
                partner = jnp.where(first_half,
                                    pltpu.roll(acc, LANES - HEAD_DIM // 2, 1),
                                    pltpu.roll(acc, HEAD_DIM // 2, 1))
                acc = acc * cos + partner * sin
            if blk in (BLK_B_Q, BLK_B_Q + 1):
                o = (blk - BLK_B_Q) * LANES
                bq_ref[0, :, o:o + LANES] = acc
            if blk in (BLK_B_K, BLK_B_K + 1):
                o = (blk - BLK_B_K) * LANES
                nblk = tm // MOBA_BLOCK
                means = [jnp.sum(acc[i * MOBA_BLOCK:(i + 1) * MOBA_BLOCK], axis=0, keepdims=True)
                         * (1.0 / MOBA_BLOCK) for i in range(nblk)]
                means.append(jnp.zeros((8 - nblk, LANES), F32))
                kmean_ref[0, :, o:o + LANES] = jnp.concatenate(means, axis=0)
            if blk in (BLK_D_G, BLK_D_G + 1):
                o = (blk - BLK_D_G) * LANES
                g_ref[0, :, o:o + LANES] = acc
            if blk < N_QKV_BLKS:
                if blk in SCALED_BLKS:
                    acc = acc * QK_SCALE
                qkv_ref[0, :, blk * LANES:(blk + 1) * LANES] = acc.astype(BF16)


def _in_proj(x, w_in_bf16, cos_t, sin_t, *, tm):
    b, s, d = x.shape
    nt = s // tm
    n_in = w_in_bf16.shape[1]
    kern = functools.partial(_in_proj_kernel, tm=tm)
    return pl.pallas_call(
        kern,
        grid=(b, nt),
        in_specs=[
            pl.BlockSpec((1, tm, d), lambda i, t: (i, t, 0)),
            pl.BlockSpec((d, n_in), lambda i, t: (0, 0), pipeline_mode=pl.Buffered(1)),
            pl.BlockSpec((tm, LANES), lambda i, t: (t, 0)),
            pl.BlockSpec((tm, LANES), lambda i, t: (t, 0)),
        ],
        out_specs=[
            pl.BlockSpec((1, tm, N_QKV_BLKS * LANES), lambda i, t: (i, t, 0)),
            pl.BlockSpec((1, tm, 2 * LANES), lambda i, t: (i, t, 0)),
            pl.BlockSpec((1, 8, 2 * LANES), lambda i, t: (i * nt + t, 0, 0)),
            pl.BlockSpec((1, tm, 2 * LANES), lambda i, t: (i, t, 0)),
        ],
        out_shape=[
            jax.ShapeDtypeStruct((b, s, N_QKV_BLKS * LANES), BF16),
            jax.ShapeDtypeStruct((b, s, 2 * LANES), F32),
            jax.ShapeDtypeStruct((b * nt, 8, 2 * LANES), F32),
            jax.ShapeDtypeStruct((b, s, 2 * LANES), F32),
        ],
        compiler_params=pltpu.CompilerParams(
            dimension_semantics=("arbitrary", "arbitrary"), vmem_limit_bytes=VMEM_LIMIT),
        name="in_proj",
    )(x, w_in_bf16, cos_t, sin_t)


def _swa_kernel(sinks_ref, q_ref, k_ref, v_ref, o_ref, kd_ref, vd_ref, *, seq):
    j = pl.program_id(1)
    w = SWA_BLOCK
    nb = seq // w
    chunk = 512

    def prep(c, _):
        rows = pl.ds(pl.multiple_of(c * chunk, chunk), chunk)
        keep = (_lane_iota((chunk, LANES)) // HEAD_DIM) == j
        for src, dst in ((k_ref, kd_ref), (v_ref, vd_ref)):
            t = src[0, rows, :].astype(F32)
            dst[rows, :] = jnp.where(keep, t, pltpu.roll(t, HEAD_DIM, 1)).astype(BF16)
        return 0

    lax.fori_loop(0, seq // chunk, prep, 0)

    lane_lo = _lane_iota((w, LANES)) < HEAD_DIM
    row2 = _row_iota((2 * w, 1))
    sink = jnp.where(row2 < w, sinks_ref[2 * j], sinks_ref[2 * j + 1])
    qrow = _row_iota((2 * w, 2 * w)) % w
    kcol = _lane_iota((2 * w, 2 * w))

    def body(i, _):
        qs = pl.multiple_of(i * w, w)
        ks = pl.multiple_of(jnp.maximum(i - 1, 0) * w, w)
        q2 = q_ref[0, pl.ds(qs, w), :].astype(F32)
        qst = jnp.concatenate([jnp.where(lane_lo, q2, 0.0), jnp.where(lane_lo, 0.0, q2)], axis=0).astype(BF16)
        kk = kd_ref[pl.ds(ks, 2 * w), :]
        vv = vd_ref[pl.ds(ks, 2 * w), :]
        s = _dot_nt(qst, kk)
        dist = (qs + qrow) - (ks + kcol)
        s = jnp.where((dist >= 0) & (dist < SWA_WINDOW), s, NEG)
        m = jnp.maximum(jnp.max(s, axis=-1, keepdims=True), sink)
        p = jnp.exp(s - m)
        den = jnp.sum(p, axis=-1, keepdims=True) + jnp.exp(sink - m)
        p = p * (1.0 / den)
        o = _dot(p.astype(BF16), vv)
        o_ref[0, pl.ds(qs, w), :] = jnp.where(lane_lo, o[:w], o[w:]).astype(o_ref.dtype)
        return 0

    lax.fori_loop(0, nb, body, 0)


def _swa(qkv, sinks):
    b, s, _ = qkv.shape
    kern = functools.partial(_swa_kernel, seq=s)
    return pl.pallas_call(
        kern,
        grid=(b, 2),
        in_specs=[
            pl.BlockSpec(memory_space=pltpu.SMEM),
            pl.BlockSpec((1, s, LANES), lambda i, j: (i, 0, BLK_A_Q + j)),
            pl.BlockSpec((1, s, LANES), lambda i, j: (i, 0, BLK_A_K)),
            pl.BlockSpec((1, s, LANES), lambda i, j: (i, 0, BLK_A_V)),
        ],
        out_specs=pl.BlockSpec((1, s, LANES), lambda i, j: (i, 0, j)),
        out_shape=jax.ShapeDtypeStruct((b, s, 2 * LANES), BF16),
        scratch_shapes=[pltpu.VMEM((s, LANES), BF16), pltpu.VMEM((s, LANES), BF16)],
        compiler_params=pltpu.CompilerParams(
            dimension_semantics=("arbitrary", "arbitrary"), vmem_limit_bytes=VMEM_LIMIT),
        name="swa",
    )(sinks, qkv, qkv, qkv)


def _moba_kernel(q_ref, k_ref, v_ref, q32_ref, kmean_ref, o_ref, ka_ref, m_ref, l_ref, acc_ref, *, seq):
    lb = MOBA_BLOCK
    nb = seq // lb
    chunk = 512

    def prep(c, _):
        rows = pl.ds(pl.multiple_of(c * chunk, chunk), chunk)
        lane = _lane_iota((chunk, LANES))
        blk_of_row = (c * chunk + _row_iota((chunk, LANES))) // lb
        kt = k_ref[0, rows, :].astype(F32)
        ka_ref[0, rows, :] = jnp.where(lane < HEAD_DIM, kt,
                                       jnp.where(lane - HEAD_DIM == blk_of_row, 1.0, 0.0)).astype(BF16)
        ka_ref[1, rows, :] = jnp.where(lane >= HEAD_DIM, kt,
                                       jnp.where(lane == blk_of_row, 1.0, 0.0)).astype(BF16)
        return 0

    lax.fori_loop(0, seq // chunk, prep, 0)

    lane_lo_k = _lane_iota((nb, LANES)) < HEAD_DIM
    km = kmean_ref[0]
    kmh = (jnp.where(lane_lo_k, km, 0.0), jnp.where(lane_lo_k, 0.0, km))
    blk_iota = _row_iota((nb, lb))
    lane_lo = _lane_iota((lb, LANES)) < HEAD_DIM
    causal = _lane_iota((lb, lb)) <= _row_iota((lb, lb))

    def tile(i, _):
        qs = pl.multiple_of(i * lb, lb)
        q2 = q_ref[0, pl.ds(qs, lb), :]
        q32 = q32_ref[0, pl.ds(qs, lb), :]
        valid = blk_iota < i

        bias_t = []
        for h in range(2):
            gate = _dot_nt(kmh[h], q32, precision=lax.Precision.HIGHEST)
            gm = jnp.where(valid, gate, -jnp.inf)
            rank = jnp.zeros((nb, lb), F32)
            for m_idx in range(nb):
                row = gm[m_idx:m_idx + 1, :]
                tie = jnp.where(blk_iota > m_idx, 1.0, 0.0)
                rank = rank + jnp.where(row > gm, 1.0, jnp.where(row == gm, tie, 0.0))
            routed = jnp.where(rank < MOBA_TOPK, 0.0, NEG)
            bias_t.append(jnp.where(valid, routed, 0.0))
        pad = jnp.zeros((HEAD_DIM - nb, lb), F32)
        bias = jnp.concatenate([bias_t[1], pad, bias_t[0], pad], axis=0).T
        q2f = q2.astype(F32)
        qa = (jnp.where(lane_lo, q2f, bias).astype(BF16), jnp.where(lane_lo, bias, q2f).astype(BF16))

        vv = v_ref[0, pl.ds(qs, lb), :]
        for h in range(2):
            s = _dot_nt(qa[h], ka_ref[h, pl.ds(qs, lb), :])
            s = jnp.where(causal, s, NEG)
            m = jnp.max(s, axis=-1, keepdims=True)
            p = jnp.exp(s - m)
            m_ref[h] = m
            l_ref[h] = jnp.sum(p, axis=-1, keepdims=True)
            acc_ref[h] = _dot(p.astype(BF16), vv)

        def past(n, _):
            ks = pl.multiple_of(n * lb, lb)
            vv = v_ref[0, pl.ds(ks, lb), :]
            for h in range(2):
                s = _dot_nt(qa[h], ka_ref[h, pl.ds(ks, lb), :])
                m_old = m_ref[h]
                m_new = jnp.maximum(m_old, jnp.max(s, axis=-1, keepdims=True))
                alpha = jnp.exp(m_old - m_new)
                p = jnp.exp(s - m_new)
                m_ref[h] = m_new
                l_ref[h] = alpha * l_ref[h] + jnp.sum(p, axis=-1, keepdims=True)
                acc_ref[h] = alpha * acc_ref[h] + _dot(p.astype(BF16), vv)
            return 0

        lax.fori_loop(0, i, past, 0)
        o0 = acc_ref[0] * (1.0 / l_ref[0])
        o1 = acc_ref[1] * (1.0 / l_ref[1])
        o_ref[0, pl.ds(qs, lb), :] = jnp.where(lane_lo, o0, o1).astype(o_ref.dtype)
        return 0

    lax.fori_loop(0, nb, tile, 0)


def _moba(qkv, bq32, kmean):
    b, s, _ = qkv.shape
    nb = s // MOBA_BLOCK
    kern = functools.partial(_moba_kernel, seq=s)
    return pl.pallas_call(
        kern,
        grid=(b, 2),
        in_specs=[
            pl.BlockSpec((1, s, LANES), lambda i, j: (i, 0, BLK_B_Q + j)),
            pl.BlockSpec((1, s, LANES), lambda i, j: (i, 0, BLK_B_K + j)),
            pl.BlockSpec((1, s, LANES), lambda i, j: (i, 0, BLK_B_V + j)),
            pl.BlockSpec((1, s, LANES), lambda i, j: (i, 0, j)),
            pl.BlockSpec((1, nb, LANES), lambda i, j: (i, 0, j)),
        ],
        out_specs=pl.BlockSpec((1, s, LANES), lambda i, j: (i, 0, j)),
        out_shape=jax.ShapeDtypeStruct((b, s, 2 * LANES), BF16),
        scratch_shapes=[
            pltpu.VMEM((2, s, LANES), BF16),
            pltpu.VMEM((2, MOBA_BLOCK, 1), F32),
            pltpu.VMEM((2, MOBA_BLOCK, 1), F32),
            pltpu.VMEM((2, MOBA_BLOCK, LANES), F32),
        ],
        compiler_params=pltpu.CompilerParams(
            dimension_semantics=("arbitrary", "arbitrary"), vmem_limit_bytes=VMEM_LIMIT),
        name="moba",
    )(qkv, qkv, qkv, bq32, kmean)


def _sb_kernel(q_ref, k_ref, v_ref, o_ref, km_ref, carry_ref, acc_ref, *, seq):
    w = SB_BLOCK
    nb = seq // w
    chunk = 512

    def prep(c, _):
        rows = pl.ds(pl.multiple_of(c * chunk, chunk), chunk)
        lane_lo = _lane_iota((chunk, LANES)) < HEAD_DIM
        kt = k_ref[0, rows, :].astype(F32)
        km_ref[0, rows, :] = jnp.where(lane_lo, kt, 0.0).astype(BF16)
        km_ref[1, rows, :] = jnp.where(lane_lo, 0.0, kt).astype(BF16)
        return 0

    lax.fori_loop(0, seq // chunk, prep, 0)

    tri = jnp.where((_lane_iota((w, 2 * w)) >= w) | (_row_iota((w, 2 * w)) > _lane_iota((w, 2 * w))),
                    1.0, 0.0).astype(BF16)
    strict = _lane_iota((w, w)) < _row_iota((w, w))
    lane_lo = _lane_iota((w, LANES)) < HEAD_DIM

    def step(q2, ks, diag):
        vv = v_ref[0, pl.ds(ks, w), :]
        for h in range(2):
            z = _dot_nt(q2, km_ref[h, pl.ds(ks, w), :])
            soft = jnp.log(1.0 + jnp.exp(-jnp.abs(z)))
            log_beta = jnp.minimum(z, 0.0) - soft
            log_fail = log_beta - z
            if diag:
                log_fail = jnp.where(strict, log_fail, 0.0)
            hi = log_fail.astype(BF16)
            lo = (log_fail - hi.astype(F32)).astype(BF16)
            r = _dot(hi, tri) + _dot(lo, tri)
            if diag:
                after = r[:, :w]
            else:
                after = r[:, :w] + carry_ref[h]
            a = jnp.exp(log_beta + after)
            if diag:
                a = jnp.where(strict, a, 0.0)
            pv = _dot(a.astype(BF16), vv)
            if diag:
                acc_ref[h] = pv
                carry_ref[h] = r[:, w:]
            else:
                acc_ref[h] = acc_ref[h] + pv
                carry_ref[h] = carry_ref[h] + r[:, w:]

    def tile(i, _):
        qs = pl.multiple_of(i * w, w)
        q2 = q_ref[0, pl.ds(qs, w), :]
        step(q2, qs, True)

        def cond(st):
            n, live = st
            return (n >= 0) & live

        def body(st):
            n, _ = st
            step(q2, pl.multiple_of(n * w, w), False)
            live = jnp.maximum(jnp.max(carry_ref[0]), jnp.max(carry_ref[1])) > SB_EXIT
            return n - 1, live

        lax.while_loop(cond, body, (i - 1, True))
        o_ref[0, pl.ds(qs, w), :] = jnp.where(lane_lo, acc_ref[0], acc_ref[1]).astype(o_ref.dtype)
        return 0

    lax.fori_loop(0, nb, tile, 0)


def _sb(qkv):
    b, s, _ = qkv.shape
    kern = functools.partial(_sb_kernel, seq=s)
    return pl.pallas_call(
        kern,
        grid=(b, 2),
        in_specs=[
            pl.BlockSpec((1, s, LANES), lambda i, j: (i, 0, BLK_C_Q + j)),
            pl.BlockSpec((1, s, LANES), lambda i, j: (i, 0, BLK_C_K + j)),
            pl.BlockSpec((1, s, LANES), lambda i, j: (i, 0, BLK_C_V + j)),
        ],
        out_specs=pl.BlockSpec((1, s, LANES), lambda i, j: (i, 0, j)),
        out_shape=jax.ShapeDtypeStruct((b, s, 2 * LANES), BF16),
        scratch_shapes=[
            pltpu.VMEM((2, s, LANES), BF16),
            pltpu.VMEM((2, SB_BLOCK, SB_BLOCK), F32),
            pltpu.VMEM((2, SB_BLOCK, LANES), F32),
        ],
        compiler_params=pltpu.CompilerParams(
            dimension_semantics=("arbitrary", "arbitrary"), vmem_limit_bytes=VMEM_LIMIT),
        name="stick_breaking",
    )(qkv, qkv, qkv)


def _ret_kernel(q_ref, k_ref, v_ref, g_ref, decay_ref, qw_ref, kw_ref, gc_ref, o_ref, state_ref, *, seq):
    c = RET_CHUNK
    n = seq // c
    lane_lo = _lane_iota((c, LANES)) < HEAD_DIM
    state_ref[...] = jnp.zeros_like(state_ref)

    def chunk(i, _):
        rows = pl.ds(pl.multiple_of(i * c, c), c)
        for p in range(2):
            cols = slice(p * LANES, (p + 1) * LANES)
            q2 = q_ref[0, rows, cols]
            k2 = k_ref[0, rows, cols]
            v2 = v_ref[0, rows, cols]
            k2f = k2.astype(F32)
            inner = []
            for h in range(2):
                kh = jnp.where(lane_lo, k2f, 0.0) if h == 0 else jnp.where(lane_lo, 0.0, k2f)
                sc = _dot_nt(q2, kh.astype(BF16)) * decay_ref[2 * p + h]
                inner.append(_dot(sc.astype(BF16), v2))
            state = state_ref[p]
            cross = _dot(q2, state.astype(BF16)) * qw_ref[p]
            y = jnp.where(lane_lo, inner[0], inner[1]) + cross
            kt = (k2f * kw_ref[p]).T.astype(BF16)
            gcm = gc_ref[p]
            state_ref[p] = state * gcm + jnp.where(gcm > 0.0, _dot(kt, v2), 0.0)

            def half_stat(t):
                lo = jnp.sum(jnp.where(lane_lo, t, 0.0), axis=-1, keepdims=True)
                hi = jnp.sum(jnp.where(lane_lo, 0.0, t), axis=-1, keepdims=True)
                return jnp.where(lane_lo, lo, hi) * (1.0 / HEAD_DIM)

            mu = half_stat(y)
            yc = y - mu
            var = half_stat(yc * yc)
            yn = yc * lax.rsqrt(var + LN_EPS)
            g = g_ref[0, rows, cols]
            gate = g * (1.0 / (1.0 + jnp.exp(-g)))
            o_ref[0, rows, cols] = (yn * gate).astype(o_ref.dtype)
        return 0

    lax.fori_loop(0, n, chunk, 0)


def _ret(qkv, g, decay, qw, kw, gcm):
    b, s, _ = qkv.shape
    kern = functools.partial(_ret_kernel, seq=s)
    full3 = lambda a: pl.BlockSpec(a.shape, lambda i: (0, 0, 0))
    return pl.pallas_call(
        kern,
        grid=(b,),
        in_specs=[
            pl.BlockSpec((1, s, 2 * LANES), lambda i: (i, 0, BLK_D_Q // 2)),
            pl.BlockSpec((1, s, 2 * LANES), lambda i: (i, 0, BLK_D_K // 2)),
            pl.BlockSpec((1, s, 2 * LANES), lambda i: (i, 0, BLK_D_V // 2)),
            pl.BlockSpec((1, s, 2 * LANES), lambda i: (i, 0, 0)),
            full3(decay), full3(qw), full3(kw), full3(gcm),
        ],
        out_specs=pl.BlockSpec((1, s, 2 * LANES), lambda i: (i, 0, 0)),
        out_shape=jax.ShapeDtypeStruct((b, s, 2 * LANES), BF16),
        scratch_shapes=[pltpu.VMEM((2, LANES, LANES), F32)],
        compiler_params=pltpu.CompilerParams(
            dimension_semantics=("arbitrary",), vmem_limit_bytes=VMEM_LIMIT),
        name="retention",
    )(qkv, qkv, qkv, g, decay, qw, kw, gcm)


def _layer_norm(t, w, b):
    mu = jnp.mean(t, axis=-1, keepdims=True)
    tc = t - mu
    var = jnp.mean(tc * tc, axis=-1, keepdims=True)
    return tc * lax.rsqrt(var + LN_EPS) * w + b


def _out_mlp_kernel(ya_ref, yb_ref, yc_ref, yd_ref, x_ref, wo_ref, l1w_ref, l1b_ref,
                    wu_ref, wd_ref, l2w_ref, l2b_ref, o_ref):
    width = ya_ref.shape[-1]
    h = None
    for idx, y_ref in enumerate((ya_ref, yb_ref, yc_ref, yd_ref)):
        part = _dot(y_ref[0], wo_ref[idx * width:(idx + 1) * width, :])
        h = part if h is None else h + part
    x1 = _layer_norm(DEEPNORM_ALPHA * x_ref[0] + h, l1w_ref[...], l1b_ref[...])
    x1b = x1.astype(BF16)
    ff = wu_ref.shape[1]
    step = 1024
    acc = None
    for c in range(ff // step):
        hid = _dot(x1b, wu_ref[:, c * step:(c + 1) * step])
        hid = jnp.square(jnp.maximum(hid, 0.0)).astype(BF16)
        part = _dot(hid, wd_ref[c * step:(c + 1) * step, :])
        acc = part if acc is None else acc + part
    o_ref[0] = _layer_norm(DEEPNORM_ALPHA * x1 + acc, l2w_ref[...], l2b_ref[...])


def _out_mlp(ya, yb, yc, yd, x, wo, l1w, l1b, wu, wd, l2w, l2b, *, tm):
    b, s, d = x.shape
    nt = s // tm
    ytile = pl.BlockSpec((1, tm, ya.shape[-1]), lambda i, t: (i, t, 0))
    const = lambda a: pl.BlockSpec(a.shape, lambda i, t: (0, 0), pipeline_mode=pl.Buffered(1))
    return pl.pallas_call(
        _out_mlp_kernel,
        grid=(b, nt),
        in_specs=[ytile, ytile, ytile, ytile,
                  pl.BlockSpec((1, tm, d), lambda i, t: (i, t, 0)),
                  const(wo), const(l1w), const(l1b), const(wu), const(wd), const(l2w), const(l2b)],
        out_specs=pl.BlockSpec((1, tm, d), lambda i, t: (i, t, 0)),
        out_shape=jax.ShapeDtypeStruct((b, s, d), F32),
        compiler_params=pltpu.CompilerParams(
            dimension_semantics=("arbitrary", "arbitrary"), vmem_limit_bytes=VMEM_LIMIT),
        name="out_mlp",
    )(ya, yb, yc, yd, x, wo, l1w, l1b, wu, wd, l2w, l2b)


def _rope_tables(s):
    half = HEAD_DIM // 2
    inv = 1.0 / (ROPE_THETA ** (jnp.arange(0, HEAD_DIM, 2, dtype=F32) / HEAD_DIM))
    ang = jnp.arange(s, dtype=F32)[:, None] * inv[None, :]
    cos = jnp.cos(ang)
    sin = jnp.sin(ang)
    cos_t = jnp.tile(cos, (1, LANES // half))
    sin_t = jnp.tile(jnp.concatenate([-sin, sin], axis=1), (1, LANES // HEAD_DIM))
    return cos_t, sin_t


def _retention_tables():
    c = RET_CHUNK
    h = RET_HEADS
    log_g = jnp.log(1.0 - 2.0 ** (-5.0 - jnp.arange(h, dtype=F32)))
    idx = jnp.arange(c, dtype=F32)
    diff = idx[:, None] - idx[None, :]
    decay = jnp.where(diff >= 0, jnp.exp(jnp.maximum(diff, 0.0)[None] * log_g[:, None, None]), 0.0)
    k_w = jnp.exp((c - 1.0 - idx)[:, None] * log_g[None, :])
    q_w = jnp.exp((idx + 1.0)[:, None] * log_g[None, :])
    g_c = jnp.exp(c * log_g)
    per_lane = lambda t: jnp.repeat(t, HEAD_DIM, axis=1).reshape(c, h // 2, LANES).transpose(1, 0, 2)
    head_of = jnp.arange(LANES) // HEAD_DIM
    same = head_of[:, None] == head_of[None, :]
    gcm = jnp.stack([jnp.where(same, g_c[2 * p + head_of][None, :], 0.0) for p in range(h // 2)])
    return decay, per_lane(q_w), per_lane(k_w), gcm


def kernel(x, w_in, swa_sinks, w_out, ln1_w, ln1_b, w_up, w_down, ln2_w, ln2_b):
    b, s, d = x.shape
    cos_t, sin_t = _rope_tables(s)
    decay, qw, kw, gcm = _retention_tables()
    row = lambda t: t.reshape(1, -1)
    tm_in = 512
    tm_out = 512
    for layer in range(w_in.shape[0]):
        qkv, bq32, kmean8, g = _in_proj(x, w_in[layer].astype(BF16), cos_t, sin_t, tm=tm_in)
        nblk = tm_in // MOBA_BLOCK
        kmean = kmean8.reshape(b, s // tm_in, 8, 2 * LANES)[:, :, :nblk].reshape(b, s // MOBA_BLOCK, 2 * LANES)
        ya = _swa(qkv, swa_sinks[layer])
        yb = _moba(qkv, bq32, kmean)
        yc = _sb(qkv)
        yd = _ret(qkv, g, decay, qw, kw, gcm)
        x = _out_mlp(ya, yb, yc, yd, x, w_out[layer].astype(BF16), row(ln1_w[layer]), row(ln1_b[layer]),
                     w_up[layer].astype(BF16), w_down[layer].astype(BF16), row(ln2_w[layer]), row(ln2_b[layer]),
                     tm=tm_out)
    return x
```

```python
import functools
import math

import jax
import jax.numpy as jnp
from jax import lax
from jax.experimental import pallas as pl
from jax.experimental.pallas import tpu as pltpu

F32 = jnp.float32
BF16 = jnp.bfloat16

D_MODEL = 1024
HEAD_DIM = 64
LANES = 128
D_FF = 4 * D_MODEL
DEPTH = 2
ROPE_THETA = 10000.0
LN_EPS = 1e-5
NEG = -1e30
DEEPNORM_ALPHA = (2 * DEPTH) ** 0.25
QK_SCALE = HEAD_DIM ** -0.5

SWA_BLOCK = 128
SWA_WINDOW = 128
MOBA_BLOCK = 256
MOBA_CHUNK = 128
MOBA_TOPK = 3
SB_BLOCK = 128
SB_WINDOW_BLKS = 4
RET_CHUNK = 128
RET_HEADS = 4

BLK_A_Q, BLK_A_K, BLK_A_V = 0, 2, 3
BLK_B_Q, BLK_B_K, BLK_B_V = 4, 6, 8
BLK_C_Q, BLK_C_K, BLK_C_V = 10, 12, 14
BLK_D_Q, BLK_D_K, BLK_D_V, BLK_D_G = 16, 18, 20, 22
N_QKV_BLKS = 22
ROPED_BLKS = frozenset([0, 1, 2, 4, 5, 6, 7, 16, 17, 18, 19])
SCALED_BLKS = frozenset([0, 1, 4, 5, 10, 11, 18, 19])

SB_EXIT = -110.0

VMEM_LIMIT = 56 * 1024 * 1024


def _dot(a, b, **kw):
    return jnp.dot(a, b, preferred_element_type=F32, **kw)


def _dot_nt(a, b, **kw):
    return lax.dot_general(a, b, (((1,), (1,)), ((), ())), preferred_element_type=F32, **kw)


def _lane_iota(shape):
    return lax.broadcasted_iota(jnp.int32, shape, len(shape) - 1)


def _row_iota(shape):
    return lax.broadcasted_iota(jnp.int32, shape, len(shape) - 2)


def _in_proj_kernel(x_ref, w_ref, cos_ref, sin_ref, qkv_ref, bq_ref, kmean_ref, g_ref, *, tm):
    xb = x_ref[0].astype(BF16)
    cos = cos_ref[...]
    sin = sin_ref[...]
    first_half = (_lane_iota((tm, LANES)) % HEAD_DIM) < (HEAD_DIM // 2)
    n_pairs = w_ref.shape[1] // (2 * LANES)
    for c in range(n_pairs):
        acc2 = _dot(xb, w_ref[:, c * 2 * LANES:(c + 1) * 2 * LANES])
        for half in range(2):
            blk = 2 * c + half
            acc = acc2[:, half * LANES:(half + 1) * LANES]
            if blk in ROPED_BLKS:
                partner = jnp.where(first_half,
                                    pltpu.roll(acc, LANES - HEAD_DIM // 2, 1),
                                    pltpu.roll(acc, HEAD_DIM // 2, 1))
                acc = acc * cos + partner * sin
            if blk in (BLK_B_Q, BLK_B_Q + 1):
                o = (blk - BLK_B_Q) * LANES
                bq_ref[0, :, o:o + LANES] = acc
            if blk in (BLK_B_K, BLK_B_K + 1):
                o = (blk - BLK_B_K) * LANES
                nblk = tm // MOBA_BLOCK
                means = [jnp.sum(acc[i * MOBA_BLOCK:(i + 1) * MOBA_BLOCK], axis=0, keepdims=True)
                         * (1.0 / MOBA_BLOCK) for i in range(nblk)]
                means.append(jnp.zeros((8 - nblk, LANES), F32))
                kmean_ref[0, :, o:o + LANES] = jnp.concatenate(means, axis=0)
            if blk in (BLK_D_G, BLK_D_G + 1):
                o = (blk - BLK_D_G) * LANES
                g_ref[0, :, o:o + LANES] = acc
            if blk < N_QKV_BLKS:
                if blk in SCALED_BLKS:
                    acc = acc * QK_SCALE
                qkv_ref[0, :, blk * LANES:(blk + 1) * LANES] = acc.astype(BF16)


def _in_proj(x, w_in_bf16, cos_t, sin_t, *, tm):
    b, s, d = x.shape
    nt = s // tm
    n_in = w_in_bf16.shape[1]
    kern = functools.partial(_in_proj_kernel, tm=tm)
    return pl.pallas_call(
        kern,
        grid=(b, nt),
        in_specs=[
            pl.BlockSpec((1, tm, d), lambda i, t: (i, t, 0)),
            pl.BlockSpec((d, n_in), lambda i, t: (0, 0), pipeline_mode=pl.Buffered(1)),
            pl.BlockSpec((tm, LANES), lambda i, t: (t, 0)),
            pl.BlockSpec((tm, LANES), lambda i, t: (t, 0)),
        ],
        out_specs=[
            pl.BlockSpec((1, tm, N_QKV_BLKS * LANES), lambda i, t: (i, t, 0)),
            pl.BlockSpec((1, tm, 2 * LANES), lambda i, t: (i, t, 0)),
            pl.BlockSpec((1, 8, 2 * LANES), lambda i, t: (i * nt + t, 0, 0)),
            pl.BlockSpec((1, tm, 2 * LANES), lambda i, t: (i, t, 0)),
        ],
        out_shape=[
            jax.ShapeDtypeStruct((b, s, N_QKV_BLKS * LANES), BF16),
            jax.ShapeDtypeStruct((b, s, 2 * LANES), F32),
            jax.ShapeDtypeStruct((b * nt, 8, 2 * LANES), F32),
            jax.ShapeDtypeStruct((b, s, 2 * LANES), F32),
        ],
        compiler_params=pltpu.CompilerParams(
            dimension_semantics=("arbitrary", "arbitrary"), vmem_limit_bytes=VMEM_LIMIT),
        name="in_proj",
    )(x, w_in_bf16, cos_t, sin_t)


def _swa_kernel(sinks_ref, q_ref, k_ref, v_ref, o_ref, kd_ref, vd_ref, *, seq):
    j = pl.program_id(1)
    w = SWA_BLOCK
    nb = seq // w
    chunk = 512

    def prep(c, _):
        rows = pl.ds(pl.multiple_of(c * chunk, chunk), chunk)
        keep = (_lane_iota((chunk, LANES)) // HEAD_DIM) == j
        for src, dst in ((k_ref, kd_ref), (v_ref, vd_ref)):
            t = src[0, rows, :].astype(F32)
            dst[rows, :] = jnp.where(keep, t, pltpu.roll(t, HEAD_DIM, 1)).astype(BF16)
        return 0

    lax.fori_loop(0, seq // chunk, prep, 0)

    lane_lo = _lane_iota((w, LANES)) < HEAD_DIM
    row2 = _row_iota((2 * w, 1))
    sink = jnp.where(row2 < w, sinks_ref[2 * j], sinks_ref[2 * j + 1])
    qrow = _row_iota((2 * w, 2 * w)) % w
    kcol = _lane_iota((2 * w, 2 * w))

    def body(i, _):
        qs = pl.multiple_of(i * w, w)
        ks = pl.multiple_of(jnp.maximum(i - 1, 0) * w, w)
        q2 = q_ref[0, pl.ds(qs, w), :].astype(F32)
        qst = jnp.concatenate([jnp.where(lane_lo, q2, 0.0), jnp.where(lane_lo, 0.0, q2)], axis=0).astype(BF16)
        kk = kd_ref[pl.ds(ks, 2 * w), :]
        vv = vd_ref[pl.ds(ks, 2 * w), :]
        s = _dot_nt(qst, kk)
        dist = (qs + qrow) - (ks + kcol)
        s = jnp.where((dist >= 0) & (dist < SWA_WINDOW), s, NEG)
        m = jnp.maximum(jnp.max(s, axis=-1, keepdims=True), sink)
        p = jnp.exp(s - m)
        den = jnp.sum(p, axis=-1, keepdims=True) + jnp.exp(sink - m)
        p = p * (1.0 / den)
        o = _dot(p.astype(BF16), vv)
        o_ref[0, pl.ds(qs, w), :] = jnp.where(lane_lo, o[:w], o[w:]).astype(o_ref.dtype)
        return 0

    lax.fori_loop(0, nb, body, 0)


def _swa(qkv, sinks):
    b, s, _ = qkv.shape
    kern = functools.partial(_swa_kernel, seq=s)
    return pl.pallas_call(
        kern,
        grid=(b, 2),
        in_specs=[
            pl.BlockSpec(memory_space=pltpu.SMEM),
            pl.BlockSpec((1, s, LANES), lambda i, j: (i, 0, BLK_A_Q + j)),
            pl.BlockSpec((1, s, LANES), lambda i, j: (i, 0, BLK_A_K)),
            pl.BlockSpec((1, s, LANES), lambda i, j: (i, 0, BLK_A_V)),
        ],
        out_specs=pl.BlockSpec((1, s, LANES), lambda i, j: (i, 0, j)),
        out_shape=jax.ShapeDtypeStruct((b, s, 2 * LANES), BF16),
        scratch_shapes=[pltpu.VMEM((s, LANES), BF16), pltpu.VMEM((s, LANES), BF16)],
        compiler_params=pltpu.CompilerParams(
            dimension_semantics=("arbitrary", "arbitrary"), vmem_limit_bytes=VMEM_LIMIT),
        name="swa",
    )(sinks, qkv, qkv, qkv)


def _moba_kernel(q_ref, k_ref, v_ref, q32_ref, kmean_ref, o_ref, ka_ref, vt_ref, qt_ref, *, seq):
    lb = MOBA_BLOCK
    nb = seq // lb

    n_pairs = q_ref.shape[-1] // LANES
    n_heads = 2 * n_pairs
    hd = HEAD_DIM

    def prep(c, _):
        rows = pl.ds(pl.multiple_of(c * lb, lb), lb)
        lane = _lane_iota((lb, LANES))
        ones = jnp.ones((16, lb), BF16)
        for pr in range(n_pairs):
            cols = slice(pr * LANES, (pr + 1) * LANES)
            kt = k_ref[0, rows, cols].astype(F32)
            ka_ref[2 * pr, rows, :] = jnp.where(lane < hd, kt, jnp.where(lane - hd == c, 1.0, 0.0)).astype(BF16)
            ka_ref[2 * pr + 1, rows, :] = jnp.where(lane >= hd, kt, jnp.where(lane == c, 1.0, 0.0)).astype(BF16)
            vt = v_ref[0, rows, cols].astype(F32).T.astype(BF16)
            vt_ref[2 * pr, :, rows] = jnp.concatenate([vt[:hd], ones], axis=0)
            vt_ref[2 * pr + 1, :, rows] = jnp.concatenate([vt[hd:], ones], axis=0)
            qt_ref[cols, rows] = q_ref[0, rows, cols].astype(F32).T.astype(BF16)
        return 0

    lax.fori_loop(0, nb, prep, 0)
    ck = MOBA_CHUNK

    lane_lo_k = _lane_iota((nb, LANES)) < hd
    kmh = []
    for pr in range(n_pairs):
        km = kmean_ref[0, :, pr * LANES:(pr + 1) * LANES]
        kmh += [jnp.where(lane_lo_k, km, 0.0), jnp.where(lane_lo_k, 0.0, km)]
    blk_iota = _row_iota((nb, lb))
    causal_t = [c * ck + _row_iota((ck, lb)) <= _lane_iota((ck, lb)) for c in range(lb // ck)]

    def tile(i, _):
        qs = pl.multiple_of(i * lb, lb)
        valid = blk_iota < i

        bias_t = []
        for h in range(n_heads):
            q32 = q32_ref[0, pl.ds(qs, lb), (h // 2) * LANES:(h // 2 + 1) * LANES]
            gate = _dot_nt(kmh[h], q32, precision=lax.Precision.HIGHEST)
            gm = jnp.where(valid, gate, -jnp.inf)
            rank = jnp.zeros((nb, lb), F32)
            for m_idx in range(nb):
                row = gm[m_idx:m_idx + 1, :]
                tie = jnp.where(blk_iota > m_idx, 1.0, 0.0)
                rank = rank + jnp.where(row > gm, 1.0, jnp.where(row == gm, tie, 0.0))
            routed = jnp.where(rank < MOBA_TOPK, 0.0, NEG)
            bias_t.append(jnp.where(valid, routed, 0.0).astype(BF16))
        pad = jnp.zeros((hd - nb, lb), BF16)
        qa = []
        for pr in range(n_pairs):
            qt = qt_ref[pr * LANES:(pr + 1) * LANES, pl.ds(qs, lb)]
            qa += [jnp.concatenate([qt[:hd], bias_t[2 * pr], pad], axis=0),
                   jnp.concatenate([bias_t[2 * pr + 1], pad, qt[hd:]], axis=0)]

        def local_many(jobs):
            s_all = [_dot(ka_ref[h, pl.ds(ks, ck), :], qa[h]) for h, ks, _ in jobs]
            mp = []
            for s, (_, _, mask) in zip(s_all, jobs):
                if mask is not None:
                    s = jnp.where(mask, s, NEG)
                m = jnp.max(s, axis=0, keepdims=True)
                mp.append((m, jnp.exp(s - m).astype(BF16)))
            return [(m, _dot(vt_ref[h, :, pl.ds(ks, ck)], p))
                    for (m, p), (h, ks, _) in zip(mp, jobs)]

        def merge(parts):
            m_new = parts[0][0]
            for m, _ in parts[1:]:
                m_new = jnp.maximum(m_new, m)
            acc = None
            for m, pv in parts:
                term = pv * jnp.exp(m - m_new)
                acc = term if acc is None else acc + term
            return m_new, acc

        nck = lb // ck
        parts = local_many([(h, pl.multiple_of(qs + c * ck, ck), causal_t[c])
                            for h in range(n_heads) for c in range(nck)])
        state = []
        for h in range(n_heads):
            state += merge(parts[h * nck:(h + 1) * nck])

        def past(n, st):
            parts = local_many([(h, pl.multiple_of(n * lb + c * ck, ck), None)
                                for h in range(n_heads) for c in range(nck)])
            out = []
            for h in range(n_heads):
                out += merge([(st[2 * h], st[2 * h + 1])] + parts[h * nck:(h + 1) * nck])
            return tuple(out)

        st = lax.fori_loop(0, i, past, tuple(state))
        o_t = jnp.concatenate([st[2 * h + 1][:hd] * (1.0 / st[2 * h + 1][hd:hd + 1])
                               for h in range(n_heads)], axis=0)
        o_ref[0, pl.ds(qs, lb), :] = o_t.T.astype(o_ref.dtype)
        return 0

    lax.fori_loop(0, nb, tile, 0)


def _moba(qkv, bq32, kmean):
    b, s, _ = qkv.shape
    nb = s // MOBA_BLOCK
    kern = functools.partial(_moba_kernel, seq=s)
    return pl.pallas_call(
        kern,
        grid=(b,),
        in_specs=[
            pl.BlockSpec((1, s, 2 * LANES), lambda i: (i, 0, BLK_B_Q // 2)),
            pl.BlockSpec((1, s, 2 * LANES), lambda i: (i, 0, BLK_B_K // 2)),
            pl.BlockSpec((1, s, 2 * LANES), lambda i: (i, 0, BLK_B_V // 2)),
            pl.BlockSpec((1, s, 2 * LANES), lambda i: (i, 0, 0)),
            pl.BlockSpec((1, nb, 2 * LANES), lambda i: (i, 0, 0)),
        ],
        out_specs=pl.BlockSpec((1, s, 2 * LANES), lambda i: (i, 0, 0)),
        out_shape=jax.ShapeDtypeStruct((b, s, 2 * LANES), BF16),
        scratch_shapes=[
            pltpu.VMEM((4, s, LANES), BF16),
            pltpu.VMEM((4, HEAD_DIM + 16, s), BF16),
            pltpu.VMEM((2 * LANES, s), BF16),
        ],
        compiler_params=pltpu.CompilerParams(
            dimension_semantics=("arbitrary",), vmem_limit_bytes=VMEM_LIMIT),
        name="moba",
    )(qkv, qkv, qkv, bq32, kmean)


def _sb_kernel(q_ref, k_ref, v_ref, o_ref, km_ref, tri_ref, carry_ref, acc_ref, *, seq):
    w = SB_BLOCK
    nb = seq // w
    win = SB_WINDOW_BLKS * w
    chunk = 512

    n_pairs = q_ref.shape[-1] // LANES
    n_heads = 2 * n_pairs

    def prep(c, _):
        rows = pl.ds(pl.multiple_of(c * chunk, chunk), chunk)
        lane_lo = _lane_iota((chunk, LANES)) < HEAD_DIM
        for pr in range(n_pairs):
            kt = k_ref[0, rows, pr * LANES:(pr + 1) * LANES].astype(F32)
            km_ref[2 * pr, rows, :] = jnp.where(lane_lo, kt, 0.0).astype(BF16)
            km_ref[2 * pr + 1, rows, :] = jnp.where(lane_lo, 0.0, kt).astype(BF16)
        return 0

    lax.fori_loop(0, seq // chunk, prep, 0)

    tri_r = _row_iota((2 * w, 2 * w)) % w
    tri_c = _lane_iota((2 * w, 2 * w))
    tri_ref[...] = jnp.where((tri_c >= w) | (tri_r > tri_c), 1.0, 0.0).astype(BF16)
    lane_lo = _lane_iota((w, LANES)) < HEAD_DIM
    row = _row_iota((w, win))
    kcol = _lane_iota((w, win))

    def tile(i, _):
        qs = pl.multiple_of(i * w, w)
        qpos = qs + row
        carry_ref[...] = jnp.zeros_like(carry_ref)
        acc_ref[...] = jnp.zeros_like(acc_ref)

        def cond(st):
            top, live = st
            return (top > 0) & live

        def body(st):
            top, _ = st
            bottom = jnp.maximum(top - SB_WINDOW_BLKS, 0)
            ws = pl.multiple_of(bottom * w, w)
            mask = (ws + kcol) < jnp.minimum(qpos, top * w)
            tri = tri_ref[...]
            z = [_dot_nt(q_ref[0, pl.ds(qs, w), (h // 2) * LANES:(h // 2 + 1) * LANES],
                         km_ref[h, pl.ds(ws, win), :]) for h in range(n_heads)]
            log_beta = []
            split = []
            for h in range(n_heads):
                soft = jnp.log(1.0 + jnp.exp(-jnp.abs(z[h])))
                lb_h = jnp.minimum(z[h], 0.0) - soft
                log_fail = jnp.where(mask, lb_h - z[h], 0.0)
                log_beta.append(lb_h)
                for c in range(SB_WINDOW_BLKS):
                    lf = log_fail[:, c * w:(c + 1) * w]
                    hi = lf.astype(BF16)
                    split.append(jnp.concatenate([hi, (lf - hi.astype(F32)).astype(BF16)], axis=1))
            r = [_dot(x, tri) for x in split]
            a_all = []
            for h in range(n_heads):
                run = carry_ref[h]
                a = [None] * SB_WINDOW_BLKS
                for c in reversed(range(SB_WINDOW_BLKS)):
                    cols = slice(c * w, (c + 1) * w)
                    rc = r[h * SB_WINDOW_BLKS + c]
                    a[c] = jnp.where(mask[:, cols], jnp.exp(log_beta[h][:, cols] + rc[:, :w] + run), 0.0)
                    run = run + rc[:, w:]
                carry_ref[h] = run
                a_all.append(jnp.concatenate(a, axis=1).astype(BF16))
            live = None
            for h in range(n_heads):
                vv = v_ref[0, pl.ds(ws, win), (h // 2) * LANES:(h // 2 + 1) * LANES]
                acc_ref[h] = acc_ref[h] + _dot(a_all[h], vv)
                top_h = jnp.max(carry_ref[h])
                live = top_h if live is None else jnp.maximum(live, top_h)
            return bottom, live > SB_EXIT

        lax.while_loop(cond, body, (i + 1, True))
        for pr in range(n_pairs):
            o_ref[0, pl.ds(qs, w), pr * LANES:(pr + 1) * LANES] = jnp.where(
                lane_lo, acc_ref[2 * pr], acc_ref[2 * pr + 1]).astype(o_ref.dtype)
        return 0

    lax.fori_loop(0, nb, tile, 0)


def _sb(qkv):
    b, s, _ = qkv.shape
    kern = functools.partial(_sb_kernel, seq=s)
    return pl.pallas_call(
        kern,
        grid=(b,),
        in_specs=[
            pl.BlockSpec((1, s, 2 * LANES), lambda i: (i, 0, BLK_C_Q // 2)),
            pl.BlockSpec((1, s, 2 * LANES), lambda i: (i, 0, BLK_C_K // 2)),
            pl.BlockSpec((1, s, 2 * LANES), lambda i: (i, 0, BLK_C_V // 2)),
        ],
        out_specs=pl.BlockSpec((1, s, 2 * LANES), lambda i: (i, 0, 0)),
        out_shape=jax.ShapeDtypeStruct((b, s, 2 * LANES), BF16),
        scratch_shapes=[
            pltpu.VMEM((4, s, LANES), BF16),
            pltpu.VMEM((2 * SB_BLOCK, 2 * SB_BLOCK), BF16),
            pltpu.VMEM((4, SB_BLOCK, SB_BLOCK), F32),
            pltpu.VMEM((4, SB_BLOCK, LANES), F32),
        ],
        compiler_params=pltpu.CompilerParams(
            dimension_semantics=("arbitrary",), vmem_limit_bytes=VMEM_LIMIT),
        name="stick_breaking",
    )(qkv, qkv, qkv)


def _ret_kernel(q_ref, k_ref, v_ref, g_ref, decay_ref, qw_ref, kw_ref, gc_ref, o_ref, state_ref, *, seq):
    c = RET_CHUNK
    n = seq // c
    lane_lo = _lane_iota((c, LANES)) < HEAD_DIM
    state_ref[...] = jnp.zeros_like(state_ref)

    def chunk(i, _):
        rows = pl.ds(pl.multiple_of(i * c, c), c)
        for p in range(2):
            cols = slice(p * LANES, (p + 1) * LANES)
            q2 = q_ref[0, rows, cols]
            k2 = k_ref[0, rows, cols]
            v2 = v_ref[0, rows, cols]
            k2f = k2.astype(F32)
            inner = []
            for h in range(2):
                kh = jnp.where(lane_lo, k2f, 0.0) if h == 0 else jnp.where(lane_lo, 0.0, k2f)
                sc = _dot_nt(q2, kh.astype(BF16)) * decay_ref[2 * p + h]
                inner.append(_dot(sc.astype(BF16), v2))
            state = state_ref[p]
            cross = _dot(q2, state.astype(BF16)) * qw_ref[p]
            y = jnp.where(lane_lo, inner[0], inner[1]) + cross
            kt = (k2f * kw_ref[p]).T.astype(BF16)
            gcm = gc_ref[p]
            state_ref[p] = state * gcm + jnp.where(gcm > 0.0, _dot(kt, v2), 0.0)

            def half_stat(t):
                lo = jnp.sum(jnp.where(lane_lo, t, 0.0), axis=-1, keepdims=True)
                hi = jnp.sum(jnp.where(lane_lo, 0.0, t), axis=-1, keepdims=True)
                return jnp.where(lane_lo, lo, hi) * (1.0 / HEAD_DIM)

            mu = half_stat(y)
            yc = y - mu
            var = half_stat(yc * yc)
            yn = yc * lax.rsqrt(var + LN_EPS)
            g = g_ref[0, rows, cols]
            gate = g * (1.0 / (1.0 + jnp.exp(-g)))
            o_ref[0, rows, cols] = (yn * gate).astype(o_ref.dtype)
        return 0

    lax.fori_loop(0, n, chunk, 0)


def _ret(qkv, g, decay, qw, kw, gcm):
    b, s, _ = qkv.shape
    kern = functools.partial(_ret_kernel, seq=s)
    full3 = lambda a: pl.BlockSpec(a.shape, lambda i: (0, 0, 0))
    return pl.pallas_call(
        kern,
        grid=(b,),
        in_specs=[
            pl.BlockSpec((1, s, 2 * LANES), lambda i: (i, 0, BLK_D_Q // 2)),
            pl.BlockSpec((1, s, 2 * LANES), lambda i: (i, 0, BLK_D_K // 2)),
            pl.BlockSpec((1, s, 2 * LANES), lambda i: (i, 0, BLK_D_V // 2)),
            pl.BlockSpec((1, s, 2 * LANES), lambda i: (i, 0, 0)),
            full3(decay), full3(qw), full3(kw), full3(gcm),
        ],
        out_specs=pl.BlockSpec((1, s, 2 * LANES), lambda i: (i, 0, 0)),
        out_shape=jax.ShapeDtypeStruct((b, s, 2 * LANES), BF16),
        scratch_shapes=[pltpu.VMEM((2, LANES, LANES), F32)],
        compiler_params=pltpu.CompilerParams(
            dimension_semantics=("arbitrary",), vmem_limit_bytes=VMEM_LIMIT),
        name="retention",
    )(qkv, qkv, qkv, g, decay, qw, kw, gcm)


def _layer_norm(t, w, b):
    mu = jnp.mean(t, axis=-1, keepdims=True)
    tc = t - mu
    var = jnp.mean(tc * tc, axis=-1, keepdims=True)
    return tc * lax.rsqrt(var + LN_EPS) * w + b


def _out_mlp_kernel(ya_ref, yb_ref, yc_ref, yd_ref, x_ref, wo_ref, l1w_ref, l1b_ref,
                    wu_ref, wd_ref, l2w_ref, l2b_ref, o_ref):
    width = ya_ref.shape[-1]
    h = None
    for idx, y_ref in enumerate((ya_ref, yb_ref, yc_ref, yd_ref)):
        part = _dot(y_ref[0], wo_ref[idx * width:(idx + 1) * width, :])
        h = part if h is None else h + part
    x1 = _layer_norm(DEEPNORM_ALPHA * x_ref[0] + h, l1w_ref[...], l1b_ref[...])
    x1b = x1.astype(BF16)
    ff = wu_ref.shape[1]
    step = 1024
    acc = None
    for c in range(ff // step):
        hid = _dot(x1b, wu_ref[:, c * step:(c + 1) * step])
        hid = jnp.square(jnp.maximum(hid, 0.0)).astype(BF16)
        part = _dot(hid, wd_ref[c * step:(c + 1) * step, :])
        acc = part if acc is None else acc + part
    o_ref[0] = _layer_norm(DEEPNORM_ALPHA * x1 + acc, l2w_ref[...], l2b_ref[...])


def _out_mlp(ya, yb, yc, yd, x, wo, l1w, l1b, wu, wd, l2w, l2b, *, tm):
    b, s, d = x.shape
    nt = s // tm
    ytile = pl.BlockSpec((1, tm, ya.shape[-1]), lambda i, t: (i, t, 0))
    const = lambda a: pl.BlockSpec(a.shape, lambda i, t: (0, 0), pipeline_mode=pl.Buffered(1))
    return pl.pallas_call(
        _out_mlp_kernel,
        grid=(b, nt),
        in_specs=[ytile, ytile, ytile, ytile,
                  pl.BlockSpec((1, tm, d), lambda i, t: (i, t, 0)),
                  const(wo), const(l1w), const(l1b), const(wu), const(wd), const(l2w), const(l2b)],
        out_specs=pl.BlockSpec((1, tm, d), lambda i, t: (i, t, 0)),
        out_shape=jax.ShapeDtypeStruct((b, s, d), F32),
        compiler_params=pltpu.CompilerParams(
            dimension_semantics=("arbitrary", "arbitrary"), vmem_limit_bytes=VMEM_LIMIT),
        name="out_mlp",
    )(ya, yb, yc, yd, x, wo, l1w, l1b, wu, wd, l2w, l2b)


def _rope_tables(s):
    half = HEAD_DIM // 2
    inv = 1.0 / (ROPE_THETA ** (jnp.arange(0, HEAD_DIM, 2, dtype=F32) / HEAD_DIM))
    ang = jnp.arange(s, dtype=F32)[:, None] * inv[None, :]
    cos = jnp.cos(ang)
    sin = jnp.sin(ang)
    cos_t = jnp.tile(cos, (1, LANES // half))
    sin_t = jnp.tile(jnp.concatenate([-sin, sin], axis=1), (1, LANES // HEAD_DIM))
    return cos_t, sin_t


def _retention_tables():
    c = RET_CHUNK
    h = RET_HEADS
    log_g = jnp.log(1.0 - 2.0 ** (-5.0 - jnp.arange(h, dtype=F32)))
    idx = jnp.arange(c, dtype=F32)
    diff = idx[:, None] - idx[None, :]
    decay = jnp.where(diff >= 0, jnp.exp(jnp.maximum(diff, 0.0)[None] * log_g[:, None, None]), 0.0)
    k_w = jnp.exp((c - 1.0 - idx)[:, None] * log_g[None, :])
    q_w = jnp.exp((idx + 1.0)[:, None] * log_g[None, :])
    g_c = jnp.exp(c * log_g)
    per_lane = lambda t: jnp.repeat(t, HEAD_DIM, axis=1).reshape(c, h // 2, LANES).transpose(1, 0, 2)
    head_of = jnp.arange(LANES) // HEAD_DIM
    same = head_of[:, None] == head_of[None, :]
    gcm = jnp.stack([jnp.where(same, g_c[2 * p + head_of][None, :], 0.0) for p in range(h // 2)])
    return decay, per_lane(q_w), per_lane(k_w), gcm


def kernel(x, w_in, swa_sinks, w_out, ln1_w, ln1_b, w_up, w_down, ln2_w, ln2_b):
    b, s, d = x.shape
    cos_t, sin_t = _rope_tables(s)
    decay, qw, kw, gcm = _retention_tables()
    row = lambda t: t.reshape(1, -1)
    tm_in = 512
    tm_out = 512
    for layer in range(w_in.shape[0]):
        qkv, bq32, kmean8, g = _in_proj(x, w_in[layer].astype(BF16), cos_t, sin_t, tm=tm_in)
        nblk = tm_in // MOBA_BLOCK
        kmean = kmean8.reshape(b, s // tm_in, 8, 2 * LANES)[:, :, :nblk].reshape(b, s // MOBA_BLOCK, 2 * LANES)
        ya = _swa(qkv, swa_sinks[layer])
        yb = _moba(qkv, bq32, kmean)
        yc = _sb(qkv)
        yd = _ret(qkv, g, decay, qw, kw, gcm)
        x = _out_mlp(ya, yb, yc, yd, x, w_out[layer].astype(BF16), row(ln1_w[layer]), row(ln1_b[layer]),
                     w_up[layer].astype(BF16), w_down[layer].astype(BF16), row(ln2_w[layer]), row(ln2_b[layer]),
                     tm=tm_out)
    return x
```

```python
import functools
import math

import jax
import jax.numpy as jnp
from jax import lax
from jax.experimental import pallas as pl
from jax.experimental.pallas import tpu as pltpu

F32 = jnp.float32
BF16 = jnp.bfloat16

D_MODEL = 1024
HEAD_DIM = 64
LANES = 128
D_FF = 4 * D_MODEL
DEPTH = 2
ROPE_THETA = 10000.0
LN_EPS = 1e-5
NEG = -1e30
DEEPNORM_ALPHA = (2 * DEPTH) ** 0.25
QK_SCALE = HEAD_DIM ** -0.5

SWA_BLOCK = 128
SWA_WINDOW = 128
SWA_BLOCKS_PER_STEP = 2
MOBA_BLOCK = 256
MOBA_CHUNK = 128
MOBA_TOPK = 3
SB_BLOCK = 128
SB_WINDOW_BLKS = 4
RET_CHUNK = 128
RET_HEADS = 4
RET_CHUNKS_PER_STEP = 4

BLK_A_Q, BLK_A_K, BLK_A_V = 0, 2, 3
BLK_B_Q, BLK_B_K, BLK_B_V = 4, 6, 8
BLK_C_Q, BLK_C_K, BLK_C_V = 10, 12, 14
BLK_D_Q, BLK_D_K, BLK_D_V, BLK_D_G = 16, 18, 20, 22
N_QKV_BLKS = 22
ROPED_BLKS = frozenset([0, 1, 2, 4, 5, 6, 7, 16, 17, 18, 19])
SCALED_BLKS = frozenset([0, 1, 4, 5, 10, 11, 18, 19])

SB_EXIT = -110.0

VMEM_LIMIT = 56 * 1024 * 1024


def _dot(a, b, **kw):
    return jnp.dot(a, b, preferred_element_type=F32, **kw)


def _dot_nt(a, b, **kw):
    return lax.dot_general(a, b, (((1,), (1,)), ((), ())), preferred_element_type=F32, **kw)


def _lane_iota(shape):
    return lax.broadcasted_iota(jnp.int32, shape, len(shape) - 1)


def _row_iota(shape):
    return lax.broadcasted_iota(jnp.int32, shape, len(shape) - 2)


def _in_proj_kernel(x_ref, w_ref, cos_ref, sin_ref, qkv_ref, bq_ref, kmean_ref, g_ref, *, tm):
    xb = x_ref[0].astype(BF16)
    cos = cos_ref[...]
    sin = sin_ref[...]
    first_half = (_lane_iota((tm, LANES)) % HEAD_DIM) < (HEAD_DIM // 2)
    n_pairs = w_ref.shape[1] // (2 * LANES)
    for c in range(n_pairs):
        acc2 = _dot(xb, w_ref[:, c * 2 * LANES:(c + 1) * 2 * LANES])
        for half in range(2):
            blk = 2 * c + half
            acc = acc2[:, half * LANES:(half + 1) * LANES]
            if blk in ROPED_BLKS:
                partner = jnp.where(first_half,
                                    pltpu.roll(acc, LANES - HEAD_DIM // 2, 1),
                                    pltpu.roll(acc, HEAD_DIM // 2, 1))
                acc = acc * cos + partner * sin
            if blk in (BLK_B_Q, BLK_B_Q + 1):
                o = (blk - BLK_B_Q) * LANES
                bq_ref[0, :, o:o + LANES] = acc
            if blk in (BLK_B_K, BLK_B_K + 1):
                o = (blk - BLK_B_K) * LANES
                nblk = tm // MOBA_BLOCK
                means = [jnp.sum(acc[i * MOBA_BLOCK:(i + 1) * MOBA_BLOCK], axis=0, keepdims=True)
                         * (1.0 / MOBA_BLOCK) for i in range(nblk)]
                means.append(jnp.zeros((8 - nblk, LANES), F32))
                kmean_ref[0, :, o:o + LANES] = jnp.concatenate(means, axis=0)
            if blk in (BLK_D_G, BLK_D_G + 1):
                o = (blk - BLK_D_G) * LANES
                g_ref[0, :, o:o + LANES] = acc
            if blk < N_QKV_BLKS:
                if blk in SCALED_BLKS:
                    acc = acc * QK_SCALE
                qkv_ref[0, :, blk * LANES:(blk + 1) * LANES] = acc.astype(BF16)


def _in_proj(x, w_in_bf16, cos_t, sin_t, *, tm):
    b, s, d = x.shape
    nt = s // tm
    n_in = w_in_bf16.shape[1]
    kern = functools.partial(_in_proj_kernel, tm=tm)
    return pl.pallas_call(
        kern,
        grid=(b, nt),
        in_specs=[
            pl.BlockSpec((1, tm, d), lambda i, t: (i, t, 0)),
            pl.BlockSpec((d, n_in), lambda i, t: (0, 0), pipeline_mode=pl.Buffered(1)),
            pl.BlockSpec((tm, LANES), lambda i, t: (t, 0)),
            pl.BlockSpec((tm, LANES), lambda i, t: (t, 0)),
        ],
        out_specs=[
            pl.BlockSpec((1, tm, N_QKV_BLKS * LANES), lambda i, t: (i, t, 0)),
            pl.BlockSpec((1, tm, 2 * LANES), lambda i, t: (i, t, 0)),
            pl.BlockSpec((1, 8, 2 * LANES), lambda i, t: (i * nt + t, 0, 0)),
            pl.BlockSpec((1, tm, 2 * LANES), lambda i, t: (i, t, 0)),
        ],
        out_shape=[
            jax.ShapeDtypeStruct((b, s, N_QKV_BLKS * LANES), BF16),
            jax.ShapeDtypeStruct((b, s, 2 * LANES), F32),
            jax.ShapeDtypeStruct((b * nt, 8, 2 * LANES), F32),
            jax.ShapeDtypeStruct((b, s, 2 * LANES), F32),
        ],
        compiler_params=pltpu.CompilerParams(
            dimension_semantics=("arbitrary", "arbitrary"), vmem_limit_bytes=VMEM_LIMIT),
        name="in_proj",
    )(x, w_in_bf16, cos_t, sin_t)


def _swa_kernel(sinks_ref, q_ref, k_ref, v_ref, o_ref, qt_ref, vt_ref, bias_ref, *, seq):
    w = SWA_BLOCK
    nb = seq // w
    hd = HEAD_DIM
    n_kv = k_ref.shape[-1] // hd
    chunk = 2 * w

    def prep(c, _):
        rows = pl.ds(pl.multiple_of(c * chunk, chunk), chunk)
        qt_ref[:, rows] = q_ref[0, rows, :].astype(F32).T.astype(BF16)
        vt = v_ref[0, rows, :].astype(F32).T.astype(BF16)
        ones = jnp.ones((16, chunk), BF16)
        for kv in range(n_kv):
            vt_ref[kv, :, rows] = jnp.concatenate([vt[kv * hd:(kv + 1) * hd], ones], axis=0)
        return 0

    lax.fori_loop(0, seq // chunk, prep, 0)

    delta = (_lane_iota((2 * w, 2 * w)) % w) - _row_iota((2 * w, 2 * w))
    for t in range(2):
        dist = delta + t * w
        bias_ref[t] = jnp.where((dist >= 0) & (dist < SWA_WINDOW), 0.0, NEG)
    first_head = _lane_iota((1, 2 * w)) < w
    sink = [jnp.where(first_head, sinks_ref[2 * kv], sinks_ref[2 * kv + 1]) for kv in range(n_kv)]
    zeros = jnp.zeros((hd, 2 * w), BF16)

    def body(it, _):
        jobs = []
        for u in range(SWA_BLOCKS_PER_STEP):
            i = it * SWA_BLOCKS_PER_STEP + u
            qs = pl.multiple_of(i * w, w)
            ks = pl.multiple_of(jnp.maximum(i - 1, 0) * w, w)
            for kv in range(n_kv):
                qq = jnp.concatenate([qt_ref[(2 * kv) * hd:(2 * kv + 1) * hd, pl.ds(qs, w)],
                                      qt_ref[(2 * kv + 1) * hd:(2 * kv + 2) * hd, pl.ds(qs, w)]], axis=1)
                rhs = jnp.concatenate([qq, zeros] if kv == 0 else [zeros, qq], axis=0)
                jobs.append((qs, ks, jnp.minimum(i, 1), kv, rhs))
        s_all = [_dot(k_ref[0, pl.ds(ks, 2 * w), :], rhs) for _, ks, _, _, rhs in jobs]
        mp = []
        for s, (_, _, t, kv, _) in zip(s_all, jobs):
            s = s + bias_ref[t]
            m = jnp.maximum(jnp.max(s, axis=0, keepdims=True), sink[kv])
            mp.append((m, jnp.exp(s - m).astype(BF16)))
        pv_all = [_dot(vt_ref[kv, :, pl.ds(ks, 2 * w)], p) for (_, p), (_, ks, _, kv, _) in zip(mp, jobs)]
        for pv, (m, _), (qs, _, _, kv, _) in zip(pv_all, mp, jobs):
            den = pv[hd:hd + 1] + jnp.exp(sink[kv] - m)
            o = pv[:hd] * (1.0 / den)
            o_t = jnp.concatenate([o[:, :w], o[:, w:]], axis=0)
            o_ref[0, pl.ds(qs, w), kv * LANES:(kv + 1) * LANES] = o_t.T.astype(o_ref.dtype)
        return 0

    lax.fori_loop(0, nb // SWA_BLOCKS_PER_STEP, body, 0)


def _swa(qkv, sinks):
    b, s, _ = qkv.shape
    kern = functools.partial(_swa_kernel, seq=s)
    return pl.pallas_call(
        kern,
        grid=(b,),
        in_specs=[
            pl.BlockSpec(memory_space=pltpu.SMEM),
            pl.BlockSpec((1, s, 2 * LANES), lambda i: (i, 0, BLK_A_Q // 2)),
            pl.BlockSpec((1, s, LANES), lambda i: (i, 0, BLK_A_K)),
            pl.BlockSpec((1, s, LANES), lambda i: (i, 0, BLK_A_V)),
        ],
        out_specs=pl.BlockSpec((1, s, 2 * LANES), lambda i: (i, 0, 0)),
        out_shape=jax.ShapeDtypeStruct((b, s, 2 * LANES), BF16),
        scratch_shapes=[
            pltpu.VMEM((2 * LANES, s), BF16),
            pltpu.VMEM((2, HEAD_DIM + 16, s), BF16),
            pltpu.VMEM((2, 2 * SWA_BLOCK, 2 * SWA_BLOCK), F32),
        ],
        compiler_params=pltpu.CompilerParams(
            dimension_semantics=("arbitrary",), vmem_limit_bytes=VMEM_LIMIT),
        name="swa",
    )(sinks, qkv, qkv, qkv)


def _moba_kernel(q_ref, k_ref, v_ref, q32_ref, kmean_ref, o_ref, ka_ref, vt_ref, qt_ref, *, seq):
    lb = MOBA_BLOCK
    nb = seq // lb

    n_pairs = q_ref.shape[-1] // LANES
    n_heads = 2 * n_pairs
    hd = HEAD_DIM

    def prep(c, _):
        rows = pl.ds(pl.multiple_of(c * lb, lb), lb)
        lane = _lane_iota((lb, LANES))
        ones = jnp.ones((16, lb), BF16)
        for pr in range(n_pairs):
            cols = slice(pr * LANES, (pr + 1) * LANES)
            kt = k_ref[0, rows, cols].astype(F32)
            ka_ref[2 * pr, rows, :] = jnp.where(lane < hd, kt, jnp.where(lane - hd == c, 1.0, 0.0)).astype(BF16)
            ka_ref[2 * pr + 1, rows, :] = jnp.where(lane >= hd, kt, jnp.where(lane == c, 1.0, 0.0)).astype(BF16)
            vt = v_ref[0, rows, cols].astype(F32).T.astype(BF16)
            vt_ref[2 * pr, :, rows] = jnp.concatenate([vt[:hd], ones], axis=0)
            vt_ref[2 * pr + 1, :, rows] = jnp.concatenate([vt[hd:], ones], axis=0)
            qt_ref[cols, rows] = q_ref[0, rows, cols].astype(F32).T.astype(BF16)
        return 0

    lax.fori_loop(0, nb, prep, 0)
    ck = MOBA_CHUNK

    lane_lo_k = _lane_iota((nb, LANES)) < hd
    kmh = []
    for pr in range(n_pairs):
        km = kmean_ref[0, :, pr * LANES:(pr + 1) * LANES]
        kmh += [jnp.where(lane_lo_k, km, 0.0), jnp.where(lane_lo_k, 0.0, km)]
    blk_iota = _row_iota((nb, lb))
    causal_t = [c * ck + _row_iota((ck, lb)) <= _lane_iota((ck, lb)) for c in range(lb // ck)]

    def tile(i, _):
        qs = pl.multiple_of(i * lb, lb)
        valid = blk_iota < i

        bias_t = []
        for h in range(n_heads):
            q32 = q32_ref[0, pl.ds(qs, lb), (h // 2) * LANES:(h // 2 + 1) * LANES]
            gate = _dot_nt(kmh[h], q32, precision=lax.Precision.HIGHEST)
            gm = jnp.where(valid, gate, -jnp.inf)
            rank = jnp.zeros((nb, lb), F32)
            for m_idx in range(nb):
                row = gm[m_idx:m_idx + 1, :]
                tie = jnp.where(blk_iota > m_idx, 1.0, 0.0)
                rank = rank + jnp.where(row > gm, 1.0, jnp.where(row == gm, tie, 0.0))
            routed = jnp.where(rank < MOBA_TOPK, 0.0, NEG)
            bias_t.append(jnp.where(valid, routed, 0.0).astype(BF16))
        pad = jnp.zeros((hd - nb, lb), BF16)
        qa = []
        for pr in range(n_pairs):
            qt = qt_ref[pr * LANES:(pr + 1) * LANES, pl.ds(qs, lb)]
            qa += [jnp.concatenate([qt[:hd], bias_t[2 * pr], pad], axis=0),
                   jnp.concatenate([bias_t[2 * pr + 1], pad, qt[hd:]], axis=0)]

        def local_many(jobs):
            s_all = [_dot(ka_ref[h, pl.ds(ks, ck), :], qa[h]) for h, ks, _ in jobs]
            mp = []
            for s, (_, _, mask) in zip(s_all, jobs):
                if mask is not None:
                    s = jnp.where(mask, s, NEG)
                m = jnp.max(s, axis=0, keepdims=True)
                mp.append((m, jnp.exp(s - m).astype(BF16)))
            return [(m, _dot(vt_ref[h, :, pl.ds(ks, ck)], p))
                    for (m, p), (h, ks, _) in zip(mp, jobs)]

        def merge(parts):
            m_new = parts[0][0]
            for m, _ in parts[1:]:
                m_new = jnp.maximum(m_new, m)
            acc = None
            for m, pv in parts:
                term = pv * jnp.exp(m - m_new)
                acc = term if acc is None else acc + term
            return m_new, acc

        nck = lb // ck
        parts = local_many([(h, pl.multiple_of(qs + c * ck, ck), causal_t[c])
                            for h in range(n_heads) for c in range(nck)])
        state = []
        for h in range(n_heads):
            state += merge(parts[h * nck:(h + 1) * nck])

        def past(n, st):
            parts = local_many([(h, pl.multiple_of(n * lb + c * ck, ck), None)
                                for h in range(n_heads) for c in range(nck)])
            out = []
            for h in range(n_heads):
                out += merge([(st[2 * h], st[2 * h + 1])] + parts[h * nck:(h + 1) * nck])
            return tuple(out)

        st = lax.fori_loop(0, i, past, tuple(state))
        o_t = jnp.concatenate([st[2 * h + 1][:hd] * (1.0 / st[2 * h + 1][hd:hd + 1])
                               for h in range(n_heads)], axis=0)
        o_ref[0, pl.ds(qs, lb), :] = o_t.T.astype(o_ref.dtype)
        return 0

    lax.fori_loop(0, nb, tile, 0)


def _moba(qkv, bq32, kmean):
    b, s, _ = qkv.shape
    nb = s // MOBA_BLOCK
    kern = functools.partial(_moba_kernel, seq=s)
    return pl.pallas_call(
        kern,
        grid=(b,),
        in_specs=[
            pl.BlockSpec((1, s, 2 * LANES), lambda i: (i, 0, BLK_B_Q // 2)),
            pl.BlockSpec((1, s, 2 * LANES), lambda i: (i, 0, BLK_B_K // 2)),
            pl.BlockSpec((1, s, 2 * LANES), lambda i: (i, 0, BLK_B_V // 2)),
            pl.BlockSpec((1, s, 2 * LANES), lambda i: (i, 0, 0)),
            pl.BlockSpec((1, nb, 2 * LANES), lambda i: (i, 0, 0)),
        ],
        out_specs=pl.BlockSpec((1, s, 2 * LANES), lambda i: (i, 0, 0)),
        out_shape=jax.ShapeDtypeStruct((b, s, 2 * LANES), BF16),
        scratch_shapes=[
            pltpu.VMEM((4, s, LANES), BF16),
            pltpu.VMEM((4, HEAD_DIM + 16, s), BF16),
            pltpu.VMEM((2 * LANES, s), BF16),
        ],
        compiler_params=pltpu.CompilerParams(
            dimension_semantics=("arbitrary",), vmem_limit_bytes=VMEM_LIMIT),
        name="moba",
    )(qkv, qkv, qkv, bq32, kmean)


def _sb_kernel(q_ref, k_ref, v_ref, o_ref, km_ref, tri_ref, carry_ref, acc_ref, *, seq):
    w = SB_BLOCK
    nb = seq // w
    win = SB_WINDOW_BLKS * w
    chunk = 512

    n_pairs = q_ref.shape[-1] // LANES
    n_heads = 2 * n_pairs

    def prep(c, _):
        rows = pl.ds(pl.multiple_of(c * chunk, chunk), chunk)
        lane_lo = _lane_iota((chunk, LANES)) < HEAD_DIM
        for pr in range(n_pairs):
            kt = k_ref[0, rows, pr * LANES:(pr + 1) * LANES].astype(F32)
            km_ref[2 * pr, rows, :] = jnp.where(lane_lo, kt, 0.0).astype(BF16)
            km_ref[2 * pr + 1, rows, :] = jnp.where(lane_lo, 0.0, kt).astype(BF16)
        return 0

    lax.fori_loop(0, seq // chunk, prep, 0)

    tri_r = _row_iota((2 * w, 2 * w)) % w
    tri_c = _lane_iota((2 * w, 2 * w))
    tri_ref[...] = jnp.where((tri_c >= w) | (tri_r > tri_c), 1.0, 0.0).astype(BF16)
    lane_lo = _lane_iota((w, LANES)) < HEAD_DIM
    row = _row_iota((w, win))
    kcol = _lane_iota((w, win))

    def tile(i, _):
        qs = pl.multiple_of(i * w, w)
        qpos = qs + row
        carry_ref[...] = jnp.zeros_like(carry_ref)
        acc_ref[...] = jnp.zeros_like(acc_ref)

        def cond(st):
            top, live = st
            return (top > 0) & live

        def body(st):
            top, _ = st
            bottom = jnp.maximum(top - SB_WINDOW_BLKS, 0)
            ws = pl.multiple_of(bottom * w, w)
            mask = (ws + kcol) < jnp.minimum(qpos, top * w)
            tri = tri_ref[...]
            z = [_dot_nt(q_ref[0, pl.ds(qs, w), (h // 2) * LANES:(h // 2 + 1) * LANES],
                         km_ref[h, pl.ds(ws, win), :]) for h in range(n_heads)]
            log_beta = []
            split = []
            for h in range(n_heads):
                soft = jnp.log(1.0 + jnp.exp(-jnp.abs(z[h])))
                lb_h = jnp.minimum(z[h], 0.0) - soft
                log_fail = jnp.where(mask, lb_h - z[h], 0.0)
                log_beta.append(lb_h)
                for c in range(SB_WINDOW_BLKS):
                    lf = log_fail[:, c * w:(c + 1) * w]
                    hi = lf.astype(BF16)
                    split.append(jnp.concatenate([hi, (lf - hi.astype(F32)).astype(BF16)], axis=1))
            r = [_dot(x, tri) for x in split]
            a_all = []
            for h in range(n_heads):
                run = carry_ref[h]
                a = [None] * SB_WINDOW_BLKS
                for c in reversed(range(SB_WINDOW_BLKS)):
                    cols = slice(c * w, (c + 1) * w)
                    rc = r[h * SB_WINDOW_BLKS + c]
                    a[c] = jnp.where(mask[:, cols], jnp.exp(log_beta[h][:, cols] + rc[:, :w] + run), 0.0)
                    run = run + rc[:, w:]
                carry_ref[h] = run
                a_all.append(jnp.concatenate(a, axis=1).astype(BF16))
            live = None
            for h in range(n_heads):
                vv = v_ref[0, pl.ds(ws, win), (h // 2) * LANES:(h // 2 + 1) * LANES]
                acc_ref[h] = acc_ref[h] + _dot(a_all[h], vv)
                top_h = jnp.max(carry_ref[h])
                live = top_h if live is None else jnp.maximum(live, top_h)
            return bottom, live > SB_EXIT

        lax.while_loop(cond, body, (i + 1, True))
        for pr in range(n_pairs):
            o_ref[0, pl.ds(qs, w), pr * LANES:(pr + 1) * LANES] = jnp.where(
                lane_lo, acc_ref[2 * pr], acc_ref[2 * pr + 1]).astype(o_ref.dtype)
        return 0

    lax.fori_loop(0, nb, tile, 0)


def _sb(qkv):
    b, s, _ = qkv.shape
    kern = functools.partial(_sb_kernel, seq=s)
    return pl.pallas_call(
        kern,
        grid=(b,),
        in_specs=[
            pl.BlockSpec((1, s, 2 * LANES), lambda i: (i, 0, BLK_C_Q // 2)),
            pl.BlockSpec((1, s, 2 * LANES), lambda i: (i, 0, BLK_C_K // 2)),
            pl.BlockSpec((1, s, 2 * LANES), lambda i: (i, 0, BLK_C_V // 2)),
        ],
        out_specs=pl.BlockSpec((1, s, 2 * LANES), lambda i: (i, 0, 0)),
        out_shape=jax.ShapeDtypeStruct((b, s, 2 * LANES), BF16),
        scratch_shapes=[
            pltpu.VMEM((4, s, LANES), BF16),
            pltpu.VMEM((2 * SB_BLOCK, 2 * SB_BLOCK), BF16),
            pltpu.VMEM((4, SB_BLOCK, SB_BLOCK), F32),
            pltpu.VMEM((4, SB_BLOCK, LANES), F32),
        ],
        compiler_params=pltpu.CompilerParams(
            dimension_semantics=("arbitrary",), vmem_limit_bytes=VMEM_LIMIT),
        name="stick_breaking",
    )(qkv, qkv, qkv)


def _ret_kernel(q_ref, k_ref, v_ref, g_ref, decay_ref, qw_ref, kw_ref, gc_ref, o_ref, state_ref, *, seq):
    c = RET_CHUNK
    n = seq // c
    lane_lo = _lane_iota((c, LANES)) < HEAD_DIM
    state_ref[...] = jnp.zeros_like(state_ref)

    def split_heads(t):
        return jnp.concatenate([jnp.where(lane_lo, t, 0.0), jnp.where(lane_lo, 0.0, t)], axis=0).astype(BF16)

    def half_stat(t):
        lo = jnp.sum(jnp.where(lane_lo, t, 0.0), axis=-1, keepdims=True)
        hi = jnp.sum(jnp.where(lane_lo, 0.0, t), axis=-1, keepdims=True)
        return jnp.where(lane_lo, lo, hi) * (1.0 / HEAD_DIM)

    def step(it, _):
        jobs = [(u, p) for u in range(RET_CHUNKS_PER_STEP) for p in range(2)]
        rows = [pl.ds(pl.multiple_of((it * RET_CHUNKS_PER_STEP + u) * c, c), c) for u in range(RET_CHUNKS_PER_STEP)]
        cols = [slice(p * LANES, (p + 1) * LANES) for p in range(2)]
        q2 = {j: q_ref[0, rows[j[0]], cols[j[1]]] for j in jobs}
        k2f = {j: k_ref[0, rows[j[0]], cols[j[1]]].astype(F32) for j in jobs}
        v2 = {j: v_ref[0, rows[j[0]], cols[j[1]]] for j in jobs}
        scores = {j: _dot_nt(q2[j], split_heads(k2f[j])) for j in jobs}
        kv = {j: _dot((k2f[j] * kw_ref[j[1]]).T.astype(BF16), v2[j]) for j in jobs}
        state_before = {}
        for p in range(2):
            gcm = gc_ref[p]
            st = state_ref[p]
            for u in range(RET_CHUNKS_PER_STEP):
                state_before[(u, p)] = st
                st = st * gcm + jnp.where(gcm > 0.0, kv[(u, p)], 0.0)
            state_ref[p] = st
        inner = {j: _dot((scores[j] * decay_ref[j[1]]).astype(BF16), split_heads(v2[j].astype(F32))) for j in jobs}
        cross = {j: _dot(q2[j], state_before[j].astype(BF16)) for j in jobs}
        for j in jobs:
            u, p = j
            y = inner[j] + cross[j] * qw_ref[p]
            mu = half_stat(y)
            yc = y - mu
            var = half_stat(yc * yc)
            yn = yc * lax.rsqrt(var + LN_EPS)
            g = g_ref[0, rows[u], cols[p]]
            gate = g * (1.0 / (1.0 + jnp.exp(-g)))
            o_ref[0, rows[u], cols[p]] = (yn * gate).astype(o_ref.dtype)
        return 0

    lax.fori_loop(0, n // RET_CHUNKS_PER_STEP, step, 0)


def _ret(qkv, g, decay, qw, kw, gcm):
    b, s, _ = qkv.shape
    kern = functools.partial(_ret_kernel, seq=s)
    full3 = lambda a: pl.BlockSpec(a.shape, lambda i: (0, 0, 0))
    return pl.pallas_call(
        kern,
        grid=(b,),
        in_specs=[
            pl.BlockSpec((1, s, 2 * LANES), lambda i: (i, 0, BLK_D_Q // 2)),
            pl.BlockSpec((1, s, 2 * LANES), lambda i: (i, 0, BLK_D_K // 2)),
            pl.BlockSpec((1, s, 2 * LANES), lambda i: (i, 0, BLK_D_V // 2)),
            pl.BlockSpec((1, s, 2 * LANES), lambda i: (i, 0, 0)),
            full3(decay), full3(qw), full3(kw), full3(gcm),
        ],
        out_specs=pl.BlockSpec((1, s, 2 * LANES), lambda i: (i, 0, 0)),
        out_shape=jax.ShapeDtypeStruct((b, s, 2 * LANES), BF16),
        scratch_shapes=[pltpu.VMEM((2, LANES, LANES), F32)],
        compiler_params=pltpu.CompilerParams(
            dimension_semantics=("arbitrary",), vmem_limit_bytes=VMEM_LIMIT),
        name="retention",
    )(qkv, qkv, qkv, g, decay, qw, kw, gcm)


def _layer_norm(t, w, b):
    mu = jnp.mean(t, axis=-1, keepdims=True)
    tc = t - mu
    var = jnp.mean(tc * tc, axis=-1, keepdims=True)
    return tc * lax.rsqrt(var + LN_EPS) * w + b


def _out_mlp_kernel(ya_ref, yb_ref, yc_ref, yd_ref, x_ref, wo_ref, l1w_ref, l1b_ref,
                    wu_ref, wd_ref, l2w_ref, l2b_ref, o_ref):
    width = ya_ref.shape[-1]
    h = None
    for idx, y_ref in enumerate((ya_ref, yb_ref, yc_ref, yd_ref)):
        part = _dot(y_ref[0], wo_ref[idx * width:(idx + 1) * width, :])
        h = part if h is None else h + part
    x1 = _layer_norm(DEEPNORM_ALPHA * x_ref[0] + h, l1w_ref[...], l1b_ref[...])
    x1b = x1.astype(BF16)
    ff = wu_ref.shape[1]
    step = 1024
    acc = None
    for c in range(ff // step):
        hid = _dot(x1b, wu_ref[:, c * step:(c + 1) * step])
        hid = jnp.square(jnp.maximum(hid, 0.0)).astype(BF16)
        part = _dot(hid, wd_ref[c * step:(c + 1) * step, :])
        acc = part if acc is None else acc + part
    o_ref[0] = _layer_norm(DEEPNORM_ALPHA * x1 + acc, l2w_ref[...], l2b_ref[...])


def _out_mlp(ya, yb, yc, yd, x, wo, l1w, l1b, wu, wd, l2w, l2b, *, tm):
    b, s, d = x.shape
    nt = s // tm
    ytile = pl.BlockSpec((1, tm, ya.shape[-1]), lambda i, t: (i, t, 0))
    const = lambda a: pl.BlockSpec(a.shape, lambda i, t: (0, 0), pipeline_mode=pl.Buffered(1))
    return pl.pallas_call(
        _out_mlp_kernel,
        grid=(b, nt),
        in_specs=[ytile, ytile, ytile, ytile,
                  pl.BlockSpec((1, tm, d), lambda i, t: (i, t, 0)),
                  const(wo), const(l1w), const(l1b), const(wu), const(wd), const(l2w), const(l2b)],
        out_specs=pl.BlockSpec((1, tm, d), lambda i, t: (i, t, 0)),
        out_shape=jax.ShapeDtypeStruct((b, s, d), F32),
        compiler_params=pltpu.CompilerParams(
            dimension_semantics=("arbitrary", "arbitrary"), vmem_limit_bytes=VMEM_LIMIT),
        name="out_mlp",
    )(ya, yb, yc, yd, x, wo, l1w, l1b, wu, wd, l2w, l2b)


def _rope_tables(s):
    half = HEAD_DIM // 2
    inv = 1.0 / (ROPE_THETA ** (jnp.arange(0, HEAD_DIM, 2, dtype=F32) / HEAD_DIM))
    ang = jnp.arange(s, dtype=F32)[:, None] * inv[None, :]
    cos = jnp.cos(ang)
    sin = jnp.sin(ang)
    cos_t = jnp.tile(cos, (1, LANES // half))
    sin_t = jnp.tile(jnp.concatenate([-sin, sin], axis=1), (1, LANES // HEAD_DIM))
    return cos_t, sin_t


def _retention_tables():
    c = RET_CHUNK
    h = RET_HEADS
    log_g = jnp.log(1.0 - 2.0 ** (-5.0 - jnp.arange(h, dtype=F32)))
    idx = jnp.arange(c, dtype=F32)
    diff = idx[:, None] - idx[None, :]
    decay = jnp.where(diff >= 0, jnp.exp(jnp.maximum(diff, 0.0)[None] * log_g[:, None, None]), 0.0)
    k_w = jnp.exp((c - 1.0 - idx)[:, None] * log_g[None, :])
    q_w = jnp.exp((idx + 1.0)[:, None] * log_g[None, :])
    g_c = jnp.exp(c * log_g)
    per_lane = lambda t: jnp.repeat(t, HEAD_DIM, axis=1).reshape(c, h // 2, LANES).transpose(1, 0, 2)
    head_of = jnp.arange(LANES) // HEAD_DIM
    same = head_of[:, None] == head_of[None, :]
    gcm = jnp.stack([jnp.where(same, g_c[2 * p + head_of][None, :], 0.0) for p in range(h // 2)])
    decay_pairs = jnp.concatenate([decay[0::2], decay[1::2]], axis=2)
    return decay_pairs, per_lane(q_w), per_lane(k_w), gcm


def kernel(x, w_in, swa_sinks, w_out, ln1_w, ln1_b, w_up, w_down, ln2_w, ln2_b):
    b, s, d = x.shape
    cos_t, sin_t = _rope_tables(s)
    decay, qw, kw, gcm = _retention_tables()
    row = lambda t: t.reshape(1, -1)
    tm_in = 512
    tm_out = 512
    for layer in range(w_in.shape[0]):
        qkv, bq32, kmean8, g = _in_proj(x, w_in[layer].astype(BF16), cos_t, sin_t, tm=tm_in)
        nblk = tm_in // MOBA_BLOCK
        kmean = kmean8.reshape(b, s // tm_in, 8, 2 * LANES)[:, :, :nblk].reshape(b, s // MOBA_BLOCK, 2 * LANES)
        ya = _swa(qkv, swa_sinks[layer])
        yb = _moba(qkv, bq32, kmean)
        yc = _sb(qkv)
        yd = _ret(qkv, g, decay, qw, kw, gcm)
        x = _out_mlp(ya, yb, yc, yd, x, w_out[layer].astype(BF16), row(ln1_w[layer]), row(ln1_b[layer]),
                     w_up[layer].astype(BF16), w_down[layer].astype(BF16), row(ln2_w[layer]), row(ln2_b[layer]),
                     tm=tm_out)
    return x
```

```python
import functools
import math

import jax
import jax.numpy as jnp
from jax import lax
from jax.experimental import pallas as pl
from jax.experimental.pallas import tpu as pltpu

F32 = jnp.float32
BF16 = jnp.bfloat16

D_MODEL = 1024
HEAD_DIM = 64
LANES = 128
D_FF = 4 * D_MODEL
DEPTH = 2
ROPE_THETA = 10000.0
LN_EPS = 1e-5
NEG = -1e30
DEEPNORM_ALPHA = (2 * DEPTH) ** 0.25
QK_SCALE = HEAD_DIM ** -0.5

SWA_BLOCK = 128
SWA_WINDOW = 128
SWA_BLOCKS_PER_STEP = 2
MOBA_BLOCK = 256
MOBA_BLOCKS_PER_TILE = 4
MOBA_TOPK = 3
SB_BLOCK = 128
SB_WINDOW_BLKS = 4
RET_CHUNK = 128
RET_HEADS = 4
RET_CHUNKS_PER_STEP = 4

BLK_A_Q, BLK_A_K, BLK_A_V = 0, 2, 3
BLK_B_Q, BLK_B_K, BLK_B_V = 4, 6, 8
BLK_C_Q, BLK_C_K, BLK_C_V = 10, 12, 14
BLK_D_Q, BLK_D_K, BLK_D_V, BLK_D_G = 16, 18, 20, 22
N_QKV_BLKS = 22
ROPED_BLKS = frozenset([0, 1, 2, 4, 5, 6, 7, 16, 17, 18, 19])
LOG2E = math.log2(math.e)
BLK_SCALE = {blk: QK_SCALE * LOG2E for blk in (0, 1, 4, 5, 10, 11)}
BLK_SCALE.update({18: QK_SCALE, 19: QK_SCALE})

SB_EXIT_LOG2 = -110.0 * LOG2E

VMEM_LIMIT = 56 * 1024 * 1024


def _dot(a, b, **kw):
    return jnp.dot(a, b, preferred_element_type=F32, **kw)


def _dot_nt(a, b, **kw):
    return lax.dot_general(a, b, (((1,), (1,)), ((), ())), preferred_element_type=F32, **kw)


def _lane_iota(shape):
    return lax.broadcasted_iota(jnp.int32, shape, len(shape) - 1)


def _row_iota(shape):
    return lax.broadcasted_iota(jnp.int32, shape, len(shape) - 2)


def _in_proj_kernel(x_ref, w_ref, cos_ref, sin_ref, qkv_ref, bq_ref, kmean_ref, g_ref, *, tm):
    xb = x_ref[0].astype(BF16)
    cos = cos_ref[...]
    sin = sin_ref[...]
    first_half = (_lane_iota((tm, LANES)) % HEAD_DIM) < (HEAD_DIM // 2)
    n_pairs = w_ref.shape[1] // (2 * LANES)
    for c in range(n_pairs):
        acc2 = _dot(xb, w_ref[:, c * 2 * LANES:(c + 1) * 2 * LANES])
        for half in range(2):
            blk = 2 * c + half
            acc = acc2[:, half * LANES:(half + 1) * LANES]
            if blk in ROPED_BLKS:
                partner = jnp.where(first_half,
                                    pltpu.roll(acc, LANES - HEAD_DIM // 2, 1),
                                    pltpu.roll(acc, HEAD_DIM // 2, 1))
                acc = acc * cos + partner * sin
            if blk in (BLK_B_Q, BLK_B_Q + 1):
                o = (blk - BLK_B_Q) * LANES
                bq_ref[0, :, o:o + LANES] = acc
            if blk in (BLK_B_K, BLK_B_K + 1):
                o = (blk - BLK_B_K) * LANES
                nblk = tm // MOBA_BLOCK
                means = [jnp.sum(acc[i * MOBA_BLOCK:(i + 1) * MOBA_BLOCK], axis=0, keepdims=True)
                         * (1.0 / MOBA_BLOCK) for i in range(nblk)]
                means.append(jnp.zeros((8 - nblk, LANES), F32))
                kmean_ref[0, :, o:o + LANES] = jnp.concatenate(means, axis=0)
            if blk in (BLK_D_G, BLK_D_G + 1):
                o = (blk - BLK_D_G) * LANES
                g_ref[0, :, o:o + LANES] = acc
            if blk < N_QKV_BLKS:
                if blk in BLK_SCALE:
                    acc = acc * BLK_SCALE[blk]
                qkv_ref[0, :, blk * LANES:(blk + 1) * LANES] = acc.astype(BF16)


def _in_proj(x, w_in_bf16, cos_t, sin_t, *, tm):
    b, s, d = x.shape
    nt = s // tm
    n_in = w_in_bf16.shape[1]
    kern = functools.partial(_in_proj_kernel, tm=tm)
    return pl.pallas_call(
        kern,
        grid=(b, nt),
        in_specs=[
            pl.BlockSpec((1, tm, d), lambda i, t: (i, t, 0)),
            pl.BlockSpec((d, n_in), lambda i, t: (0, 0), pipeline_mode=pl.Buffered(1)),
            pl.BlockSpec((tm, LANES), lambda i, t: (t, 0)),
            pl.BlockSpec((tm, LANES), lambda i, t: (t, 0)),
        ],
        out_specs=[
            pl.BlockSpec((1, tm, N_QKV_BLKS * LANES), lambda i, t: (i, t, 0)),
            pl.BlockSpec((1, tm, 2 * LANES), lambda i, t: (i, t, 0)),
            pl.BlockSpec((1, 8, 2 * LANES), lambda i, t: (i * nt + t, 0, 0)),
            pl.BlockSpec((1, tm, 2 * LANES), lambda i, t: (i, t, 0)),
        ],
        out_shape=[
            jax.ShapeDtypeStruct((b, s, N_QKV_BLKS * LANES), BF16),
            jax.ShapeDtypeStruct((b, s, 2 * LANES), F32),
            jax.ShapeDtypeStruct((b * nt, 8, 2 * LANES), F32),
            jax.ShapeDtypeStruct((b, s, 2 * LANES), F32),
        ],
        compiler_params=pltpu.CompilerParams(
            dimension_semantics=("arbitrary", "arbitrary"), vmem_limit_bytes=VMEM_LIMIT),
        name="in_proj",
    )(x, w_in_bf16, cos_t, sin_t)


def _swa_kernel(sinks_ref, q_ref, k_ref, v_ref, o_ref, qt_ref, vt_ref, bias_ref, *, seq):
    w = SWA_BLOCK
    nb = seq // w
    hd = HEAD_DIM
    n_kv = k_ref.shape[-1] // hd
    chunk = 2 * w

    def prep(c, _):
        rows = pl.ds(pl.multiple_of(c * chunk, chunk), chunk)
        qt_ref[:, rows] = q_ref[0, rows, :].astype(F32).T.astype(BF16)
        vt = v_ref[0, rows, :].astype(F32).T.astype(BF16)
        ones = jnp.ones((16, chunk), BF16)
        for kv in range(n_kv):
            vt_ref[kv, :, rows] = jnp.concatenate([vt[kv * hd:(kv + 1) * hd], ones], axis=0)
        return 0

    lax.fori_loop(0, seq // chunk, prep, 0)

    delta = (_lane_iota((2 * w, 2 * w)) % w) - _row_iota((2 * w, 2 * w))
    for t in range(2):
        dist = delta + t * w
        bias_ref[t] = jnp.where((dist >= 0) & (dist < SWA_WINDOW), 0.0, NEG)
    first_head = _lane_iota((1, 2 * w)) < w
    sink = [jnp.where(first_head, sinks_ref[2 * kv], sinks_ref[2 * kv + 1]) * LOG2E
            for kv in range(n_kv)]
    zeros = jnp.zeros((hd, 2 * w), BF16)

    def body(it, _):
        jobs = []
        for u in range(SWA_BLOCKS_PER_STEP):
            i = it * SWA_BLOCKS_PER_STEP + u
            qs = pl.multiple_of(i * w, w)
            ks = pl.multiple_of(jnp.maximum(i - 1, 0) * w, w)
            for kv in range(n_kv):
                qq = jnp.concatenate([qt_ref[(2 * kv) * hd:(2 * kv + 1) * hd, pl.ds(qs, w)],
                                      qt_ref[(2 * kv + 1) * hd:(2 * kv + 2) * hd, pl.ds(qs, w)]], axis=1)
                rhs = jnp.concatenate([qq, zeros] if kv == 0 else [zeros, qq], axis=0)
                jobs.append((qs, ks, jnp.minimum(i, 1), kv, rhs))
        s_all = [_dot(k_ref[0, pl.ds(ks, 2 * w), :], rhs) for _, ks, _, _, rhs in jobs]
        mp = []
        for s, (_, _, t, kv, _) in zip(s_all, jobs):
            s = s + bias_ref[t]
            m = jnp.maximum(jnp.max(s, axis=0, keepdims=True), sink[kv])
            mp.append((m, jnp.exp2(s - m).astype(BF16)))
        pv_all = [_dot(vt_ref[kv, :, pl.ds(ks, 2 * w)], p) for (_, p), (_, ks, _, kv, _) in zip(mp, jobs)]
        for pv, (m, _), (qs, _, _, kv, _) in zip(pv_all, mp, jobs):
            den = pv[hd:hd + 1] + jnp.exp2(sink[kv] - m)
            o = pv[:hd] * (1.0 / den)
            o_t = jnp.concatenate([o[:, :w], o[:, w:]], axis=0)
            o_ref[0, pl.ds(qs, w), kv * LANES:(kv + 1) * LANES] = o_t.T.astype(o_ref.dtype)
        return 0

    lax.fori_loop(0, nb // SWA_BLOCKS_PER_STEP, body, 0)


def _swa(qkv, sinks):
    b, s, _ = qkv.shape
    kern = functools.partial(_swa_kernel, seq=s)
    return pl.pallas_call(
        kern,
        grid=(b,),
        in_specs=[
            pl.BlockSpec(memory_space=pltpu.SMEM),
            pl.BlockSpec((1, s, 2 * LANES), lambda i: (i, 0, BLK_A_Q // 2)),
            pl.BlockSpec((1, s, LANES), lambda i: (i, 0, BLK_A_K)),
            pl.BlockSpec((1, s, LANES), lambda i: (i, 0, BLK_A_V)),
        ],
        out_specs=pl.BlockSpec((1, s, 2 * LANES), lambda i: (i, 0, 0)),
        out_shape=jax.ShapeDtypeStruct((b, s, 2 * LANES), BF16),
        scratch_shapes=[
            pltpu.VMEM((2 * LANES, s), BF16),
            pltpu.VMEM((2, HEAD_DIM + 16, s), BF16),
            pltpu.VMEM((2, 2 * SWA_BLOCK, 2 * SWA_BLOCK), F32),
        ],
        compiler_params=pltpu.CompilerParams(
            dimension_semantics=("arbitrary",), vmem_limit_bytes=VMEM_LIMIT),
        name="swa",
    )(sinks, qkv, qkv, qkv)


def _moba_kernel(q_ref, k_ref, v_ref, q32_ref, kmean_ref, o_ref, ka_ref, vt_ref, qt_ref, *, seq):
    lb = MOBA_BLOCK
    nb = seq // lb

    n_pairs = q_ref.shape[-1] // LANES
    n_heads = 2 * n_pairs
    hd = HEAD_DIM

    def prep(c, _):
        rows = pl.ds(pl.multiple_of(c * lb, lb), lb)
        lane = _lane_iota((lb, LANES))
        ones = jnp.ones((16, lb), BF16)
        for pr in range(n_pairs):
            cols = slice(pr * LANES, (pr + 1) * LANES)
            kt = k_ref[0, rows, cols].astype(F32)
            ka_ref[2 * pr, rows, :] = jnp.where(lane < hd, kt, jnp.where(lane - hd == c, 1.0, 0.0)).astype(BF16)
            ka_ref[2 * pr + 1, rows, :] = jnp.where(lane >= hd, kt, jnp.where(lane == c, 1.0, 0.0)).astype(BF16)
            vt = v_ref[0, rows, cols].astype(F32).T.astype(BF16)
            vt_ref[2 * pr, :, rows] = jnp.concatenate([vt[:hd], ones], axis=0)
            vt_ref[2 * pr + 1, :, rows] = jnp.concatenate([vt[hd:], ones], axis=0)
            qt_ref[cols, rows] = q_ref[0, rows, cols].astype(F32).T.astype(BF16)
        return 0

    lax.fori_loop(0, nb, prep, 0)

    lane_lo_k = _lane_iota((nb, LANES)) < hd
    kmh = []
    for pr in range(n_pairs):
        km = kmean_ref[0, :, pr * LANES:(pr + 1) * LANES]
        kmh += [jnp.where(lane_lo_k, km, 0.0), jnp.where(lane_lo_k, 0.0, km)]
    blk_iota = _row_iota((nb, lb))
    causal_t = _row_iota((lb, lb)) <= _lane_iota((lb, lb))
    pad = jnp.zeros((hd - nb, lb), BF16)
    halves = range(MOBA_BLOCKS_PER_TILE)

    def tile(j, _):
        first = j * MOBA_BLOCKS_PER_TILE
        qs = [pl.multiple_of((first + u) * lb, lb) for u in halves]

        qa = {}
        for u in halves:
            valid = blk_iota < first + u
            bias_t = []
            for h in range(n_heads):
                q32 = q32_ref[0, pl.ds(qs[u], lb), (h // 2) * LANES:(h // 2 + 1) * LANES]
                gate = _dot_nt(kmh[h], q32, precision=lax.Precision.HIGHEST)
                gm = jnp.where(valid, gate, -jnp.inf)
                rank = jnp.zeros((nb, lb), F32)
                for m_idx in range(nb):
                    row = gm[m_idx:m_idx + 1, :]
                    tie = jnp.where(blk_iota > m_idx, 1.0, 0.0)
                    rank = rank + jnp.where(row > gm, 1.0, jnp.where(row == gm, tie, 0.0))
                routed = jnp.where(rank < MOBA_TOPK, 0.0, NEG)
                bias_t.append(jnp.where(valid, routed, 0.0).astype(BF16))
            for pr in range(n_pairs):
                qt = qt_ref[pr * LANES:(pr + 1) * LANES, pl.ds(qs[u], lb)]
                qa[(2 * pr, u)] = jnp.concatenate([qt[:hd], bias_t[2 * pr], pad], axis=0)
                qa[(2 * pr + 1, u)] = jnp.concatenate([bias_t[2 * pr + 1], pad, qt[hd:]], axis=0)
        chains = [(h, u) for h in range(n_heads) for u in halves]

        def local_many(jobs):
            s_all = [_dot(ka_ref[h, pl.ds(ks, lb), :], qa[(h, u)]) for h, u, ks, _ in jobs]
            mp = []
            for s, (_, _, _, masked) in zip(s_all, jobs):
                if masked:
                    s = jnp.where(causal_t, s, NEG)
                m = jnp.max(s, axis=0, keepdims=True)
                mp.append((m, jnp.exp2((s - m).astype(BF16))))
            return [(m, _dot(vt_ref[h, :, pl.ds(ks, lb)], p))
                    for (m, p), (h, _, ks, _) in zip(mp, jobs)]

        def merge(parts):
            m_new = parts[0][0]
            for m, _ in parts[1:]:
                m_new = jnp.maximum(m_new, m)
            acc = None
            for m, pv in parts:
                term = pv * jnp.exp2(m - m_new)
                acc = term if acc is None else acc + term
            return m_new, acc

        jobs = [(h, u, qs[v], v == u) for h, u in chains for v in range(u + 1)]
        parts = local_many(jobs)
        state = []
        for h, u in chains:
            state += merge([part for part, job in zip(parts, jobs) if job[:2] == (h, u)])

        def past(n, st):
            ks = pl.multiple_of(n * lb, lb)
            parts = local_many([(h, u, ks, False) for h, u in chains])
            out = []
            for c in range(len(chains)):
                out += merge([(st[2 * c], st[2 * c + 1]), parts[c]])
            return tuple(out)

        st = lax.fori_loop(0, first, past, tuple(state))
        for u in halves:
            accs = [st[2 * chains.index((h, u)) + 1] for h in range(n_heads)]
            o_t = jnp.concatenate([a[:hd] * (1.0 / a[hd:hd + 1]) for a in accs], axis=0)
            o_ref[0, pl.ds(qs[u], lb), :] = o_t.T.astype(o_ref.dtype)
        return 0

    lax.fori_loop(0, nb // MOBA_BLOCKS_PER_TILE, tile, 0)


def _moba(qkv, bq32, kmean):
    b, s, _ = qkv.shape
    nb = s // MOBA_BLOCK
    kern = functools.partial(_moba_kernel, seq=s)
    return pl.pallas_call(
        kern,
        grid=(b,),
        in_specs=[
            pl.BlockSpec((1, s, 2 * LANES), lambda i: (i, 0, BLK_B_Q // 2)),
            pl.BlockSpec((1, s, 2 * LANES), lambda i: (i, 0, BLK_B_K // 2)),
            pl.BlockSpec((1, s, 2 * LANES), lambda i: (i, 0, BLK_B_V // 2)),
            pl.BlockSpec((1, s, 2 * LANES), lambda i: (i, 0, 0)),
            pl.BlockSpec((1, nb, 2 * LANES), lambda i: (i, 0, 0)),
        ],
        out_specs=pl.BlockSpec((1, s, 2 * LANES), lambda i: (i, 0, 0)),
        out_shape=jax.ShapeDtypeStruct((b, s, 2 * LANES), BF16),
        scratch_shapes=[
            pltpu.VMEM((4, s, LANES), BF16),
            pltpu.VMEM((4, HEAD_DIM + 16, s), BF16),
            pltpu.VMEM((2 * LANES, s), BF16),
        ],
        compiler_params=pltpu.CompilerParams(
            dimension_semantics=("arbitrary",), vmem_limit_bytes=VMEM_LIMIT),
        name="moba",
    )(qkv, qkv, qkv, bq32, kmean)


def _sb_kernel(q_ref, k_ref, v_ref, o_ref, km_ref, tri_ref, carry_ref, acc_ref, *, seq):
    w = SB_BLOCK
    nb = seq // w
    win = SB_WINDOW_BLKS * w
    chunk = 512

    n_pairs = q_ref.shape[-1] // LANES
    n_heads = 2 * n_pairs

    def prep(c, _):
        rows = pl.ds(pl.multiple_of(c * chunk, chunk), chunk)
        lane_lo = _lane_iota((chunk, LANES)) < HEAD_DIM
        for pr in range(n_pairs):
            kt = k_ref[0, rows, pr * LANES:(pr + 1) * LANES].astype(F32)
            km_ref[2 * pr, rows, :] = jnp.where(lane_lo, kt, 0.0).astype(BF16)
            km_ref[2 * pr + 1, rows, :] = jnp.where(lane_lo, 0.0, kt).astype(BF16)
        return 0

    lax.fori_loop(0, seq // chunk, prep, 0)

    tri_r = _row_iota((2 * w, 2 * w)) % w
    tri_c = _lane_iota((2 * w, 2 * w))
    tri_ref[...] = jnp.where((tri_c >= w) | (tri_r > tri_c), 1.0, 0.0).astype(BF16)
    lane_lo = _lane_iota((w, LANES)) < HEAD_DIM
    row = _row_iota((w, win))
    kcol = _lane_iota((w, win))

    def tile(i, _):
        qs = pl.multiple_of(i * w, w)
        qpos = qs + row
        carry_ref[...] = jnp.zeros_like(carry_ref)
        acc_ref[...] = jnp.zeros_like(acc_ref)

        def cond(st):
            top, live = st
            return (top > 0) & live

        def body(st):
            top, _ = st
            bottom = jnp.maximum(top - SB_WINDOW_BLKS, 0)
            ws = pl.multiple_of(bottom * w, w)
            mask = (ws + kcol) < jnp.minimum(qpos, top * w)
            tri = tri_ref[...]
            z = [_dot_nt(q_ref[0, pl.ds(qs, w), (h // 2) * LANES:(h // 2 + 1) * LANES],
                         km_ref[h, pl.ds(ws, win), :]) for h in range(n_heads)]
            log_beta = []
            split = []
            for h in range(n_heads):
                soft = jnp.log2(1.0 + jnp.exp2(-jnp.abs(z[h])))
                lb_h = jnp.minimum(z[h], 0.0) - soft
                log_fail = jnp.where(mask, lb_h - z[h], 0.0)
                log_beta.append(lb_h)
                for c in range(SB_WINDOW_BLKS):
                    lf = log_fail[:, c * w:(c + 1) * w]
                    hi = lf.astype(BF16)
                    split.append(jnp.concatenate([hi, (lf - hi.astype(F32)).astype(BF16)], axis=1))
            r = [_dot(x, tri) for x in split]
            a_all = []
            for h in range(n_heads):
                run = carry_ref[h]
                a = [None] * SB_WINDOW_BLKS
                for c in reversed(range(SB_WINDOW_BLKS)):
                    cols = slice(c * w, (c + 1) * w)
                    rc = r[h * SB_WINDOW_BLKS + c]
                    a[c] = jnp.where(mask[:, cols], jnp.exp2(log_beta[h][:, cols] + rc[:, :w] + run), 0.0)
                    run = run + rc[:, w:]
                carry_ref[h] = run
                a_all.append(jnp.concatenate(a, axis=1).astype(BF16))
            live = None
            for h in range(n_heads):
                vv = v_ref[0, pl.ds(ws, win), (h // 2) * LANES:(h // 2 + 1) * LANES]
                acc_ref[h] = acc_ref[h] + _dot(a_all[h], vv)
                top_h = jnp.max(carry_ref[h])
                live = top_h if live is None else jnp.maximum(live, top_h)
            return bottom, live > SB_EXIT_LOG2

        lax.while_loop(cond, body, (i + 1, True))
        for pr in range(n_pairs):
            o_ref[0, pl.ds(qs, w), pr * LANES:(pr + 1) * LANES] = jnp.where(
                lane_lo, acc_ref[2 * pr], acc_ref[2 * pr + 1]).astype(o_ref.dtype)
        return 0

    lax.fori_loop(0, nb, tile, 0)


def _sb(qkv):
    b, s, _ = qkv.shape
    kern = functools.partial(_sb_kernel, seq=s)
    return pl.pallas_call(
        kern,
        grid=(b,),
        in_specs=[
            pl.BlockSpec((1, s, 2 * LANES), lambda i: (i, 0, BLK_C_Q // 2)),
            pl.BlockSpec((1, s, 2 * LANES), lambda i: (i, 0, BLK_C_K // 2)),
            pl.BlockSpec((1, s, 2 * LANES), lambda i: (i, 0, BLK_C_V // 2)),
        ],
        out_specs=pl.BlockSpec((1, s, 2 * LANES), lambda i: (i, 0, 0)),
        out_shape=jax.ShapeDtypeStruct((b, s, 2 * LANES), BF16),
        scratch_shapes=[
            pltpu.VMEM((4, s, LANES), BF16),
            pltpu.VMEM((2 * SB_BLOCK, 2 * SB_BLOCK), BF16),
            pltpu.VMEM((4, SB_BLOCK, SB_BLOCK), F32),
            pltpu.VMEM((4, SB_BLOCK, LANES), F32),
        ],
        compiler_params=pltpu.CompilerParams(
            dimension_semantics=("arbitrary",), vmem_limit_bytes=VMEM_LIMIT),
        name="stick_breaking",
    )(qkv, qkv, qkv)


def _ret_kernel(q_ref, k_ref, v_ref, g_ref, decay_ref, qw_ref, kw_ref, gc_ref, o_ref, state_ref, *, seq):
    c = RET_CHUNK
    n = seq // c
    lane_lo = _lane_iota((c, LANES)) < HEAD_DIM
    state_ref[...] = jnp.zeros_like(state_ref)

    def split_heads(t):
        return jnp.concatenate([jnp.where(lane_lo, t, 0.0), jnp.where(lane_lo, 0.0, t)], axis=0).astype(BF16)

    def half_stat(t):
        lo = jnp.sum(jnp.where(lane_lo, t, 0.0), axis=-1, keepdims=True)
        hi = jnp.sum(jnp.where(lane_lo, 0.0, t), axis=-1, keepdims=True)
        return jnp.where(lane_lo, lo, hi) * (1.0 / HEAD_DIM)

    def step(it, _):
        jobs = [(u, p) for u in range(RET_CHUNKS_PER_STEP) for p in range(2)]
        rows = [pl.ds(pl.multiple_of((it * RET_CHUNKS_PER_STEP + u) * c, c), c) for u in range(RET_CHUNKS_PER_STEP)]
        cols = [slice(p * LANES, (p + 1) * LANES) for p in range(2)]
        q2 = {j: q_ref[0, rows[j[0]], cols[j[1]]] for j in jobs}
        k2f = {j: k_ref[0, rows[j[0]], cols[j[1]]].astype(F32) for j in jobs}
        v2 = {j: v_ref[0, rows[j[0]], cols[j[1]]] for j in jobs}
        scores = {j: _dot_nt(q2[j], split_heads(k2f[j])) for j in jobs}
        kv = {j: _dot((k2f[j] * kw_ref[j[1]]).T.astype(BF16), v2[j]) for j in jobs}
        state_before = {}
        for p in range(2):
            gcm = gc_ref[p]
            st = state_ref[p]
            for u in range(RET_CHUNKS_PER_STEP):
                state_before[(u, p)] = st
                st = st * gcm + jnp.where(gcm > 0.0, kv[(u, p)], 0.0)
            state_ref[p] = st
        inner = {j: _dot((scores[j] * decay_ref[j[1]]).astype(BF16), split_heads(v2[j].astype(F32))) for j in jobs}
        cross = {j: _dot(q2[j], state_before[j].astype(BF16)) for j in jobs}
        for j in jobs:
            u, p = j
            y = inner[j] + cross[j] * qw_ref[p]
            mu = half_stat(y)
            yc = y - mu
            var = half_stat(yc * yc)
            yn = yc * lax.rsqrt(var + LN_EPS)
            g = g_ref[0, rows[u], cols[p]]
            gate = g * (1.0 / (1.0 + jnp.exp(-g)))
            o_ref[0, rows[u], cols[p]] = (yn * gate).astype(o_ref.dtype)
        return 0

    lax.fori_loop(0, n // RET_CHUNKS_PER_STEP, step, 0)


def _ret(qkv, g, decay, qw, kw, gcm):
    b, s, _ = qkv.shape
    kern = functools.partial(_ret_kernel, seq=s)
    full3 = lambda a: pl.BlockSpec(a.shape, lambda i: (0, 0, 0))
    return pl.pallas_call(
        kern,
        grid=(b,),
        in_specs=[
            pl.BlockSpec((1, s, 2 * LANES), lambda i: (i, 0, BLK_D_Q // 2)),
            pl.BlockSpec((1, s, 2 * LANES), lambda i: (i, 0, BLK_D_K // 2)),
            pl.BlockSpec((1, s, 2 * LANES), lambda i: (i, 0, BLK_D_V // 2)),
            pl.BlockSpec((1, s, 2 * LANES), lambda i: (i, 0, 0)),
            full3(decay), full3(qw), full3(kw), full3(gcm),
        ],
        out_specs=pl.BlockSpec((1, s, 2 * LANES), lambda i: (i, 0, 0)),
        out_shape=jax.ShapeDtypeStruct((b, s, 2 * LANES), BF16),
        scratch_shapes=[pltpu.VMEM((2, LANES, LANES), F32)],
        compiler_params=pltpu.CompilerParams(
            dimension_semantics=("arbitrary",), vmem_limit_bytes=VMEM_LIMIT),
        name="retention",
    )(qkv, qkv, qkv, g, decay, qw, kw, gcm)


def _layer_norm(t, w, b):
    mu = jnp.mean(t, axis=-1, keepdims=True)
    tc = t - mu
    var = jnp.mean(tc * tc, axis=-1, keepdims=True)
    return tc * lax.rsqrt(var + LN_EPS) * w + b


def _out_mlp_kernel(ya_ref, yb_ref, yc_ref, yd_ref, x_ref, wo_ref, l1w_ref, l1b_ref,
                    wu_ref, wd_ref, l2w_ref, l2b_ref, o_ref):
    width = ya_ref.shape[-1]
    h = None
    for idx, y_ref in enumerate((ya_ref, yb_ref, yc_ref, yd_ref)):
        part = _dot(y_ref[0], wo_ref[idx * width:(idx + 1) * width, :])
        h = part if h is None else h + part
    x1 = _layer_norm(DEEPNORM_ALPHA * x_ref[0] + h, l1w_ref[...], l1b_ref[...])
    x1b = x1.astype(BF16)
    ff = wu_ref.shape[1]
    step = 1024
    acc = None
    for c in range(ff // step):
        hid = _dot(x1b, wu_ref[:, c * step:(c + 1) * step])
        hid = jnp.square(jnp.maximum(hid, 0.0)).astype(BF16)
        part = _dot(hid, wd_ref[c * step:(c + 1) * step, :])
        acc = part if acc is None else acc + part
    o_ref[0] = _layer_norm(DEEPNORM_ALPHA * x1 + acc, l2w_ref[...], l2b_ref[...])


def _out_mlp(ya, yb, yc, yd, x, wo, l1w, l1b, wu, wd, l2w, l2b, *, tm):
    b, s, d = x.shape
    nt = s // tm
    ytile = pl.BlockSpec((1, tm, ya.shape[-1]), lambda i, t: (i, t, 0))
    const = lambda a: pl.BlockSpec(a.shape, lambda i, t: (0, 0), pipeline_mode=pl.Buffered(1))
    return pl.pallas_call(
        _out_mlp_kernel,
        grid=(b, nt),
        in_specs=[ytile, ytile, ytile, ytile,
                  pl.BlockSpec((1, tm, d), lambda i, t: (i, t, 0)),
                  const(wo), const(l1w), const(l1b), const(wu), const(wd), const(l2w), const(l2b)],
        out_specs=pl.BlockSpec((1, tm, d), lambda i, t: (i, t, 0)),
        out_shape=jax.ShapeDtypeStruct((b, s, d), F32),
        compiler_params=pltpu.CompilerParams(
            dimension_semantics=("arbitrary", "arbitrary"), vmem_limit_bytes=VMEM_LIMIT),
        name="out_mlp",
    )(ya, yb, yc, yd, x, wo, l1w, l1b, wu, wd, l2w, l2b)


def _rope_tables(s):
    half = HEAD_DIM // 2
    inv = 1.0 / (ROPE_THETA ** (jnp.arange(0, HEAD_DIM, 2, dtype=F32) / HEAD_DIM))
    ang = jnp.arange(s, dtype=F32)[:, None] * inv[None, :]
    cos = jnp.cos(ang)
    sin = jnp.sin(ang)
    cos_t = jnp.tile(cos, (1, LANES // half))
    sin_t = jnp.tile(jnp.concatenate([-sin, sin], axis=1), (1, LANES // HEAD_DIM))
    return cos_t, sin_t


def _retention_tables():
    c = RET_CHUNK
    h = RET_HEADS
    log_g = jnp.log(1.0 - 2.0 ** (-5.0 - jnp.arange(h, dtype=F32)))
    idx = jnp.arange(c, dtype=F32)
    diff = idx[:, None] - idx[None, :]
    decay = jnp.where(diff >= 0, jnp.exp(jnp.maximum(diff, 0.0)[None] * log_g[:, None, None]), 0.0)
    k_w = jnp.exp((c - 1.0 - idx)[:, None] * log_g[None, :])
    q_w = jnp.exp((idx + 1.0)[:, None] * log_g[None, :])
    g_c = jnp.exp(c * log_g)
    per_lane = lambda t: jnp.repeat(t, HEAD_DIM, axis=1).reshape(c, h // 2, LANES).transpose(1, 0, 2)
    head_of = jnp.arange(LANES) // HEAD_DIM
    same = head_of[:, None] == head_of[None, :]
    gcm = jnp.stack([jnp.where(same, g_c[2 * p + head_of][None, :], 0.0) for p in range(h // 2)])
    decay_pairs = jnp.concatenate([decay[0::2], decay[1::2]], axis=2)
    return decay_pairs, per_lane(q_w), per_lane(k_w), gcm


def kernel(x, w_in, swa_sinks, w_out, ln1_w, ln1_b, w_up, w_down, ln2_w, ln2_b):
    b, s, d = x.shape
    cos_t, sin_t = _rope_tables(s)
    decay, qw, kw, gcm = _retention_tables()
    row = lambda t: t.reshape(1, -1)
    tm_in = 512
    tm_out = 512
    for layer in range(w_in.shape[0]):
        qkv, bq32, kmean8, g = _in_proj(x, w_in[layer].astype(BF16), cos_t, sin_t, tm=tm_in)
        nblk = tm_in // MOBA_BLOCK
        kmean = kmean8.reshape(b, s // tm_in, 8, 2 * LANES)[:, :, :nblk].reshape(b, s // MOBA_BLOCK, 2 * LANES)
        ya = _swa(qkv, swa_sinks[layer])
        yb = _moba(qkv, bq32, kmean)
        yc = _sb(qkv)
        yd = _ret(qkv, g, decay, qw, kw, gcm)
        x = _out_mlp(ya, yb, yc, yd, x, w_out[layer].astype(BF16), row(ln1_w[layer]), row(ln1_b[layer]),
                     w_up[layer].astype(BF16), w_down[layer].astype(BF16), row(ln2_w[layer]), row(ln2_b[layer]),
                     tm=tm_out)
    return x
```

```python
import functools
import math

import jax
import jax.numpy as jnp
from jax import lax
from jax.experimental import pallas as pl
from jax.experimental.pallas import tpu as pltpu

F32 = jnp.float32
BF16 = jnp.bfloat16

D_MODEL = 1024
HEAD_DIM = 64
LANES = 128
D_FF = 4 * D_MODEL
DEPTH = 2
ROPE_THETA = 10000.0
LN_EPS = 1e-5
NEG = -1e30
DEEPNORM_ALPHA = (2 * DEPTH) ** 0.25
QK_SCALE = HEAD_DIM ** -0.5

SWA_BLOCK = 128
SWA_WINDOW = 128
SWA_BLOCKS_PER_STEP = 2
MOBA_BLOCK = 256
MOBA_BLOCKS_PER_TILE = 4
MOBA_TOPK = 3
SB_BLOCK = 128
SB_WINDOW_BLKS = 3
SB_ROWS_PER_STEP = 2
SB_POW_CLAMP = 126.0
RET_CHUNK = 128
RET_HEADS = 4
OUT_SUBTILE = 256
RET_CHUNKS_PER_STEP = 4

BLK_A_Q, BLK_A_K, BLK_A_V = 0, 2, 3
BLK_B_Q, BLK_B_K, BLK_B_V = 4, 6, 8
BLK_C_Q, BLK_C_K, BLK_C_V = 10, 12, 14
BLK_D_Q, BLK_D_K, BLK_D_V, BLK_D_G = 16, 18, 20, 22
N_QKV_BLKS = 22
ROPED_BLKS = frozenset([0, 1, 2, 4, 5, 6, 7, 16, 17, 18, 19])
LOG2E = math.log2(math.e)
BLK_SCALE = {blk: QK_SCALE * LOG2E for blk in (0, 1, 4, 5, 10, 11)}
BLK_SCALE.update({18: QK_SCALE, 19: QK_SCALE})

SB_EXIT_LOG2 = -110.0 * LOG2E

VMEM_LIMIT = 56 * 1024 * 1024


def _dot(a, b, **kw):
    return jnp.dot(a, b, preferred_element_type=F32, **kw)


def _dot_nt(a, b, **kw):
    return lax.dot_general(a, b, (((1,), (1,)), ((), ())), preferred_element_type=F32, **kw)


def _lane_iota(shape):
    return lax.broadcasted_iota(jnp.int32, shape, len(shape) - 1)


def _row_iota(shape):
    return lax.broadcasted_iota(jnp.int32, shape, len(shape) - 2)


def _in_proj_kernel(x_ref, w_ref, cos_ref, sin_ref, qkv_ref, bq_ref, kmean_ref, g_ref, *, tm):
    xb = x_ref[0].astype(BF16)
    cos = cos_ref[...]
    sin = sin_ref[...]
    first_half = (_lane_iota((tm, LANES)) % HEAD_DIM) < (HEAD_DIM // 2)
    n_pairs = w_ref.shape[1] // (2 * LANES)
    for c in range(n_pairs):
        acc2 = _dot(xb, w_ref[:, c * 2 * LANES:(c + 1) * 2 * LANES])
        for half in range(2):
            blk = 2 * c + half
            acc = acc2[:, half * LANES:(half + 1) * LANES]
            if blk in ROPED_BLKS:
                partner = jnp.where(first_half,
                                    pltpu.roll(acc, LANES - HEAD_DIM // 2, 1),
                                    pltpu.roll(acc, HEAD_DIM // 2, 1))
                acc = acc * cos + partner * sin
            if blk in (BLK_B_Q, BLK_B_Q + 1):
                o = (blk - BLK_B_Q) * LANES
                bq_ref[0, :, o:o + LANES] = acc
            if blk in (BLK_B_K, BLK_B_K + 1):
                o = (blk - BLK_B_K) * LANES
                nblk = tm // MOBA_BLOCK
                means = [jnp.sum(acc[i * MOBA_BLOCK:(i + 1) * MOBA_BLOCK], axis=0, keepdims=True)
                         * (1.0 / MOBA_BLOCK) for i in range(nblk)]
                means.append(jnp.zeros((8 - nblk, LANES), F32))
                kmean_ref[0, :, o:o + LANES] = jnp.concatenate(means, axis=0)
            if blk in (BLK_D_G, BLK_D_G + 1):
                o = (blk - BLK_D_G) * LANES
                g_ref[0, :, o:o + LANES] = acc
            if blk < N_QKV_BLKS:
                if blk in BLK_SCALE:
                    acc = acc * BLK_SCALE[blk]
                qkv_ref[0, :, blk * LANES:(blk + 1) * LANES] = acc.astype(BF16)


def _in_proj(x, w_in_bf16, cos_t, sin_t, *, tm):
    b, s, d = x.shape
    nt = s // tm
    n_in = w_in_bf16.shape[1]
    kern = functools.partial(_in_proj_kernel, tm=tm)
    return pl.pallas_call(
        kern,
        grid=(b, nt),
        in_specs=[
            pl.BlockSpec((1, tm, d), lambda i, t: (i, t, 0)),
            pl.BlockSpec((d, n_in), lambda i, t: (0, 0), pipeline_mode=pl.Buffered(1)),
            pl.BlockSpec((tm, LANES), lambda i, t: (t, 0)),
            pl.BlockSpec((tm, LANES), lambda i, t: (t, 0)),
        ],
        out_specs=[
            pl.BlockSpec((1, tm, N_QKV_BLKS * LANES), lambda i, t: (i, t, 0)),
            pl.BlockSpec((1, tm, 2 * LANES), lambda i, t: (i, t, 0)),
            pl.BlockSpec((1, 8, 2 * LANES), lambda i, t: (i * nt + t, 0, 0)),
            pl.BlockSpec((1, tm, 2 * LANES), lambda i, t: (i, t, 0)),
        ],
        out_shape=[
            jax.ShapeDtypeStruct((b, s, N_QKV_BLKS * LANES), BF16),
            jax.ShapeDtypeStruct((b, s, 2 * LANES), F32),
            jax.ShapeDtypeStruct((b * nt, 8, 2 * LANES), F32),
            jax.ShapeDtypeStruct((b, s, 2 * LANES), F32),
        ],
        compiler_params=pltpu.CompilerParams(
            dimension_semantics=("arbitrary", "arbitrary"), vmem_limit_bytes=VMEM_LIMIT),
        name="in_proj",
    )(x, w_in_bf16, cos_t, sin_t)


def _swa_kernel(sinks_ref, q_ref, k_ref, v_ref, o_ref, qt_ref, vt_ref, bias_ref, *, seq):
    w = SWA_BLOCK
    nb = seq // w
    hd = HEAD_DIM
    n_kv = k_ref.shape[-1] // hd
    chunk = 2 * w

    def prep(c, _):
        rows = pl.ds(pl.multiple_of(c * chunk, chunk), chunk)
        qt_ref[:, rows] = q_ref[0, rows, :].astype(F32).T.astype(BF16)
        vt = v_ref[0, rows, :].astype(F32).T.astype(BF16)
        ones = jnp.ones((16, chunk), BF16)
        for kv in range(n_kv):
            vt_ref[kv, :, rows] = jnp.concatenate([vt[kv * hd:(kv + 1) * hd], ones], axis=0)
        return 0

    lax.fori_loop(0, seq // chunk, prep, 0)

    delta = (_lane_iota((2 * w, 2 * w)) % w) - _row_iota((2 * w, 2 * w))
    for t in range(2):
        dist = delta + t * w
        bias_ref[t] = jnp.where((dist >= 0) & (dist < SWA_WINDOW), 0.0, NEG)
    first_head = _lane_iota((1, 2 * w)) < w
    sink = [jnp.where(first_head, sinks_ref[2 * kv], sinks_ref[2 * kv + 1]) * LOG2E
            for kv in range(n_kv)]
    zeros = jnp.zeros((hd, 2 * w), BF16)

    def body(it, _):
        jobs = []
        for u in range(SWA_BLOCKS_PER_STEP):
            i = it * SWA_BLOCKS_PER_STEP + u
            qs = pl.multiple_of(i * w, w)
            ks = pl.multiple_of(jnp.maximum(i - 1, 0) * w, w)
            for kv in range(n_kv):
                qq = jnp.concatenate([qt_ref[(2 * kv) * hd:(2 * kv + 1) * hd, pl.ds(qs, w)],
                                      qt_ref[(2 * kv + 1) * hd:(2 * kv + 2) * hd, pl.ds(qs, w)]], axis=1)
                rhs = jnp.concatenate([qq, zeros] if kv == 0 else [zeros, qq], axis=0)
                jobs.append((qs, ks, jnp.minimum(i, 1), kv, rhs))
        s_all = [_dot(k_ref[0, pl.ds(ks, 2 * w), :], rhs) for _, ks, _, _, rhs in jobs]
        mp = []
        for s, (_, _, t, kv, _) in zip(s_all, jobs):
            s = s + bias_ref[t]
            m = jnp.maximum(jnp.max(s, axis=0, keepdims=True), sink[kv])
            mp.append((m, jnp.exp2(s - m).astype(BF16)))
        pv_all = [_dot(vt_ref[kv, :, pl.ds(ks, 2 * w)], p) for (_, p), (_, ks, _, kv, _) in zip(mp, jobs)]
        for pv, (m, _), (qs, _, _, kv, _) in zip(pv_all, mp, jobs):
            den = pv[hd:hd + 1] + jnp.exp2(sink[kv] - m)
            o = pv[:hd] * (1.0 / den)
            o_t = jnp.concatenate([o[:, :w], o[:, w:]], axis=0)
            o_ref[0, pl.ds(qs, w), kv * LANES:(kv + 1) * LANES] = o_t.T.astype(o_ref.dtype)
        return 0

    lax.fori_loop(0, nb // SWA_BLOCKS_PER_STEP, body, 0)


def _swa(qkv, sinks):
    b, s, _ = qkv.shape
    kern = functools.partial(_swa_kernel, seq=s)
    return pl.pallas_call(
        kern,
        grid=(b,),
        in_specs=[
            pl.BlockSpec(memory_space=pltpu.SMEM),
            pl.BlockSpec((1, s, 2 * LANES), lambda i: (i, 0, BLK_A_Q // 2)),
            pl.BlockSpec((1, s, LANES), lambda i: (i, 0, BLK_A_K)),
            pl.BlockSpec((1, s, LANES), lambda i: (i, 0, BLK_A_V)),
        ],
        out_specs=pl.BlockSpec((1, s, 2 * LANES), lambda i: (i, 0, 0)),
        out_shape=jax.ShapeDtypeStruct((b, s, 2 * LANES), BF16),
        scratch_shapes=[
            pltpu.VMEM((2 * LANES, s), BF16),
            pltpu.VMEM((2, HEAD_DIM + 16, s), BF16),
            pltpu.VMEM((2, 2 * SWA_BLOCK, 2 * SWA_BLOCK), F32),
        ],
        compiler_params=pltpu.CompilerParams(
            dimension_semantics=("arbitrary",), vmem_limit_bytes=VMEM_LIMIT),
        name="swa",
    )(sinks, qkv, qkv, qkv)


def _moba_kernel(q_ref, k_ref, v_ref, q32_ref, kmean_ref, o_ref, ka_ref, vt_ref, qt_ref, *, seq):
    lb = MOBA_BLOCK
    nb = seq // lb

    n_pairs = q_ref.shape[-1] // LANES
    n_heads = 2 * n_pairs
    hd = HEAD_DIM

    def prep(c, _):
        rows = pl.ds(pl.multiple_of(c * lb, lb), lb)
        lane = _lane_iota((lb, LANES))
        ones = jnp.ones((16, lb), BF16)
        for pr in range(n_pairs):
            cols = slice(pr * LANES, (pr + 1) * LANES)
            kt = k_ref[0, rows, cols].astype(F32)
            ka_ref[2 * pr, rows, :] = jnp.where(lane < hd, kt, jnp.where(lane - hd == c, 1.0, 0.0)).astype(BF16)
            ka_ref[2 * pr + 1, rows, :] = jnp.where(lane >= hd, kt, jnp.where(lane == c, 1.0, 0.0)).astype(BF16)
            vt = v_ref[0, rows, cols].astype(F32).T.astype(BF16)
            vt_ref[2 * pr, :, rows] = jnp.concatenate([vt[:hd], ones], axis=0)
            vt_ref[2 * pr + 1, :, rows] = jnp.concatenate([vt[hd:], ones], axis=0)
            qt_ref[cols, rows] = q_ref[0, rows, cols].astype(F32).T.astype(BF16)
        return 0

    lax.fori_loop(0, nb, prep, 0)

    lane_lo_k = _lane_iota((nb, LANES)) < hd
    kmh = []
    for pr in range(n_pairs):
        km = kmean_ref[0, :, pr * LANES:(pr + 1) * LANES]
        kmh += [jnp.where(lane_lo_k, km, 0.0), jnp.where(lane_lo_k, 0.0, km)]
    blk_iota = _row_iota((nb, lb))
    causal_t = _row_iota((lb, lb)) <= _lane_iota((lb, lb))
    pad = jnp.zeros((hd - nb, lb), BF16)
    halves = range(MOBA_BLOCKS_PER_TILE)

    def tile(j, _):
        first = j * MOBA_BLOCKS_PER_TILE
        qs = [pl.multiple_of((first + u) * lb, lb) for u in halves]

        qa = {}
        for u in halves:
            valid = blk_iota < first + u
            bias_t = []
            for h in range(n_heads):
                q32 = q32_ref[0, pl.ds(qs[u], lb), (h // 2) * LANES:(h // 2 + 1) * LANES]
                gate = _dot_nt(kmh[h], q32, precision=lax.Precision.HIGHEST)
                gm = jnp.where(valid, gate, -jnp.inf)
                rank = jnp.zeros((nb, lb), F32)
                for m_idx in range(nb):
                    row = gm[m_idx:m_idx + 1, :]
                    tie = jnp.where(blk_iota > m_idx, 1.0, 0.0)
                    rank = rank + jnp.where(row > gm, 1.0, jnp.where(row == gm, tie, 0.0))
                routed = jnp.where(rank < MOBA_TOPK, 0.0, NEG)
                bias_t.append(jnp.where(valid, routed, 0.0).astype(BF16))
            for pr in range(n_pairs):
                qt = qt_ref[pr * LANES:(pr + 1) * LANES, pl.ds(qs[u], lb)]
                qa[(2 * pr, u)] = jnp.concatenate([qt[:hd], bias_t[2 * pr], pad], axis=0)
                qa[(2 * pr + 1, u)] = jnp.concatenate([bias_t[2 * pr + 1], pad, qt[hd:]], axis=0)
        chains = [(h, u) for h in range(n_heads) for u in halves]

        def local_many(jobs):
            s_all = [_dot(ka_ref[h, pl.ds(ks, lb), :], qa[(h, u)]) for h, u, ks, _ in jobs]
            mp = []
            for s, (_, _, _, masked) in zip(s_all, jobs):
                if masked:
                    s = jnp.where(causal_t, s, NEG)
                m = jnp.max(s, axis=0, keepdims=True)
                mp.append((m, jnp.exp2((s - m).astype(BF16))))
            return [(m, _dot(vt_ref[h, :, pl.ds(ks, lb)], p))
                    for (m, p), (h, _, ks, _) in zip(mp, jobs)]

        def merge(parts):
            m_new = parts[0][0]
            for m, _ in parts[1:]:
                m_new = jnp.maximum(m_new, m)
            acc = None
            for m, pv in parts:
                term = pv * jnp.exp2(m - m_new)
                acc = term if acc is None else acc + term
            return m_new, acc

        jobs = [(h, u, qs[v], v == u) for h, u in chains for v in range(u + 1)]
        parts = local_many(jobs)
        state = []
        for h, u in chains:
            state += merge([part for part, job in zip(parts, jobs) if job[:2] == (h, u)])

        def past(n, st):
            ks = pl.multiple_of(n * lb, lb)
            parts = local_many([(h, u, ks, False) for h, u in chains])
            out = []
            for c in range(len(chains)):
                out += merge([(st[2 * c], st[2 * c + 1]), parts[c]])
            return tuple(out)

        st = lax.fori_loop(0, first, past, tuple(state))
        for u in halves:
            accs = [st[2 * chains.index((h, u)) + 1] for h in range(n_heads)]
            o_t = jnp.concatenate([a[:hd] * (1.0 / a[hd:hd + 1]) for a in accs], axis=0)
            o_ref[0, pl.ds(qs[u], lb), :] = o_t.T.astype(o_ref.dtype)
        return 0

    lax.fori_loop(0, nb // MOBA_BLOCKS_PER_TILE, tile, 0)


def _moba(qkv, bq32, kmean):
    b, s, _ = qkv.shape
    nb = s // MOBA_BLOCK
    kern = functools.partial(_moba_kernel, seq=s)
    return pl.pallas_call(
        kern,
        grid=(b,),
        in_specs=[
            pl.BlockSpec((1, s, 2 * LANES), lambda i: (i, 0, BLK_B_Q // 2)),
            pl.BlockSpec((1, s, 2 * LANES), lambda i: (i, 0, BLK_B_K // 2)),
            pl.BlockSpec((1, s, 2 * LANES), lambda i: (i, 0, BLK_B_V // 2)),
            pl.BlockSpec((1, s, 2 * LANES), lambda i: (i, 0, 0)),
            pl.BlockSpec((1, nb, 2 * LANES), lambda i: (i, 0, 0)),
        ],
        out_specs=pl.BlockSpec((1, s, 2 * LANES), lambda i: (i, 0, 0)),
        out_shape=jax.ShapeDtypeStruct((b, s, 2 * LANES), BF16),
        scratch_shapes=[
            pltpu.VMEM((4, s, LANES), BF16),
            pltpu.VMEM((4, HEAD_DIM + 16, s), BF16),
            pltpu.VMEM((2 * LANES, s), BF16),
        ],
        compiler_params=pltpu.CompilerParams(
            dimension_semantics=("arbitrary",), vmem_limit_bytes=VMEM_LIMIT),
        name="moba",
    )(qkv, qkv, qkv, bq32, kmean)


def _sb_kernel(q_ref, k_ref, v_ref, o_ref, km_ref, tri_ref, carry_ref, acc_ref, *, seq):
    w = SB_BLOCK
    nb = seq // w
    win = SB_WINDOW_BLKS * w
    chunk = 512

    n_rows = q_ref.shape[0]
    n_pairs = q_ref.shape[-1] // LANES
    n_heads = 2 * n_pairs
    chains = [(b, h) for b in range(n_rows) for h in range(n_heads)]

    def prep(c, _):
        rows = pl.ds(pl.multiple_of(c * chunk, chunk), chunk)
        lane_lo = _lane_iota((chunk, LANES)) < HEAD_DIM
        for b in range(n_rows):
            for pr in range(n_pairs):
                kt = k_ref[b, rows, pr * LANES:(pr + 1) * LANES].astype(F32)
                km_ref[b * n_heads + 2 * pr, rows, :] = jnp.where(lane_lo, kt, 0.0).astype(BF16)
                km_ref[b * n_heads + 2 * pr + 1, rows, :] = jnp.where(lane_lo, 0.0, kt).astype(BF16)
        return 0

    lax.fori_loop(0, seq // chunk, prep, 0)

    tri_r = _row_iota((2 * w, 2 * w)) % w
    tri_c = _lane_iota((2 * w, 2 * w))
    tri_ref[...] = jnp.where((tri_c >= w) | (tri_r >= tri_c), 1.0, 0.0).astype(BF16)
    lane_lo = _lane_iota((w, LANES)) < HEAD_DIM
    row = _row_iota((w, win))
    kcol = _lane_iota((w, win))

    def tile(i, _):
        qs = pl.multiple_of(i * w, w)
        qpos = qs + row
        carry_ref[...] = jnp.zeros_like(carry_ref)
        acc_ref[...] = jnp.zeros_like(acc_ref)

        def cond(st):
            top, live = st
            return (top > 0) & live

        def body(st):
            top, _ = st
            bottom = jnp.maximum(top - SB_WINDOW_BLKS, 0)
            ws = pl.multiple_of(bottom * w, w)
            mask = (ws + kcol) < jnp.minimum(qpos, top * w)
            tri = tri_ref[...]
            z = [_dot_nt(q_ref[b, pl.ds(qs, w), (h // 2) * LANES:(h // 2 + 1) * LANES],
                         km_ref[b * n_heads + h, pl.ds(ws, win), :]) for b, h in chains]
            zm = [jnp.where(mask, z_c, NEG) for z_c in z]
            split = []
            for ch in range(len(chains)):
                neg_log_fail = jnp.maximum(zm[ch], jnp.log2(1.0 + jnp.exp2(jnp.minimum(zm[ch], SB_POW_CLAMP))))
                for c in range(SB_WINDOW_BLKS):
                    nlf = neg_log_fail[:, c * w:(c + 1) * w]
                    hi = pltpu.bitcast(pltpu.bitcast(nlf, jnp.uint32) & jnp.uint32(0xFFFF0000), F32)
                    split.append(jnp.concatenate([hi.astype(BF16), (nlf - hi).astype(BF16)], axis=1))
            r = [_dot(x, tri) for x in split]
            a_all = []
            for ch in range(len(chains)):
                run = carry_ref[ch]
                a = [None] * SB_WINDOW_BLKS
                for c in reversed(range(SB_WINDOW_BLKS)):
                    rc = r[ch * SB_WINDOW_BLKS + c]
                    a[c] = jnp.exp2(zm[ch][:, c * w:(c + 1) * w] - rc[:, :w] - run)
                    run = run + rc[:, w:]
                carry_ref[ch] = run
                a_all.append(jnp.concatenate(a, axis=1).astype(BF16))
            live = None
            for ch, (b, h) in enumerate(chains):
                vv = v_ref[b, pl.ds(ws, win), (h // 2) * LANES:(h // 2 + 1) * LANES]
                acc_ref[ch] = acc_ref[ch] + _dot(a_all[ch], vv)
                low = jnp.min(carry_ref[ch])
                live = low if live is None else jnp.minimum(live, low)
            return bottom, live < -SB_EXIT_LOG2

        lax.while_loop(cond, body, (i + 1, True))
        for b in range(n_rows):
            for pr in range(n_pairs):
                o_ref[b, pl.ds(qs, w), pr * LANES:(pr + 1) * LANES] = jnp.where(
                    lane_lo, acc_ref[b * n_heads + 2 * pr], acc_ref[b * n_heads + 2 * pr + 1]).astype(o_ref.dtype)
        return 0

    lax.fori_loop(0, nb, tile, 0)


def _sb(qkv):
    b, s, _ = qkv.shape
    kern = functools.partial(_sb_kernel, seq=s)
    return pl.pallas_call(
        kern,
        grid=(b // SB_ROWS_PER_STEP,),
        in_specs=[
            pl.BlockSpec((SB_ROWS_PER_STEP, s, 2 * LANES), lambda i: (i, 0, BLK_C_Q // 2)),
            pl.BlockSpec((SB_ROWS_PER_STEP, s, 2 * LANES), lambda i: (i, 0, BLK_C_K // 2)),
            pl.BlockSpec((SB_ROWS_PER_STEP, s, 2 * LANES), lambda i: (i, 0, BLK_C_V // 2)),
        ],
        out_specs=pl.BlockSpec((SB_ROWS_PER_STEP, s, 2 * LANES), lambda i: (i, 0, 0)),
        out_shape=jax.ShapeDtypeStruct((b, s, 2 * LANES), BF16),
        scratch_shapes=[
            pltpu.VMEM((4 * SB_ROWS_PER_STEP, s, LANES), BF16),
            pltpu.VMEM((2 * SB_BLOCK, 2 * SB_BLOCK), BF16),
            pltpu.VMEM((4 * SB_ROWS_PER_STEP, SB_BLOCK, SB_BLOCK), F32),
            pltpu.VMEM((4 * SB_ROWS_PER_STEP, SB_BLOCK, LANES), F32),
        ],
        compiler_params=pltpu.CompilerParams(
            dimension_semantics=("arbitrary",), vmem_limit_bytes=VMEM_LIMIT),
        name="stick_breaking",
    )(qkv, qkv, qkv)


def _ret_kernel(q_ref, k_ref, v_ref, g_ref, decay_ref, qw_ref, kw_ref, gc_ref, o_ref, state_ref, *, seq):
    c = RET_CHUNK
    n = seq // c
    lane_lo = _lane_iota((c, LANES)) < HEAD_DIM
    state_ref[...] = jnp.zeros_like(state_ref)

    def split_heads(t):
        return jnp.concatenate([jnp.where(lane_lo, t, 0.0), jnp.where(lane_lo, 0.0, t)], axis=0).astype(BF16)

    def half_stat(t):
        lo = jnp.sum(jnp.where(lane_lo, t, 0.0), axis=-1, keepdims=True)
        hi = jnp.sum(jnp.where(lane_lo, 0.0, t), axis=-1, keepdims=True)
        return jnp.where(lane_lo, lo, hi) * (1.0 / HEAD_DIM)

    def step(it, _):
        jobs = [(u, p) for u in range(RET_CHUNKS_PER_STEP) for p in range(2)]
        rows = [pl.ds(pl.multiple_of((it * RET_CHUNKS_PER_STEP + u) * c, c), c) for u in range(RET_CHUNKS_PER_STEP)]
        cols = [slice(p * LANES, (p + 1) * LANES) for p in range(2)]
        q2 = {j: q_ref[0, rows[j[0]], cols[j[1]]] for j in jobs}
        k2f = {j: k_ref[0, rows[j[0]], cols[j[1]]].astype(F32) for j in jobs}
        v2 = {j: v_ref[0, rows[j[0]], cols[j[1]]] for j in jobs}
        scores = {j: _dot_nt(q2[j], split_heads(k2f[j])) for j in jobs}
        kv = {j: _dot((k2f[j] * kw_ref[j[1]]).T.astype(BF16), v2[j]) for j in jobs}
        state_before = {}
        for p in range(2):
            gcm = gc_ref[p]
            st = state_ref[p]
            for u in range(RET_CHUNKS_PER_STEP):
                state_before[(u, p)] = st
                st = st * gcm + jnp.where(gcm > 0.0, kv[(u, p)], 0.0)
            state_ref[p] = st
        inner = {j: _dot((scores[j] * decay_ref[j[1]]).astype(BF16), split_heads(v2[j].astype(F32))) for j in jobs}
        cross = {j: _dot(q2[j], state_before[j].astype(BF16)) for j in jobs}
        for j in jobs:
            u, p = j
            y = inner[j] + cross[j] * qw_ref[p]
            mu = half_stat(y)
            yc = y - mu
            var = half_stat(yc * yc)
            yn = yc * lax.rsqrt(var + LN_EPS)
            g = g_ref[0, rows[u], cols[p]]
            gate = g * (1.0 / (1.0 + jnp.exp(-g)))
            o_ref[0, rows[u], cols[p]] = (yn * gate).astype(o_ref.dtype)
        return 0

    lax.fori_loop(0, n // RET_CHUNKS_PER_STEP, step, 0)


def _ret(qkv, g, decay, qw, kw, gcm):
    b, s, _ = qkv.shape
    kern = functools.partial(_ret_kernel, seq=s)
    full3 = lambda a: pl.BlockSpec(a.shape, lambda i: (0, 0, 0))
    return pl.pallas_call(
        kern,
        grid=(b,),
        in_specs=[
            pl.BlockSpec((1, s, 2 * LANES), lambda i: (i, 0, BLK_D_Q // 2)),
            pl.BlockSpec((1, s, 2 * LANES), lambda i: (i, 0, BLK_D_K // 2)),
            pl.BlockSpec((1, s, 2 * LANES), lambda i: (i, 0, BLK_D_V // 2)),
            pl.BlockSpec((1, s, 2 * LANES), lambda i: (i, 0, 0)),
            full3(decay), full3(qw), full3(kw), full3(gcm),
        ],
        out_specs=pl.BlockSpec((1, s, 2 * LANES), lambda i: (i, 0, 0)),
        out_shape=jax.ShapeDtypeStruct((b, s, 2 * LANES), BF16),
        scratch_shapes=[pltpu.VMEM((2, LANES, LANES), F32)],
        compiler_params=pltpu.CompilerParams(
            dimension_semantics=("arbitrary",), vmem_limit_bytes=VMEM_LIMIT),
        name="retention",
    )(qkv, qkv, qkv, g, decay, qw, kw, gcm)


def _layer_norm(t, w, b):
    mu = jnp.mean(t, axis=-1, keepdims=True)
    tc = t - mu
    var = jnp.mean(tc * tc, axis=-1, keepdims=True)
    return tc * lax.rsqrt(var + LN_EPS) * w + b


def _out_mlp_kernel(ya_ref, yb_ref, yc_ref, yd_ref, x_ref, wo_ref, l1w_ref, l1b_ref,
                    wu_ref, wd_ref, l2w_ref, l2b_ref, o_ref):
    width = ya_ref.shape[-1]
    ff = wu_ref.shape[1]
    step = 1024
    n_sub = x_ref.shape[1] // OUT_SUBTILE
    rows = [slice(i * OUT_SUBTILE, (i + 1) * OUT_SUBTILE) for i in range(n_sub)]

    def out_proj(r):
        h = None
        for idx, y_ref in enumerate((ya_ref, yb_ref, yc_ref, yd_ref)):
            part = _dot(y_ref[0, r, :], wo_ref[idx * width:(idx + 1) * width, :])
            h = part if h is None else h + part
        return h

    def mlp(x1b):
        acc = None
        for c in range(ff // step):
            hid = _dot(x1b, wu_ref[:, c * step:(c + 1) * step])
            hid = jnp.square(jnp.maximum(hid, 0.0)).astype(BF16)
            part = _dot(hid, wd_ref[c * step:(c + 1) * step, :])
            acc = part if acc is None else acc + part
        return acc

    h = [out_proj(r) for r in rows]
    x1 = [None] * n_sub
    acc = [None] * n_sub
    for i in range(n_sub):
        x1[i] = _layer_norm(DEEPNORM_ALPHA * x_ref[0, rows[i], :] + h[i], l1w_ref[...], l1b_ref[...])
        acc[i] = mlp(x1[i].astype(BF16))
        if i > 0:
            o_ref[0, rows[i - 1], :] = _layer_norm(DEEPNORM_ALPHA * x1[i - 1] + acc[i - 1], l2w_ref[...], l2b_ref[...])
    o_ref[0, rows[-1], :] = _layer_norm(DEEPNORM_ALPHA * x1[-1] + acc[-1], l2w_ref[...], l2b_ref[...])


def _out_mlp(ya, yb, yc, yd, x, wo, l1w, l1b, wu, wd, l2w, l2b, *, tm):
    b, s, d = x.shape
    nt = s // tm
    ytile = pl.BlockSpec((1, tm, ya.shape[-1]), lambda i, t: (i, t, 0))
    const = lambda a: pl.BlockSpec(a.shape, lambda i, t: (0, 0), pipeline_mode=pl.Buffered(1))
    return pl.pallas_call(
        _out_mlp_kernel,
        grid=(b, nt),
        in_specs=[ytile, ytile, ytile, ytile,
                  pl.BlockSpec((1, tm, d), lambda i, t: (i, t, 0)),
                  const(wo), const(l1w), const(l1b), const(wu), const(wd), const(l2w), const(l2b)],
        out_specs=pl.BlockSpec((1, tm, d), lambda i, t: (i, t, 0)),
        out_shape=jax.ShapeDtypeStruct((b, s, d), F32),
        compiler_params=pltpu.CompilerParams(
            dimension_semantics=("arbitrary", "arbitrary"), vmem_limit_bytes=VMEM_LIMIT),
        name="out_mlp",
    )(ya, yb, yc, yd, x, wo, l1w, l1b, wu, wd, l2w, l2b)


def _rope_tables(s):
    half = HEAD_DIM // 2
    inv = 1.0 / (ROPE_THETA ** (jnp.arange(0, HEAD_DIM, 2, dtype=F32) / HEAD_DIM))
    ang = jnp.arange(s, dtype=F32)[:, None] * inv[None, :]
    cos = jnp.cos(ang)
    sin = jnp.sin(ang)
    cos_t = jnp.tile(cos, (1, LANES // half))
    sin_t = jnp.tile(jnp.concatenate([-sin, sin], axis=1), (1, LANES // HEAD_DIM))
    return cos_t, sin_t


def _retention_tables():
    c = RET_CHUNK
    h = RET_HEADS
    log_g = jnp.log(1.0 - 2.0 ** (-5.0 - jnp.arange(h, dtype=F32)))
    idx = jnp.arange(c, dtype=F32)
    diff = idx[:, None] - idx[None, :]
    decay = jnp.where(diff >= 0, jnp.exp(jnp.maximum(diff, 0.0)[None] * log_g[:, None, None]), 0.0)
    k_w = jnp.exp((c - 1.0 - idx)[:, None] * log_g[None, :])
    q_w = jnp.exp((idx + 1.0)[:, None] * log_g[None, :])
    g_c = jnp.exp(c * log_g)
    per_lane = lambda t: jnp.repeat(t, HEAD_DIM, axis=1).reshape(c, h // 2, LANES).transpose(1, 0, 2)
    head_of = jnp.arange(LANES) // HEAD_DIM
    same = head_of[:, None] == head_of[None, :]
    gcm = jnp.stack([jnp.where(same, g_c[2 * p + head_of][None, :], 0.0) for p in range(h // 2)])
    decay_pairs = jnp.concatenate([decay[0::2], decay[1::2]], axis=2)
    return decay_pairs, per_lane(q_w), per_lane(k_w), gcm


def kernel(x, w_in, swa_sinks, w_out, ln1_w, ln1_b, w_up, w_down, ln2_w, ln2_b):
    b, s, d = x.shape
    cos_t, sin_t = _rope_tables(s)
    decay, qw, kw, gcm = _retention_tables()
    row = lambda t: t.reshape(1, -1)
    tm_in = 512
    tm_out = 2 * OUT_SUBTILE
    for layer in range(w_in.shape[0]):
        qkv, bq32, kmean8, g = _in_proj(x, w_in[layer].astype(BF16), cos_t, sin_t, tm=tm_in)
        nblk = tm_in // MOBA_BLOCK
        kmean = kmean8.reshape(b, s // tm_in, 8, 2 * LANES)[:, :, :nblk].reshape(b, s // MOBA_BLOCK, 2 * LANES)
        ya = _swa(qkv, swa_sinks[layer])
        yb = _moba(qkv, bq32, kmean)
        yc = _sb(qkv)
        yd = _ret(qkv, g, decay, qw, kw, gcm)
        x = _out_mlp(ya, yb, yc, yd, x, w_out[layer].astype(BF16), row(ln1_w[layer]), row(ln1_b[layer]),
                     w_up[layer].astype(BF16), w_down[layer].astype(BF16), row(ln2_w[layer]), row(ln2_b[layer]),
                     tm=tm_out)
    return x
```

```python
import functools
import math

import jax
import jax.numpy as jnp
from jax import lax
from jax.experimental import pallas as pl
from jax.experimental.pallas import tpu as pltpu

F32 = jnp.float32
BF16 = jnp.bfloat16

D_MODEL = 1024
HEAD_DIM = 64
LANES = 128
D_FF = 4 * D_MODEL
DEPTH = 2
ROPE_THETA = 10000.0
LN_EPS = 1e-5
NEG = -1e30
DEEPNORM_ALPHA = (2 * DEPTH) ** 0.25
QK_SCALE = HEAD_DIM ** -0.5

SWA_BLOCK = 128
SWA_WINDOW = 128
SWA_BLOCKS_PER_STEP = 4
MOBA_BLOCK = 256
MOBA_BLOCKS_PER_TILE = 4
MOBA_TOPK = 3
MOBA_PAST_PER_STEP = 4
MOBA_PV_LAG = 6
SB_BLOCK = 128
SB_WINDOW_BLKS = 3
SB_ROWS_PER_STEP = 2
SB_POW_CLAMP = 126.0
RET_CHUNK = 128
RET_HEADS = 4
OUT_SUBTILE = 256
RET_CHUNKS_PER_STEP = 8

BLK_A_Q, BLK_A_K, BLK_A_V = 0, 2, 3
BLK_B_Q, BLK_B_K, BLK_B_V = 4, 6, 8
BLK_C_Q, BLK_C_K, BLK_C_V = 10, 12, 14
BLK_D_Q, BLK_D_K, BLK_D_V, BLK_D_G = 16, 18, 20, 22
N_QKV_BLKS = 22
ROPED_BLKS = frozenset([0, 1, 2, 4, 5, 6, 7, 16, 17, 18, 19])
LOG2E = math.log2(math.e)
BLK_SCALE = {blk: QK_SCALE * LOG2E for blk in (0, 1, 4, 5, 10, 11)}
BLK_SCALE.update({18: QK_SCALE, 19: QK_SCALE})

SB_EXIT_LOG2 = -110.0 * LOG2E

VMEM_LIMIT = 56 * 1024 * 1024


def _dot(a, b, **kw):
    return jnp.dot(a, b, preferred_element_type=F32, **kw)


def _dot_nt(a, b, **kw):
    return lax.dot_general(a, b, (((1,), (1,)), ((), ())), preferred_element_type=F32, **kw)


def _lane_iota(shape):
    return lax.broadcasted_iota(jnp.int32, shape, len(shape) - 1)


def _row_iota(shape):
    return lax.broadcasted_iota(jnp.int32, shape, len(shape) - 2)


def _in_proj_kernel(x_ref, w_ref, cos_ref, sin_ref, qkv_ref, bq_ref, kmean_ref, g_ref, *, tm):
    xb = x_ref[0].astype(BF16)
    cos = cos_ref[...]
    sin = sin_ref[...]
    first_half = (_lane_iota((tm, LANES)) % HEAD_DIM) < (HEAD_DIM // 2)
    n_pairs = w_ref.shape[1] // (2 * LANES)
    for c in range(n_pairs):
        acc2 = _dot(xb, w_ref[:, c * 2 * LANES:(c + 1) * 2 * LANES])
        for half in range(2):
            blk = 2 * c + half
            acc = acc2[:, half * LANES:(half + 1) * LANES]
            if blk in ROPED_BLKS:
                partner = jnp.where(first_half,
                                    pltpu.roll(acc, LANES - HEAD_DIM // 2, 1),
                                    pltpu.roll(acc, HEAD_DIM // 2, 1))
                acc = acc * cos + partner * sin
            if blk in (BLK_B_Q, BLK_B_Q + 1):
                o = (blk - BLK_B_Q) * LANES
                bq_ref[0, :, o:o + LANES] = acc
            if blk in (BLK_B_K, BLK_B_K + 1):
                o = (blk - BLK_B_K) * LANES
                nblk = tm // MOBA_BLOCK
                means = [jnp.sum(acc[i * MOBA_BLOCK:(i + 1) * MOBA_BLOCK], axis=0, keepdims=True)
                         * (1.0 / MOBA_BLOCK) for i in range(nblk)]
                means.append(jnp.zeros((8 - nblk, LANES), F32))
                kmean_ref[0, :, o:o + LANES] = jnp.concatenate(means, axis=0)
            if blk in (BLK_D_G, BLK_D_G + 1):
                o = (blk - BLK_D_G) * LANES
                g_ref[0, :, o:o + LANES] = acc
            if blk < N_QKV_BLKS:
                if blk in BLK_SCALE:
                    acc = acc * BLK_SCALE[blk]
                qkv_ref[0, :, blk * LANES:(blk + 1) * LANES] = acc.astype(BF16)


def _in_proj(x, w_in_bf16, cos_t, sin_t, *, tm):
    b, s, d = x.shape
    nt = s // tm
    n_in = w_in_bf16.shape[1]
    kern = functools.partial(_in_proj_kernel, tm=tm)
    return pl.pallas_call(
        kern,
        grid=(b, nt),
        in_specs=[
            pl.BlockSpec((1, tm, d), lambda i, t: (i, t, 0)),
            pl.BlockSpec((d, n_in), lambda i, t: (0, 0), pipeline_mode=pl.Buffered(1)),
            pl.BlockSpec((tm, LANES), lambda i, t: (t, 0)),
            pl.BlockSpec((tm, LANES), lambda i, t: (t, 0)),
        ],
        out_specs=[
            pl.BlockSpec((1, tm, N_QKV_BLKS * LANES), lambda i, t: (i, t, 0)),
            pl.BlockSpec((1, tm, 2 * LANES), lambda i, t: (i, t, 0)),
            pl.BlockSpec((1, 8, 2 * LANES), lambda i, t: (i * nt + t, 0, 0)),
            pl.BlockSpec((1, tm, 2 * LANES), lambda i, t: (i, t, 0)),
        ],
        out_shape=[
            jax.ShapeDtypeStruct((b, s, N_QKV_BLKS * LANES), BF16),
            jax.ShapeDtypeStruct((b, s, 2 * LANES), F32),
            jax.ShapeDtypeStruct((b * nt, 8, 2 * LANES), F32),
            jax.ShapeDtypeStruct((b, s, 2 * LANES), F32),
        ],
        compiler_params=pltpu.CompilerParams(
            dimension_semantics=("arbitrary", "arbitrary"), vmem_limit_bytes=VMEM_LIMIT),
        name="in_proj",
    )(x, w_in_bf16, cos_t, sin_t)


def _swa_kernel(sinks_ref, q_ref, k_ref, v_ref, o_ref, qt_ref, vt_ref, bias_ref, *, seq):
    w = SWA_BLOCK
    nb = seq // w
    hd = HEAD_DIM
    n_kv = k_ref.shape[-1] // hd
    chunk = 2 * w

    def prep(c, _):
        rows = pl.ds(pl.multiple_of(c * chunk, chunk), chunk)
        qt_ref[:, rows] = q_ref[0, rows, :].astype(F32).T.astype(BF16)
        vt = v_ref[0, rows, :].astype(F32).T.astype(BF16)
        ones = jnp.ones((16, chunk), BF16)
        for kv in range(n_kv):
            vt_ref[kv, :, rows] = jnp.concatenate([vt[kv * hd:(kv + 1) * hd], ones], axis=0)
        return 0

    lax.fori_loop(0, seq // chunk, prep, 0)

    delta = (_lane_iota((2 * w, 2 * w)) % w) - _row_iota((2 * w, 2 * w))
    for t in range(2):
        dist = delta + t * w
        bias_ref[t] = jnp.where((dist >= 0) & (dist < SWA_WINDOW), 0.0, NEG)
    first_head = _lane_iota((1, 2 * w)) < w
    sink = [jnp.where(first_head, sinks_ref[2 * kv], sinks_ref[2 * kv + 1]) * LOG2E
            for kv in range(n_kv)]
    zeros = jnp.zeros((hd, 2 * w), BF16)

    def body(it, _):
        jobs = []
        for u in range(SWA_BLOCKS_PER_STEP):
            i = it * SWA_BLOCKS_PER_STEP + u
            qs = pl.multiple_of(i * w, w)
            ks = pl.multiple_of(jnp.maximum(i - 1, 0) * w, w)
            for kv in range(n_kv):
                qq = jnp.concatenate([qt_ref[(2 * kv) * hd:(2 * kv + 1) * hd, pl.ds(qs, w)],
                                      qt_ref[(2 * kv + 1) * hd:(2 * kv + 2) * hd, pl.ds(qs, w)]], axis=1)
                rhs = jnp.concatenate([qq, zeros] if kv == 0 else [zeros, qq], axis=0)
                jobs.append((qs, ks, jnp.minimum(i, 1), kv, rhs))
        s_all = [_dot(k_ref[0, pl.ds(ks, 2 * w), :], rhs) for _, ks, _, _, rhs in jobs]
        mp = []
        for s, (_, _, t, kv, _) in zip(s_all, jobs):
            s = s + bias_ref[t]
            m = jnp.maximum(jnp.max(s, axis=0, keepdims=True), sink[kv])
            mp.append((m, jnp.exp2((s - m).astype(BF16))))
        pv_all = [_dot(vt_ref[kv, :, pl.ds(ks, 2 * w)], p) for (_, p), (_, ks, _, kv, _) in zip(mp, jobs)]
        for pv, (m, _), (qs, _, _, kv, _) in zip(pv_all, mp, jobs):
            den = pv[hd:hd + 1] + jnp.exp2(sink[kv] - m)
            o = pv[:hd] * (1.0 / den)
            o_t = jnp.concatenate([o[:, :w], o[:, w:]], axis=0)
            o_ref[0, pl.ds(qs, w), kv * LANES:(kv + 1) * LANES] = o_t.T.astype(o_ref.dtype)
        return 0

    lax.fori_loop(0, nb // SWA_BLOCKS_PER_STEP, body, 0)


def _swa(qkv, sinks):
    b, s, _ = qkv.shape
    kern = functools.partial(_swa_kernel, seq=s)
    return pl.pallas_call(
        kern,
        grid=(b,),
        in_specs=[
            pl.BlockSpec(memory_space=pltpu.SMEM),
            pl.BlockSpec((1, s, 2 * LANES), lambda i: (i, 0, BLK_A_Q // 2)),
            pl.BlockSpec((1, s, LANES), lambda i: (i, 0, BLK_A_K)),
            pl.BlockSpec((1, s, LANES), lambda i: (i, 0, BLK_A_V)),
        ],
        out_specs=pl.BlockSpec((1, s, 2 * LANES), lambda i: (i, 0, 0)),
        out_shape=jax.ShapeDtypeStruct((b, s, 2 * LANES), BF16),
        scratch_shapes=[
            pltpu.VMEM((2 * LANES, s), BF16),
            pltpu.VMEM((2, HEAD_DIM + 16, s), BF16),
            pltpu.VMEM((2, 2 * SWA_BLOCK, 2 * SWA_BLOCK), F32),
        ],
        compiler_params=pltpu.CompilerParams(
            dimension_semantics=("arbitrary",), vmem_limit_bytes=VMEM_LIMIT),
        name="swa",
    )(sinks, qkv, qkv, qkv)


def _moba_kernel(q_ref, k_ref, v_ref, q32_ref, kmean_ref, o_ref, ka_ref, vt_ref, qt_ref, *, seq):
    lb = MOBA_BLOCK
    nb = seq // lb

    n_pairs = q_ref.shape[-1] // LANES
    n_heads = 2 * n_pairs
    hd = HEAD_DIM

    def prep(c, _):
        rows = pl.ds(pl.multiple_of(c * lb, lb), lb)
        lane = _lane_iota((lb, LANES))
        ones = jnp.ones((16, lb), BF16)
        for pr in range(n_pairs):
            cols = slice(pr * LANES, (pr + 1) * LANES)
            kt = k_ref[0, rows, cols].astype(F32)
            ka_ref[2 * pr, rows, :] = jnp.where(lane < hd, kt, jnp.where(lane - hd == c, 1.0, 0.0)).astype(BF16)
            ka_ref[2 * pr + 1, rows, :] = jnp.where(lane >= hd, kt, jnp.where(lane == c, 1.0, 0.0)).astype(BF16)
            vt = v_ref[0, rows, cols].astype(F32).T.astype(BF16)
            vt_ref[2 * pr, :, rows] = jnp.concatenate([vt[:hd], ones], axis=0)
            vt_ref[2 * pr + 1, :, rows] = jnp.concatenate([vt[hd:], ones], axis=0)
            qt_ref[cols, rows] = q_ref[0, rows, cols].astype(F32).T.astype(BF16)
        return 0

    lax.fori_loop(0, nb, prep, 0)

    lane_lo_k = _lane_iota((nb, LANES)) < hd
    kmh = []
    for pr in range(n_pairs):
        km = kmean_ref[0, :, pr * LANES:(pr + 1) * LANES]
        kmh += [jnp.where(lane_lo_k, km, 0.0), jnp.where(lane_lo_k, 0.0, km)]
    blk_iota = _row_iota((nb, lb))
    causal_t = _row_iota((lb, lb)) <= _lane_iota((lb, lb))
    pad = jnp.zeros((hd - nb, lb), BF16)
    halves = range(MOBA_BLOCKS_PER_TILE)

    def tile(j, _):
        first = j * MOBA_BLOCKS_PER_TILE
        qs = [pl.multiple_of((first + u) * lb, lb) for u in halves]

        qa = {}
        for u in halves:
            valid = blk_iota < first + u
            bias_t = []
            for h in range(n_heads):
                q32 = q32_ref[0, pl.ds(qs[u], lb), (h // 2) * LANES:(h // 2 + 1) * LANES]
                gate = _dot_nt(kmh[h], q32, precision=lax.Precision.HIGHEST)
                gm = jnp.where(valid, gate, -jnp.inf)
                rank = jnp.zeros((nb, lb), F32)
                for m_idx in range(nb):
                    row = gm[m_idx:m_idx + 1, :]
                    tie = jnp.where(blk_iota > m_idx, 1.0, 0.0)
                    rank = rank + jnp.where(row > gm, 1.0, jnp.where(row == gm, tie, 0.0))
                routed = jnp.where(rank < MOBA_TOPK, 0.0, NEG)
                bias_t.append(jnp.where(valid, routed, 0.0).astype(BF16))
            for pr in range(n_pairs):
                qt = qt_ref[pr * LANES:(pr + 1) * LANES, pl.ds(qs[u], lb)]
                qa[(2 * pr, u)] = jnp.concatenate([qt[:hd], bias_t[2 * pr], pad], axis=0)
                qa[(2 * pr + 1, u)] = jnp.concatenate([bias_t[2 * pr + 1], pad, qt[hd:]], axis=0)
        chains = [(h, u) for h in range(n_heads) for u in halves]

        def local_many(jobs):
            out = [None] * len(jobs)
            s_all = [None] * len(jobs)
            for idx in range(len(jobs) + MOBA_PV_LAG):
                if idx < len(jobs):
                    h, u, ks, _ = jobs[idx]
                    s_all[idx] = _dot(ka_ref[h, pl.ds(ks, lb), :], qa[(h, u)])
                k = idx - MOBA_PV_LAG
                if k >= 0:
                    h, _, ks, masked = jobs[k]
                    s = jnp.where(causal_t, s_all[k], NEG) if masked else s_all[k]
                    m = jnp.max(s, axis=0, keepdims=True)
                    p = jnp.exp2((s - m).astype(BF16))
                    out[k] = (m, _dot(vt_ref[h, :, pl.ds(ks, lb)], p))
            return out

        def merge(parts):
            m_new = parts[0][0]
            for m, _ in parts[1:]:
                m_new = jnp.maximum(m_new, m)
            acc = None
            for m, pv in parts:
                term = pv * jnp.exp2(m - m_new)
                acc = term if acc is None else acc + term
            return m_new, acc

        jobs = [(h, u, qs[v], v == u) for h, u in chains for v in range(u + 1)]
        parts = local_many(jobs)
        state = []
        for h, u in chains:
            state += merge([part for part, job in zip(parts, jobs) if job[:2] == (h, u)])

        def past(n, st):
            ks = [pl.multiple_of((n * MOBA_PAST_PER_STEP + v) * lb, lb) for v in range(MOBA_PAST_PER_STEP)]
            parts = local_many([(h, u, ks[v], False) for v in range(MOBA_PAST_PER_STEP) for h, u in chains])
            out = []
            for c in range(len(chains)):
                out += merge([(st[2 * c], st[2 * c + 1])] + parts[c::len(chains)])
            return tuple(out)

        st = lax.fori_loop(0, first // MOBA_PAST_PER_STEP, past, tuple(state))
        for u in halves:
            accs = [st[2 * chains.index((h, u)) + 1] for h in range(n_heads)]
            o_t = jnp.concatenate([a[:hd] * (1.0 / a[hd:hd + 1]) for a in accs], axis=0)
            o_ref[0, pl.ds(qs[u], lb), :] = o_t.T.astype(o_ref.dtype)
        return 0

    lax.fori_loop(0, nb // MOBA_BLOCKS_PER_TILE, tile, 0)


def _moba(qkv, bq32, kmean):
    b, s, _ = qkv.shape
    nb = s // MOBA_BLOCK
    kern = functools.partial(_moba_kernel, seq=s)
    return pl.pallas_call(
        kern,
        grid=(b,),
        in_specs=[
            pl.BlockSpec((1, s, 2 * LANES), lambda i: (i, 0, BLK_B_Q // 2)),
            pl.BlockSpec((1, s, 2 * LANES), lambda i: (i, 0, BLK_B_K // 2)),
            pl.BlockSpec((1, s, 2 * LANES), lambda i: (i, 0, BLK_B_V // 2)),
            pl.BlockSpec((1, s, 2 * LANES), lambda i: (i, 0, 0)),
            pl.BlockSpec((1, nb, 2 * LANES), lambda i: (i, 0, 0)),
        ],
        out_specs=pl.BlockSpec((1, s, 2 * LANES), lambda i: (i, 0, 0)),
        out_shape=jax.ShapeDtypeStruct((b, s, 2 * LANES), BF16),
        scratch_shapes=[
            pltpu.VMEM((4, s, LANES), BF16),
            pltpu.VMEM((4, HEAD_DIM + 16, s), BF16),
            pltpu.VMEM((2 * LANES, s), BF16),
        ],
        compiler_params=pltpu.CompilerParams(
            dimension_semantics=("arbitrary",), vmem_limit_bytes=VMEM_LIMIT),
        name="moba",
    )(qkv, qkv, qkv, bq32, kmean)


def _sb_kernel(q_ref, k_ref, v_ref, o_ref, km_ref, tri_ref, carry_ref, acc_ref, *, seq):
    w = SB_BLOCK
    nb = seq // w
    win = SB_WINDOW_BLKS * w
    chunk = 512

    n_rows = q_ref.shape[0]
    n_pairs = q_ref.shape[-1] // LANES
    n_heads = 2 * n_pairs
    chains = [(b, h) for b in range(n_rows) for h in range(n_heads)]

    def prep(c, _):
        rows = pl.ds(pl.multiple_of(c * chunk, chunk), chunk)
        lane_lo = _lane_iota((chunk, LANES)) < HEAD_DIM
        for b in range(n_rows):
            for pr in range(n_pairs):
                kt = k_ref[b, rows, pr * LANES:(pr + 1) * LANES].astype(F32)
                km_ref[b * n_heads + 2 * pr, rows, :] = jnp.where(lane_lo, kt, 0.0).astype(BF16)
                km_ref[b * n_heads + 2 * pr + 1, rows, :] = jnp.where(lane_lo, 0.0, kt).astype(BF16)
        return 0

    lax.fori_loop(0, seq // chunk, prep, 0)

    tri_r = _row_iota((2 * w, 2 * w)) % w
    tri_c = _lane_iota((2 * w, 2 * w))
    tri_ref[...] = jnp.where((tri_c >= w) | (tri_r >= tri_c), 1.0, 0.0).astype(BF16)
    lane_lo = _lane_iota((w, LANES)) < HEAD_DIM
    row = _row_iota((w, win))
    kcol = _lane_iota((w, win))

    def tile(i, _):
        qs = pl.multiple_of(i * w, w)
        qpos = qs + row
        carry_ref[...] = jnp.zeros_like(carry_ref)
        acc_ref[...] = jnp.zeros_like(acc_ref)

        def cond(st):
            top, live = st
            return (top > 0) & live

        def body(st):
            top, _ = st
            bottom = jnp.maximum(top - SB_WINDOW_BLKS, 0)
            ws = pl.multiple_of(bottom * w, w)
            mask = (ws + kcol) < jnp.minimum(qpos, top * w)
            tri = tri_ref[...]
            z = [_dot_nt(q_ref[b, pl.ds(qs, w), (h // 2) * LANES:(h // 2 + 1) * LANES],
                         km_ref[b * n_heads + h, pl.ds(ws, win), :]) for b, h in chains]
            zm = [jnp.where(mask, z_c, NEG) for z_c in z]
            split = []
            for ch in range(len(chains)):
                neg_log_fail = jnp.maximum(zm[ch], jnp.log2(1.0 + jnp.exp2(jnp.minimum(zm[ch], SB_POW_CLAMP))))
                for c in range(SB_WINDOW_BLKS):
                    nlf = neg_log_fail[:, c * w:(c + 1) * w]
                    hi = pltpu.bitcast(pltpu.bitcast(nlf, jnp.uint32) & jnp.uint32(0xFFFF0000), F32)
                    split.append(jnp.concatenate([hi.astype(BF16), (nlf - hi).astype(BF16)], axis=1))
            r = [_dot(x, tri) for x in split]
            a_all = []
            for ch in range(len(chains)):
                run = carry_ref[ch]
                a = [None] * SB_WINDOW_BLKS
                for c in reversed(range(SB_WINDOW_BLKS)):
                    rc = r[ch * SB_WINDOW_BLKS + c]
                    a[c] = jnp.exp2(zm[ch][:, c * w:(c + 1) * w] - rc[:, :w] - run)
                    run = run + rc[:, w:]
                carry_ref[ch] = run
                a_all.append(jnp.concatenate(a, axis=1).astype(BF16))
            live = None
            for ch, (b, h) in enumerate(chains):
                vv = v_ref[b, pl.ds(ws, win), (h // 2) * LANES:(h // 2 + 1) * LANES]
                acc_ref[ch] = acc_ref[ch] + _dot(a_all[ch], vv)
                low = jnp.min(carry_ref[ch])
                live = low if live is None else jnp.minimum(live, low)
            return bottom, live < -SB_EXIT_LOG2

        lax.while_loop(cond, body, (i + 1, True))
        for b in range(n_rows):
            for pr in range(n_pairs):
                o_ref[b, pl.ds(qs, w), pr * LANES:(pr + 1) * LANES] = jnp.where(
                    lane_lo, acc_ref[b * n_heads + 2 * pr], acc_ref[b * n_heads + 2 * pr + 1]).astype(o_ref.dtype)
        return 0

    lax.fori_loop(0, nb, tile, 0)


def _sb(qkv):
    b, s, _ = qkv.shape
    kern = functools.partial(_sb_kernel, seq=s)
    return pl.pallas_call(
        kern,
        grid=(b // SB_ROWS_PER_STEP,),
        in_specs=[
            pl.BlockSpec((SB_ROWS_PER_STEP, s, 2 * LANES), lambda i: (i, 0, BLK_C_Q // 2)),
            pl.BlockSpec((SB_ROWS_PER_STEP, s, 2 * LANES), lambda i: (i, 0, BLK_C_K // 2)),
            pl.BlockSpec((SB_ROWS_PER_STEP, s, 2 * LANES), lambda i: (i, 0, BLK_C_V // 2)),
        ],
        out_specs=pl.BlockSpec((SB_ROWS_PER_STEP, s, 2 * LANES), lambda i: (i, 0, 0)),
        out_shape=jax.ShapeDtypeStruct((b, s, 2 * LANES), BF16),
        scratch_shapes=[
            pltpu.VMEM((4 * SB_ROWS_PER_STEP, s, LANES), BF16),
            pltpu.VMEM((2 * SB_BLOCK, 2 * SB_BLOCK), BF16),
            pltpu.VMEM((4 * SB_ROWS_PER_STEP, SB_BLOCK, SB_BLOCK), F32),
            pltpu.VMEM((4 * SB_ROWS_PER_STEP, SB_BLOCK, LANES), F32),
        ],
        compiler_params=pltpu.CompilerParams(
            dimension_semantics=("arbitrary",), vmem_limit_bytes=VMEM_LIMIT),
        name="stick_breaking",
    )(qkv, qkv, qkv)


def _ret_kernel(q_ref, k_ref, v_ref, g_ref, decay_ref, qw_ref, kw_ref, gc_ref, o_ref, state_ref, avg_ref, *, seq):
    c = RET_CHUNK
    n = seq // c
    lane_lo = _lane_iota((c, LANES)) < HEAD_DIM
    state_ref[...] = jnp.zeros_like(state_ref)

    def split_heads(t):
        return jnp.concatenate([jnp.where(lane_lo, t, 0.0), jnp.where(lane_lo, 0.0, t)], axis=0).astype(BF16)

    avg_r = (_row_iota((2 * LANES, LANES)) % LANES) // HEAD_DIM
    avg_c = _lane_iota((2 * LANES, LANES)) // HEAD_DIM
    avg_ref[...] = jnp.where(avg_r == avg_c, 1.0 / HEAD_DIM, 0.0).astype(BF16)

    def half_stat(t):
        hi = pltpu.bitcast(pltpu.bitcast(t, jnp.uint32) & jnp.uint32(0xFFFF0000), F32)
        return _dot(jnp.concatenate([hi.astype(BF16), (t - hi).astype(BF16)], axis=1), avg_ref[...])

    def step(it, _):
        rows = [pl.ds(pl.multiple_of((it * RET_CHUNKS_PER_STEP + u) * c, c), c) for u in range(RET_CHUNKS_PER_STEP)]
        cols = [slice(p * LANES, (p + 1) * LANES) for p in range(2)]
        q2, v2, scores, kv = {}, {}, {}, {}
        st = [state_ref[p] for p in range(2)]
        jobs = [(u, p) for u in range(RET_CHUNKS_PER_STEP) for p in range(2)]
        for u, p in jobs:
            j = (u, p)
            q2[j] = q_ref[0, rows[u], cols[p]]
            v2[j] = v_ref[0, rows[u], cols[p]]
            k2f = k_ref[0, rows[u], cols[p]].astype(F32)
            scores[j] = _dot_nt(q2[j], split_heads(k2f))
            kv[j] = _dot((k2f * kw_ref[p]).T.astype(BF16), v2[j])
        y = {}
        for u, p in jobs:
            j = (u, p)
            inner = _dot((scores[j] * decay_ref[p]).astype(BF16), split_heads(v2[j].astype(F32)))
            cross = _dot(q2[j], st[p].astype(BF16))
            gcm = gc_ref[p]
            st[p] = st[p] * gcm + jnp.where(gcm > 0.0, kv[j], 0.0)
            y[j] = inner + cross * qw_ref[p]
        mu = {j: half_stat(y[j]) for j in jobs}
        yc = {j: y[j] - mu[j] for j in jobs}
        var = {j: half_stat(yc[j] * yc[j]) for j in jobs}
        for u, p in jobs:
            yn = yc[(u, p)] * lax.rsqrt(var[(u, p)] + LN_EPS)
            g = g_ref[0, rows[u], cols[p]]
            gate = g * (1.0 / (1.0 + jnp.exp(-g)))
            o_ref[0, rows[u], cols[p]] = (yn * gate).astype(o_ref.dtype)
        for p in range(2):
            state_ref[p] = st[p]
        return 0

    lax.fori_loop(0, n // RET_CHUNKS_PER_STEP, step, 0)


def _ret(qkv, g, decay, qw, kw, gcm):
    b, s, _ = qkv.shape
    kern = functools.partial(_ret_kernel, seq=s)
    full3 = lambda a: pl.BlockSpec(a.shape, lambda i: (0, 0, 0))
    return pl.pallas_call(
        kern,
        grid=(b,),
        in_specs=[
            pl.BlockSpec((1, s, 2 * LANES), lambda i: (i, 0, BLK_D_Q // 2)),
            pl.BlockSpec((1, s, 2 * LANES), lambda i: (i, 0, BLK_D_K // 2)),
            pl.BlockSpec((1, s, 2 * LANES), lambda i: (i, 0, BLK_D_V // 2)),
            pl.BlockSpec((1, s, 2 * LANES), lambda i: (i, 0, 0)),
            full3(decay), full3(qw), full3(kw), full3(gcm),
        ],
        out_specs=pl.BlockSpec((1, s, 2 * LANES), lambda i: (i, 0, 0)),
        out_shape=jax.ShapeDtypeStruct((b, s, 2 * LANES), BF16),
        scratch_shapes=[pltpu.VMEM((2, LANES, LANES), F32), pltpu.VMEM((2 * LANES, LANES), BF16)],
        compiler_params=pltpu.CompilerParams(
            dimension_semantics=("arbitrary",), vmem_limit_bytes=VMEM_LIMIT),
        name="retention",
    )(qkv, qkv, qkv, g, decay, qw, kw, gcm)


def _layer_norm(t, w, b):
    mu = jnp.mean(t, axis=-1, keepdims=True)
    tc = t - mu
    var = jnp.mean(tc * tc, axis=-1, keepdims=True)
    return tc * lax.rsqrt(var + LN_EPS) * w + b


def _out_mlp_kernel(ya_ref, yb_ref, yc_ref, yd_ref, x_ref, wo_ref, l1w_ref, l1b_ref,
                    wu_ref, wd_ref, l2w_ref, l2b_ref, o_ref):
    width = ya_ref.shape[-1]
    ff = wu_ref.shape[1]
    step = 1024
    n_sub = x_ref.shape[1] // OUT_SUBTILE
    rows = [slice(i * OUT_SUBTILE, (i + 1) * OUT_SUBTILE) for i in range(n_sub)]

    def out_proj(r):
        h = None
        for idx, y_ref in enumerate((ya_ref, yb_ref, yc_ref, yd_ref)):
            part = _dot(y_ref[0, r, :], wo_ref[idx * width:(idx + 1) * width, :])
            h = part if h is None else h + part
        return h

    def mlp(x1b):
        acc = None
        for c in range(ff // step):
            hid = _dot(x1b, wu_ref[:, c * step:(c + 1) * step])
            hid = jnp.square(jnp.maximum(hid, 0.0)).astype(BF16)
            part = _dot(hid, wd_ref[c * step:(c + 1) * step, :])
            acc = part if acc is None else acc + part
        return acc

    h = [out_proj(r) for r in rows]
    x1 = [None] * n_sub
    acc = [None] * n_sub
    for i in range(n_sub):
        x1[i] = _layer_norm(DEEPNORM_ALPHA * x_ref[0, rows[i], :] + h[i], l1w_ref[...], l1b_ref[...])
        acc[i] = mlp(x1[i].astype(BF16))
        if i > 0:
            o_ref[0, rows[i - 1], :] = _layer_norm(DEEPNORM_ALPHA * x1[i - 1] + acc[i - 1], l2w_ref[...], l2b_ref[...])
    o_ref[0, rows[-1], :] = _layer_norm(DEEPNORM_ALPHA * x1[-1] + acc[-1], l2w_ref[...], l2b_ref[...])


def _out_mlp(ya, yb, yc, yd, x, wo, l1w, l1b, wu, wd, l2w, l2b, *, tm):
    b, s, d = x.shape
    nt = s // tm
    ytile = pl.BlockSpec((1, tm, ya.shape[-1]), lambda i, t: (i, t, 0))
    const = lambda a: pl.BlockSpec(a.shape, lambda i, t: (0, 0), pipeline_mode=pl.Buffered(1))
    return pl.pallas_call(
        _out_mlp_kernel,
        grid=(b, nt),
        in_specs=[ytile, ytile, ytile, ytile,
                  pl.BlockSpec((1, tm, d), lambda i, t: (i, t, 0)),
                  const(wo), const(l1w), const(l1b), const(wu), const(wd), const(l2w), const(l2b)],
        out_specs=pl.BlockSpec((1, tm, d), lambda i, t: (i, t, 0)),
        out_shape=jax.ShapeDtypeStruct((b, s, d), F32),
        compiler_params=pltpu.CompilerParams(
            dimension_semantics=("arbitrary", "arbitrary"), vmem_limit_bytes=VMEM_LIMIT),
        name="out_mlp",
    )(ya, yb, yc, yd, x, wo, l1w, l1b, wu, wd, l2w, l2b)


def _rope_tables(s):
    half = HEAD_DIM // 2
    inv = 1.0 / (ROPE_THETA ** (jnp.arange(0, HEAD_DIM, 2, dtype=F32) / HEAD_DIM))
    ang = jnp.arange(s, dtype=F32)[:, None] * inv[None, :]
    cos = jnp.cos(ang)
    sin = jnp.sin(ang)
    cos_t = jnp.tile(cos, (1, LANES // half))
    sin_t = jnp.tile(jnp.concatenate([-sin, sin], axis=1), (1, LANES // HEAD_DIM))
    return cos_t, sin_t


def _retention_tables():
    c = RET_CHUNK
    h = RET_HEADS
    log_g = jnp.log(1.0 - 2.0 ** (-5.0 - jnp.arange(h, dtype=F32)))
    idx = jnp.arange(c, dtype=F32)
    diff = idx[:, None] - idx[None, :]
    decay = jnp.where(diff >= 0, jnp.exp(jnp.maximum(diff, 0.0)[None] * log_g[:, None, None]), 0.0)
    k_w = jnp.exp((c - 1.0 - idx)[:, None] * log_g[None, :])
    q_w = jnp.exp((idx + 1.0)[:, None] * log_g[None, :])
    g_c = jnp.exp(c * log_g)
    per_lane = lambda t: jnp.repeat(t, HEAD_DIM, axis=1).reshape(c, h // 2, LANES).transpose(1, 0, 2)
    head_of = jnp.arange(LANES) // HEAD_DIM
    same = head_of[:, None] == head_of[None, :]
    gcm = jnp.stack([jnp.where(same, g_c[2 * p + head_of][None, :], 0.0) for p in range(h // 2)])
    decay_pairs = jnp.concatenate([decay[0::2], decay[1::2]], axis=2)
    return decay_pairs, per_lane(q_w), per_lane(k_w), gcm


def kernel(x, w_in, swa_sinks, w_out, ln1_w, ln1_b, w_up, w_down, ln2_w, ln2_b):
    b, s, d = x.shape
    cos_t, sin_t = _rope_tables(s)
    decay, qw, kw, gcm = _retention_tables()
    row = lambda t: t.reshape(1, -1)
    tm_in = 512
    tm_out = 2 * OUT_SUBTILE
    for layer in range(w_in.shape[0]):
        qkv, bq32, kmean8, g = _in_proj(x, w_in[layer].astype(BF16), cos_t, sin_t, tm=tm_in)
        nblk = tm_in // MOBA_BLOCK
        kmean = kmean8.reshape(b, s // tm_in, 8, 2 * LANES)[:, :, :nblk].reshape(b, s // MOBA_BLOCK, 2 * LANES)
        ya = _swa(qkv, swa_sinks[layer])
        yb = _moba(qkv, bq32, kmean)
        yc = _sb(qkv)
        yd = _ret(qkv, g, decay, qw, kw, gcm)
        x = _out_mlp(ya, yb, yc, yd, x, w_out[layer].astype(BF16), row(ln1_w[layer]), row(ln1_b[layer]),
                     w_up[layer].astype(BF16), w_down[layer].astype(BF16), row(ln2_w[layer]), row(ln2_b[layer]),
                     tm=tm_out)
    return x
```

```python
import functools
import math

import jax
import jax.numpy as jnp
from jax import lax
from jax.experimental import pallas as pl
from jax.experimental.pallas import tpu as pltpu

F32 = jnp.float32
BF16 = jnp.bfloat16

D_MODEL = 1024
HEAD_DIM = 64
LANES = 128
D_FF = 4 * D_MODEL
DEPTH = 2
ROPE_THETA = 10000.0
LN_EPS = 1e-5
NEG = -1e30
DEEPNORM_ALPHA = (2 * DEPTH) ** 0.25
QK_SCALE = HEAD_DIM ** -0.5

SWA_BLOCK = 128
SWA_WINDOW = 128
SWA_BLOCKS_PER_STEP = 4
MOBA_BLOCK = 256
MOBA_BLOCKS_PER_TILE = 4
MOBA_TOPK = 3
MOBA_PAST_PER_STEP = 4
MOBA_PV_LAG = 6
SB_BLOCK = 128
SB_WINDOW_BLKS = 3
SB_ROWS_PER_STEP = 2
SB_STREAMS = 2
SB_POW_CLAMP = 126.0
RET_CHUNK = 128
RET_HEADS = 4
OUT_SUBTILE = 256
RET_CHUNKS_PER_STEP = 8

BLK_A_Q, BLK_A_K, BLK_A_V = 0, 2, 3
BLK_B_Q, BLK_B_K, BLK_B_V = 4, 6, 8
BLK_C_Q, BLK_C_K, BLK_C_V = 10, 12, 14
BLK_D_Q, BLK_D_K, BLK_D_V, BLK_D_G = 16, 18, 20, 22
N_QKV_BLKS = 22
ROPED_BLKS = frozenset([0, 1, 2, 4, 5, 6, 7, 16, 17, 18, 19])
LOG2E = math.log2(math.e)
BLK_SCALE = {blk: QK_SCALE * LOG2E for blk in (0, 1, 4, 5, 10, 11)}
BLK_SCALE.update({18: QK_SCALE, 19: QK_SCALE})

SB_EXIT_LOG2 = -110.0 * LOG2E

VMEM_LIMIT = 56 * 1024 * 1024


def _dot(a, b, **kw):
    return jnp.dot(a, b, preferred_element_type=F32, **kw)


def _dot_nt(a, b, **kw):
    return lax.dot_general(a, b, (((1,), (1,)), ((), ())), preferred_element_type=F32, **kw)


def _lane_iota(shape):
    return lax.broadcasted_iota(jnp.int32, shape, len(shape) - 1)


def _row_iota(shape):
    return lax.broadcasted_iota(jnp.int32, shape, len(shape) - 2)


def _in_proj_kernel(x_ref, w_ref, cos_ref, sin_ref, qkv_ref, bq_ref, kmean_ref, g_ref, *, tm):
    xb = x_ref[0].astype(BF16)
    cos = cos_ref[...]
    sin = sin_ref[...]
    first_half = (_lane_iota((tm, LANES)) % HEAD_DIM) < (HEAD_DIM // 2)
    n_pairs = w_ref.shape[1] // (2 * LANES)
    for c in range(n_pairs):
        acc2 = _dot(xb, w_ref[:, c * 2 * LANES:(c + 1) * 2 * LANES])
        for half in range(2):
            blk = 2 * c + half
            acc = acc2[:, half * LANES:(half + 1) * LANES]
            if blk in ROPED_BLKS:
                partner = jnp.where(first_half,
                                    pltpu.roll(acc, LANES - HEAD_DIM // 2, 1),
                                    pltpu.roll(acc, HEAD_DIM // 2, 1))
                acc = acc * cos + partner * sin
            if blk in (BLK_B_Q, BLK_B_Q + 1):
                o = (blk - BLK_B_Q) * LANES
                bq_ref[0, :, o:o + LANES] = acc
            if blk in (BLK_B_K, BLK_B_K + 1):
                o = (blk - BLK_B_K) * LANES
                nblk = tm // MOBA_BLOCK
                means = [jnp.sum(acc[i * MOBA_BLOCK:(i + 1) * MOBA_BLOCK], axis=0, keepdims=True)
                         * (1.0 / MOBA_BLOCK) for i in range(nblk)]
                means.append(jnp.zeros((8 - nblk, LANES), F32))
                kmean_ref[0, :, o:o + LANES] = jnp.concatenate(means, axis=0)
            if blk in (BLK_D_G, BLK_D_G + 1):
                o = (blk - BLK_D_G) * LANES
                g_ref[0, :, o:o + LANES] = acc
            if blk < N_QKV_BLKS:
                if blk in BLK_SCALE:
                    acc = acc * BLK_SCALE[blk]
                qkv_ref[0, :, blk * LANES:(blk + 1) * LANES] = acc.astype(BF16)


def _in_proj(x, w_in_bf16, cos_t, sin_t, *, tm):
    b, s, d = x.shape
    nt = s // tm
    n_in = w_in_bf16.shape[1]
    kern = functools.partial(_in_proj_kernel, tm=tm)
    return pl.pallas_call(
        kern,
        grid=(b, nt),
        in_specs=[
            pl.BlockSpec((1, tm, d), lambda i, t: (i, t, 0)),
            pl.BlockSpec((d, n_in), lambda i, t: (0, 0), pipeline_mode=pl.Buffered(1)),
            pl.BlockSpec((tm, LANES), lambda i, t: (t, 0)),
            pl.BlockSpec((tm, LANES), lambda i, t: (t, 0)),
        ],
        out_specs=[
            pl.BlockSpec((1, tm, N_QKV_BLKS * LANES), lambda i, t: (i, t, 0)),
            pl.BlockSpec((1, tm, 2 * LANES), lambda i, t: (i, t, 0)),
            pl.BlockSpec((1, 8, 2 * LANES), lambda i, t: (i * nt + t, 0, 0)),
            pl.BlockSpec((1, tm, 2 * LANES), lambda i, t: (i, t, 0)),
        ],
        out_shape=[
            jax.ShapeDtypeStruct((b, s, N_QKV_BLKS * LANES), BF16),
            jax.ShapeDtypeStruct((b, s, 2 * LANES), F32),
            jax.ShapeDtypeStruct((b * nt, 8, 2 * LANES), F32),
            jax.ShapeDtypeStruct((b, s, 2 * LANES), F32),
        ],
        compiler_params=pltpu.CompilerParams(
            dimension_semantics=("arbitrary", "arbitrary"), vmem_limit_bytes=VMEM_LIMIT),
        name="in_proj",
    )(x, w_in_bf16, cos_t, sin_t)


def _swa_kernel(sinks_ref, q_ref, k_ref, v_ref, o_ref, qt_ref, vt_ref, bias_ref, *, seq):
    w = SWA_BLOCK
    nb = seq // w
    hd = HEAD_DIM
    n_kv = k_ref.shape[-1] // hd
    chunk = 2 * w

    def prep(c, _):
        rows = pl.ds(pl.multiple_of(c * chunk, chunk), chunk)
        qt_ref[:, rows] = q_ref[0, rows, :].astype(F32).T.astype(BF16)
        vt = v_ref[0, rows, :].astype(F32).T.astype(BF16)
        ones = jnp.ones((16, chunk), BF16)
        for kv in range(n_kv):
            vt_ref[kv, :, rows] = jnp.concatenate([vt[kv * hd:(kv + 1) * hd], ones], axis=0)
        return 0

    lax.fori_loop(0, seq // chunk, prep, 0)

    delta = (_lane_iota((2 * w, 2 * w)) % w) - _row_iota((2 * w, 2 * w))
    for t in range(2):
        dist = delta + t * w
        bias_ref[t] = jnp.where((dist >= 0) & (dist < SWA_WINDOW), 0.0, NEG)
    first_head = _lane_iota((1, 2 * w)) < w
    sink = [jnp.where(first_head, sinks_ref[2 * kv], sinks_ref[2 * kv + 1]) * LOG2E
            for kv in range(n_kv)]
    zeros = jnp.zeros((hd, 2 * w), BF16)

    def body(it, _):
        jobs = []
        for u in range(SWA_BLOCKS_PER_STEP):
            i = it * SWA_BLOCKS_PER_STEP + u
            qs = pl.multiple_of(i * w, w)
            ks = pl.multiple_of(jnp.maximum(i - 1, 0) * w, w)
            for kv in range(n_kv):
                qq = jnp.concatenate([qt_ref[(2 * kv) * hd:(2 * kv + 1) * hd, pl.ds(qs, w)],
                                      qt_ref[(2 * kv + 1) * hd:(2 * kv + 2) * hd, pl.ds(qs, w)]], axis=1)
                rhs = jnp.concatenate([qq, zeros] if kv == 0 else [zeros, qq], axis=0)
                jobs.append((qs, ks, jnp.minimum(i, 1), kv, rhs))
        s_all = [_dot(k_ref[0, pl.ds(ks, 2 * w), :], rhs) for _, ks, _, _, rhs in jobs]
        mp = []
        for s, (_, _, t, kv, _) in zip(s_all, jobs):
            s = s + bias_ref[t]
            m = jnp.maximum(jnp.max(s, axis=0, keepdims=True), sink[kv])
            mp.append((m, jnp.exp2((s - m).astype(BF16))))
        pv_all = [_dot(vt_ref[kv, :, pl.ds(ks, 2 * w)], p) for (_, p), (_, ks, _, kv, _) in zip(mp, jobs)]
        for pv, (m, _), (qs, _, _, kv, _) in zip(pv_all, mp, jobs):
            den = pv[hd:hd + 1] + jnp.exp2(sink[kv] - m)
            o = pv[:hd] * (1.0 / den)
            o_t = jnp.concatenate([o[:, :w], o[:, w:]], axis=0)
            o_ref[0, pl.ds(qs, w), kv * LANES:(kv + 1) * LANES] = o_t.T.astype(o_ref.dtype)
        return 0

    lax.fori_loop(0, nb // SWA_BLOCKS_PER_STEP, body, 0)


def _swa(qkv, sinks):
    b, s, _ = qkv.shape
    kern = functools.partial(_swa_kernel, seq=s)
    return pl.pallas_call(
        kern,
        grid=(b,),
        in_specs=[
            pl.BlockSpec(memory_space=pltpu.SMEM),
            pl.BlockSpec((1, s, 2 * LANES), lambda i: (i, 0, BLK_A_Q // 2)),
            pl.BlockSpec((1, s, LANES), lambda i: (i, 0, BLK_A_K)),
            pl.BlockSpec((1, s, LANES), lambda i: (i, 0, BLK_A_V)),
        ],
        out_specs=pl.BlockSpec((1, s, 2 * LANES), lambda i: (i, 0, 0)),
        out_shape=jax.ShapeDtypeStruct((b, s, 2 * LANES), BF16),
        scratch_shapes=[
            pltpu.VMEM((2 * LANES, s), BF16),
            pltpu.VMEM((2, HEAD_DIM + 16, s), BF16),
            pltpu.VMEM((2, 2 * SWA_BLOCK, 2 * SWA_BLOCK), F32),
        ],
        compiler_params=pltpu.CompilerParams(
            dimension_semantics=("arbitrary",), vmem_limit_bytes=VMEM_LIMIT),
        name="swa",
    )(sinks, qkv, qkv, qkv)


def _moba_kernel(q_ref, k_ref, v_ref, q32_ref, kmean_ref, o_ref, ka_ref, vt_ref, qt_ref, *, seq):
    lb = MOBA_BLOCK
    nb = seq // lb

    n_pairs = q_ref.shape[-1] // LANES
    n_heads = 2 * n_pairs
    hd = HEAD_DIM

    def prep(c, _):
        rows = pl.ds(pl.multiple_of(c * lb, lb), lb)
        lane = _lane_iota((lb, LANES))
        ones = jnp.ones((16, lb), BF16)
        for pr in range(n_pairs):
            cols = slice(pr * LANES, (pr + 1) * LANES)
            kt = k_ref[0, rows, cols].astype(F32)
            ka_ref[2 * pr, rows, :] = jnp.where(lane < hd, kt, jnp.where(lane - hd == c, 1.0, 0.0)).astype(BF16)
            ka_ref[2 * pr + 1, rows, :] = jnp.where(lane >= hd, kt, jnp.where(lane == c, 1.0, 0.0)).astype(BF16)
            vt = v_ref[0, rows, cols].astype(F32).T.astype(BF16)
            vt_ref[2 * pr, :, rows] = jnp.concatenate([vt[:hd], ones], axis=0)
            vt_ref[2 * pr + 1, :, rows] = jnp.concatenate([vt[hd:], ones], axis=0)
            qt_ref[cols, rows] = q_ref[0, rows, cols].astype(F32).T.astype(BF16)
        return 0

    lax.fori_loop(0, nb, prep, 0)

    lane_lo_k = _lane_iota((nb, LANES)) < hd
    kmh = []
    for pr in range(n_pairs):
        km = kmean_ref[0, :, pr * LANES:(pr + 1) * LANES]
        kmh += [jnp.where(lane_lo_k, km, 0.0), jnp.where(lane_lo_k, 0.0, km)]
    blk_iota = _row_iota((nb, lb))
    causal_t = _row_iota((lb, lb)) <= _lane_iota((lb, lb))
    pad = jnp.zeros((hd - nb, lb), BF16)
    halves = range(MOBA_BLOCKS_PER_TILE)

    def tile(j, _):
        first = j * MOBA_BLOCKS_PER_TILE
        qs = [pl.multiple_of((first + u) * lb, lb) for u in halves]

        qa = {}
        for u in halves:
            valid = blk_iota < first + u
            bias_t = []
            for h in range(n_heads):
                q32 = q32_ref[0, pl.ds(qs[u], lb), (h // 2) * LANES:(h // 2 + 1) * LANES]
                gate = _dot_nt(kmh[h], q32, precision=lax.Precision.HIGHEST)
                gm = jnp.where(valid, gate, -jnp.inf)
                rank = jnp.zeros((nb, lb), F32)
                for m_idx in range(nb):
                    row = gm[m_idx:m_idx + 1, :]
                    tie = jnp.where(blk_iota > m_idx, 1.0, 0.0)
                    rank = rank + jnp.where(row > gm, 1.0, jnp.where(row == gm, tie, 0.0))
                routed = jnp.where(rank < MOBA_TOPK, 0.0, NEG)
                bias_t.append(jnp.where(valid, routed, 0.0).astype(BF16))
            for pr in range(n_pairs):
                qt = qt_ref[pr * LANES:(pr + 1) * LANES, pl.ds(qs[u], lb)]
                qa[(2 * pr, u)] = jnp.concatenate([qt[:hd], bias_t[2 * pr], pad], axis=0)
                qa[(2 * pr + 1, u)] = jnp.concatenate([bias_t[2 * pr + 1], pad, qt[hd:]], axis=0)
        chains = [(h, u) for h in range(n_heads) for u in halves]

        def local_many(jobs):
            out = [None] * len(jobs)
            s_all = [None] * len(jobs)
            for idx in range(len(jobs) + MOBA_PV_LAG):
                if idx < len(jobs):
                    h, u, ks, _ = jobs[idx]
                    s_all[idx] = _dot(ka_ref[h, pl.ds(ks, lb), :], qa[(h, u)])
                k = idx - MOBA_PV_LAG
                if k >= 0:
                    h, _, ks, masked = jobs[k]
                    s = jnp.where(causal_t, s_all[k], NEG) if masked else s_all[k]
                    m = jnp.max(s, axis=0, keepdims=True)
                    p = jnp.exp2((s - m).astype(BF16))
                    out[k] = (m, _dot(vt_ref[h, :, pl.ds(ks, lb)], p))
            return out

        def merge(parts):
            m_new = parts[0][0]
            for m, _ in parts[1:]:
                m_new = jnp.maximum(m_new, m)
            acc = None
            for m, pv in parts:
                term = pv * jnp.exp2(m - m_new)
                acc = term if acc is None else acc + term
            return m_new, acc

        jobs = [(h, u, qs[v], v == u) for h, u in chains for v in range(u + 1)]
        parts = local_many(jobs)
        state = []
        for h, u in chains:
            state += merge([part for part, job in zip(parts, jobs) if job[:2] == (h, u)])

        def past(n, st):
            ks = [pl.multiple_of((n * MOBA_PAST_PER_STEP + v) * lb, lb) for v in range(MOBA_PAST_PER_STEP)]
            parts = local_many([(h, u, ks[v], False) for v in range(MOBA_PAST_PER_STEP) for h, u in chains])
            out = []
            for c in range(len(chains)):
                out += merge([(st[2 * c], st[2 * c + 1])] + parts[c::len(chains)])
            return tuple(out)

        st = lax.fori_loop(0, first // MOBA_PAST_PER_STEP, past, tuple(state))
        for u in halves:
            accs = [st[2 * chains.index((h, u)) + 1] for h in range(n_heads)]
            o_t = jnp.concatenate([a[:hd] * (1.0 / a[hd:hd + 1]) for a in accs], axis=0)
            o_ref[0, pl.ds(qs[u], lb), :] = o_t.T.astype(o_ref.dtype)
        return 0

    lax.fori_loop(0, nb // MOBA_BLOCKS_PER_TILE, tile, 0)


def _moba(qkv, bq32, kmean):
    b, s, _ = qkv.shape
    nb = s // MOBA_BLOCK
    kern = functools.partial(_moba_kernel, seq=s)
    return pl.pallas_call(
        kern,
        grid=(b,),
        in_specs=[
            pl.BlockSpec((1, s, 2 * LANES), lambda i: (i, 0, BLK_B_Q // 2)),
            pl.BlockSpec((1, s, 2 * LANES), lambda i: (i, 0, BLK_B_K // 2)),
            pl.BlockSpec((1, s, 2 * LANES), lambda i: (i, 0, BLK_B_V // 2)),
            pl.BlockSpec((1, s, 2 * LANES), lambda i: (i, 0, 0)),
            pl.BlockSpec((1, nb, 2 * LANES), lambda i: (i, 0, 0)),
        ],
        out_specs=pl.BlockSpec((1, s, 2 * LANES), lambda i: (i, 0, 0)),
        out_shape=jax.ShapeDtypeStruct((b, s, 2 * LANES), BF16),
        scratch_shapes=[
            pltpu.VMEM((4, s, LANES), BF16),
            pltpu.VMEM((4, HEAD_DIM + 16, s), BF16),
            pltpu.VMEM((2 * LANES, s), BF16),
        ],
        compiler_params=pltpu.CompilerParams(
            dimension_semantics=("arbitrary",), vmem_limit_bytes=VMEM_LIMIT),
        name="moba",
    )(qkv, qkv, qkv, bq32, kmean)


def _sb_kernel(q_ref, k_ref, v_ref, o_ref, tri_ref, carry_ref, acc_ref, *, seq):
    w = SB_BLOCK
    nb = seq // w
    win = SB_WINDOW_BLKS * w
    n_rows = q_ref.shape[0]
    n_pairs = q_ref.shape[-1] // LANES
    n_heads = 2 * n_pairs
    chains = [(t, b, h) for t in range(SB_STREAMS) for b in range(n_rows) for h in range(n_heads)]

    tri_r = _row_iota((2 * w, 2 * w)) % w
    tri_c = _lane_iota((2 * w, 2 * w))
    tri_ref[...] = jnp.where((tri_c >= w) | (tri_r >= tri_c), 1.0, 0.0).astype(BF16)
    lane_lo = _lane_iota((w, LANES)) < HEAD_DIM
    row = _row_iota((w, win))
    kcol = _lane_iota((w, win))

    def tile(i, _):
        tiles = [i + t * (nb // SB_STREAMS) for t in range(SB_STREAMS)]
        qs = [pl.multiple_of(ti * w, w) for ti in tiles]
        qpos = [q0 + row for q0 in qs]
        carry_ref[...] = jnp.zeros_like(carry_ref)
        acc_ref[...] = jnp.zeros_like(acc_ref)
        qm = []
        for t, b, h in chains:
            q2 = q_ref[b, pl.ds(qs[t], w), (h // 2) * LANES:(h // 2 + 1) * LANES].astype(F32)
            qm.append((jnp.where(lane_lo, q2, 0.0) if h % 2 == 0 else jnp.where(lane_lo, 0.0, q2)).astype(BF16))

        def cond(st):
            return st[0] > 0

        def body(st):
            tops = st[1:]
            bottoms = [jnp.maximum(top - SB_WINDOW_BLKS, 0) for top in tops]
            ws = [pl.multiple_of(bottom * w, w) for bottom in bottoms]
            bias = [jnp.where((ws[t] + kcol) < jnp.minimum(qpos[t], tops[t] * w), 0.0, NEG)
                    for t in range(SB_STREAMS)]
            tri = tri_ref[...]
            zm = [_dot_nt(qm[ch], k_ref[b, pl.ds(ws[t], win), (h // 2) * LANES:(h // 2 + 1) * LANES]) + bias[t]
                  for ch, (t, b, h) in enumerate(chains)]
            split = []
            for ch in range(len(chains)):
                neg_log_fail = jnp.maximum(zm[ch], jnp.log2(1.0 + jnp.exp2(jnp.minimum(zm[ch], SB_POW_CLAMP))))
                for c in range(SB_WINDOW_BLKS):
                    nlf = neg_log_fail[:, c * w:(c + 1) * w]
                    hi = pltpu.bitcast(pltpu.bitcast(nlf, jnp.uint32) & jnp.uint32(0xFFFF0000), F32)
                    split.append(jnp.concatenate([hi.astype(BF16), (nlf - hi).astype(BF16)], axis=1))
            run = [carry_ref[ch] for ch in range(len(chains))]
            expo = {}
            for c in reversed(range(SB_WINDOW_BLKS)):
                for ch in range(len(chains)):
                    rc = _dot(split[ch * SB_WINDOW_BLKS + c], tri) + jnp.concatenate([run[ch], run[ch]], axis=1)
                    expo[(ch, c)] = rc[:, :w]
                    run[ch] = rc[:, w:]
            a_all = []
            for ch in range(len(chains)):
                carry_ref[ch] = run[ch]
                a = [jnp.exp2(zm[ch][:, c * w:(c + 1) * w] - expo[(ch, c)]) for c in range(SB_WINDOW_BLKS)]
                a_all.append(jnp.concatenate(a, axis=1).astype(BF16))
            low = None
            for ch, (t, b, h) in enumerate(chains):
                vv = v_ref[b, pl.ds(ws[t], win), (h // 2) * LANES:(h // 2 + 1) * LANES]
                acc_ref[ch] = acc_ref[ch] + _dot(a_all[ch], vv)
                low_ch = jnp.where(bottoms[t] > 0, jnp.min(carry_ref[ch]), -SB_EXIT_LOG2)
                low = low_ch if low is None else jnp.minimum(low, low_ch)
            return (jnp.where(low < -SB_EXIT_LOG2, 1, 0), *bottoms)

        lax.while_loop(cond, body, (jnp.int32(1), *[ti + 1 for ti in tiles]))
        for ch, (t, b, h) in enumerate(chains):
            if h % 2 == 0:
                o_ref[b, pl.ds(qs[t], w), (h // 2) * LANES:(h // 2 + 1) * LANES] = jnp.where(
                    lane_lo, acc_ref[ch], acc_ref[ch + 1]).astype(o_ref.dtype)
        return 0

    lax.fori_loop(0, nb // SB_STREAMS, tile, 0)


def _sb(qkv):
    b, s, _ = qkv.shape
    kern = functools.partial(_sb_kernel, seq=s)
    return pl.pallas_call(
        kern,
        grid=(b // SB_ROWS_PER_STEP,),
        in_specs=[
            pl.BlockSpec((SB_ROWS_PER_STEP, s, 2 * LANES), lambda i: (i, 0, BLK_C_Q // 2)),
            pl.BlockSpec((SB_ROWS_PER_STEP, s, 2 * LANES), lambda i: (i, 0, BLK_C_K // 2)),
            pl.BlockSpec((SB_ROWS_PER_STEP, s, 2 * LANES), lambda i: (i, 0, BLK_C_V // 2)),
        ],
        out_specs=pl.BlockSpec((SB_ROWS_PER_STEP, s, 2 * LANES), lambda i: (i, 0, 0)),
        out_shape=jax.ShapeDtypeStruct((b, s, 2 * LANES), BF16),
        scratch_shapes=[
            pltpu.VMEM((2 * SB_BLOCK, 2 * SB_BLOCK), BF16),
            pltpu.VMEM((4 * SB_ROWS_PER_STEP * SB_STREAMS, SB_BLOCK, SB_BLOCK), F32),
            pltpu.VMEM((4 * SB_ROWS_PER_STEP * SB_STREAMS, SB_BLOCK, LANES), F32),
        ],
        compiler_params=pltpu.CompilerParams(
            dimension_semantics=("arbitrary",), vmem_limit_bytes=VMEM_LIMIT),
        name="stick_breaking",
    )(qkv, qkv, qkv)


def _ret_kernel(q_ref, k_ref, v_ref, g_ref, decay_ref, qw_ref, kw_ref, gc_ref, o_ref, state_ref, avg_ref, *, seq):
    c = RET_CHUNK
    n = seq // c
    lane_lo = _lane_iota((c, LANES)) < HEAD_DIM
    state_ref[...] = jnp.zeros_like(state_ref)

    def split_heads(t):
        return jnp.concatenate([jnp.where(lane_lo, t, 0.0), jnp.where(lane_lo, 0.0, t)], axis=0).astype(BF16)

    avg_r = (_row_iota((2 * LANES, LANES)) % LANES) // HEAD_DIM
    avg_c = _lane_iota((2 * LANES, LANES)) // HEAD_DIM
    avg_ref[...] = jnp.where(avg_r == avg_c, 1.0 / HEAD_DIM, 0.0).astype(BF16)

    def half_stat(t):
        hi = pltpu.bitcast(pltpu.bitcast(t, jnp.uint32) & jnp.uint32(0xFFFF0000), F32)
        return _dot(jnp.concatenate([hi.astype(BF16), (t - hi).astype(BF16)], axis=1), avg_ref[...])

    def step(it, _):
        rows = [pl.ds(pl.multiple_of((it * RET_CHUNKS_PER_STEP + u) * c, c), c) for u in range(RET_CHUNKS_PER_STEP)]
        cols = [slice(p * LANES, (p + 1) * LANES) for p in range(2)]
        q2, v2, scores, kv = {}, {}, {}, {}
        st = [state_ref[p] for p in range(2)]
        jobs = [(u, p) for u in range(RET_CHUNKS_PER_STEP) for p in range(2)]
        for u, p in jobs:
            j = (u, p)
            q2[j] = q_ref[0, rows[u], cols[p]]
            v2[j] = v_ref[0, rows[u], cols[p]]
            k2f = k_ref[0, rows[u], cols[p]].astype(F32)
            scores[j] = _dot_nt(q2[j], split_heads(k2f))
            kv[j] = _dot((k2f * kw_ref[p]).T.astype(BF16), v2[j])
        y = {}
        for u, p in jobs:
            j = (u, p)
            inner = _dot((scores[j] * decay_ref[p]).astype(BF16), split_heads(v2[j].astype(F32)))
            cross = _dot(q2[j], st[p].astype(BF16))
            gcm = gc_ref[p]
            st[p] = st[p] * gcm + jnp.where(gcm > 0.0, kv[j], 0.0)
            y[j] = inner + cross * qw_ref[p]
        mu = {j: half_stat(y[j]) for j in jobs}
        yc = {j: y[j] - mu[j] for j in jobs}
        var = {j: half_stat(yc[j] * yc[j]) for j in jobs}
        for u, p in jobs:
            yn = yc[(u, p)] * lax.rsqrt(var[(u, p)] + LN_EPS)
            g = g_ref[0, rows[u], cols[p]]
            gate = g * (1.0 / (1.0 + jnp.exp(-g)))
            o_ref[0, rows[u], cols[p]] = (yn * gate).astype(o_ref.dtype)
        for p in range(2):
            state_ref[p] = st[p]
        return 0

    lax.fori_loop(0, n // RET_CHUNKS_PER_STEP, step, 0)


def _ret(qkv, g, decay, qw, kw, gcm):
    b, s, _ = qkv.shape
    kern = functools.partial(_ret_kernel, seq=s)
    full3 = lambda a: pl.BlockSpec(a.shape, lambda i: (0, 0, 0))
    return pl.pallas_call(
        kern,
        grid=(b,),
        in_specs=[
            pl.BlockSpec((1, s, 2 * LANES), lambda i: (i, 0, BLK_D_Q // 2)),
            pl.BlockSpec((1, s, 2 * LANES), lambda i: (i, 0, BLK_D_K // 2)),
            pl.BlockSpec((1, s, 2 * LANES), lambda i: (i, 0, BLK_D_V // 2)),
            pl.BlockSpec((1, s, 2 * LANES), lambda i: (i, 0, 0)),
            full3(decay), full3(qw), full3(kw), full3(gcm),
        ],
        out_specs=pl.BlockSpec((1, s, 2 * LANES), lambda i: (i, 0, 0)),
        out_shape=jax.ShapeDtypeStruct((b, s, 2 * LANES), BF16),
        scratch_shapes=[pltpu.VMEM((2, LANES, LANES), F32), pltpu.VMEM((2 * LANES, LANES), BF16)],
        compiler_params=pltpu.CompilerParams(
            dimension_semantics=("arbitrary",), vmem_limit_bytes=VMEM_LIMIT),
        name="retention",
    )(qkv, qkv, qkv, g, decay, qw, kw, gcm)


def _layer_norm(t, w, b):
    mu = jnp.mean(t, axis=-1, keepdims=True)
    tc = t - mu
    var = jnp.mean(tc * tc, axis=-1, keepdims=True)
    return tc * lax.rsqrt(var + LN_EPS) * w + b


def _out_mlp_kernel(ya_ref, yb_ref, yc_ref, yd_ref, x_ref, wo_ref, l1w_ref, l1b_ref,
                    wu_ref, wd_ref, l2w_ref, l2b_ref, o_ref):
    width = ya_ref.shape[-1]
    ff = wu_ref.shape[1]
    step = 1024
    n_sub = x_ref.shape[1] // OUT_SUBTILE
    rows = [slice(i * OUT_SUBTILE, (i + 1) * OUT_SUBTILE) for i in range(n_sub)]

    def out_proj(r):
        h = None
        for idx, y_ref in enumerate((ya_ref, yb_ref, yc_ref, yd_ref)):
            part = _dot(y_ref[0, r, :], wo_ref[idx * width:(idx + 1) * width, :])
            h = part if h is None else h + part
        return h

    def mlp(x1b):
        acc = None
        for c in range(ff // step):
            hid = _dot(x1b, wu_ref[:, c * step:(c + 1) * step])
            hid = jnp.square(jnp.maximum(hid, 0.0)).astype(BF16)
            part = _dot(hid, wd_ref[c * step:(c + 1) * step, :])
            acc = part if acc is None else acc + part
        return acc

    h = [out_proj(r) for r in rows]
    x1 = [None] * n_sub
    acc = [None] * n_sub
    for i in range(n_sub):
        x1[i] = _layer_norm(DEEPNORM_ALPHA * x_ref[0, rows[i], :] + h[i], l1w_ref[...], l1b_ref[...])
        acc[i] = mlp(x1[i].astype(BF16))
        if i > 0:
            o_ref[0, rows[i - 1], :] = _layer_norm(DEEPNORM_ALPHA * x1[i - 1] + acc[i - 1], l2w_ref[...], l2b_ref[...])
    o_ref[0, rows[-1], :] = _layer_norm(DEEPNORM_ALPHA * x1[-1] + acc[-1], l2w_ref[...], l2b_ref[...])


def _out_mlp(ya, yb, yc, yd, x, wo, l1w, l1b, wu, wd, l2w, l2b, *, tm):
    b, s, d = x.shape
    nt = s // tm
    ytile = pl.BlockSpec((1, tm, ya.shape[-1]), lambda i, t: (i, t, 0))
    const = lambda a: pl.BlockSpec(a.shape, lambda i, t: (0, 0), pipeline_mode=pl.Buffered(1))
    return pl.pallas_call(
        _out_mlp_kernel,
        grid=(b, nt),
        in_specs=[ytile, ytile, ytile, ytile,
                  pl.BlockSpec((1, tm, d), lambda i, t: (i, t, 0)),
                  const(wo), const(l1w), const(l1b), const(wu), const(wd), const(l2w), const(l2b)],
        out_specs=pl.BlockSpec((1, tm, d), lambda i, t: (i, t, 0)),
        out_shape=jax.ShapeDtypeStruct((b, s, d), F32),
        compiler_params=pltpu.CompilerParams(
            dimension_semantics=("arbitrary", "arbitrary"), vmem_limit_bytes=VMEM_LIMIT),
        name="out_mlp",
    )(ya, yb, yc, yd, x, wo, l1w, l1b, wu, wd, l2w, l2b)


def _rope_tables(s):
    half = HEAD_DIM // 2
    inv = 1.0 / (ROPE_THETA ** (jnp.arange(0, HEAD_DIM, 2, dtype=F32) / HEAD_DIM))
    ang = jnp.arange(s, dtype=F32)[:, None] * inv[None, :]
    cos = jnp.cos(ang)
    sin = jnp.sin(ang)
    cos_t = jnp.tile(cos, (1, LANES // half))
    sin_t = jnp.tile(jnp.concatenate([-sin, sin], axis=1), (1, LANES // HEAD_DIM))
    return cos_t, sin_t


def _retention_tables():
    c = RET_CHUNK
    h = RET_HEADS
    log_g = jnp.log(1.0 - 2.0 ** (-5.0 - jnp.arange(h, dtype=F32)))
    idx = jnp.arange(c, dtype=F32)
    diff = idx[:, None] - idx[None, :]
    decay = jnp.where(diff >= 0, jnp.exp(jnp.maximum(diff, 0.0)[None] * log_g[:, None, None]), 0.0)
    k_w = jnp.exp((c - 1.0 - idx)[:, None] * log_g[None, :])
    q_w = jnp.exp((idx + 1.0)[:, None] * log_g[None, :])
    g_c = jnp.exp(c * log_g)
    per_lane = lambda t: jnp.repeat(t, HEAD_DIM, axis=1).reshape(c, h // 2, LANES).transpose(1, 0, 2)
    head_of = jnp.arange(LANES) // HEAD_DIM
    same = head_of[:, None] == head_of[None, :]
    gcm = jnp.stack([jnp.where(same, g_c[2 * p + head_of][None, :], 0.0) for p in range(h // 2)])
    decay_pairs = jnp.concatenate([decay[0::2], decay[1::2]], axis=2)
    return decay_pairs, per_lane(q_w), per_lane(k_w), gcm


def kernel(x, w_in, swa_sinks, w_out, ln1_w, ln1_b, w_up, w_down, ln2_w, ln2_b):
    b, s, d = x.shape
    cos_t, sin_t = _rope_tables(s)
    decay, qw, kw, gcm = _retention_tables()
    row = lambda t: t.reshape(1, -1)
    tm_in = 1024
    tm_out = 2 * OUT_SUBTILE
    for layer in range(w_in.shape[0]):
        qkv, bq32, kmean8, g = _in_proj(x, w_in[layer].astype(BF16), cos_t, sin_t, tm=tm_in)
        nblk = tm_in // MOBA_BLOCK
        kmean = kmean8.reshape(b, s // tm_in, 8, 2 * LANES)[:, :, :nblk].reshape(b, s // MOBA_BLOCK, 2 * LANES)
        ya = _swa(qkv, swa_sinks[layer])
        yb = _moba(qkv, bq32, kmean)
        yc = _sb(qkv)
        yd = _ret(qkv, g, decay, qw, kw, gcm)
        x = _out_mlp(ya, yb, yc, yd, x, w_out[layer].astype(BF16), row(ln1_w[layer]), row(ln1_b[layer]),
                     w_up[layer].astype(BF16), w_down[layer].astype(BF16), row(ln2_w[layer]), row(ln2_b[layer]),
                     tm=tm_out)
    return x
```

```python
import functools
import math

import jax
import jax.numpy as jnp
from jax import lax
from jax.experimental import pallas as pl
from jax.experimental.pallas import tpu as pltpu

F32 = jnp.float32
BF16 = jnp.bfloat16

D_MODEL = 1024
HEAD_DIM = 64
LANES = 128
D_FF = 4 * D_MODEL
DEPTH = 2
ROPE_THETA = 10000.0
LN_EPS = 1e-5
NEG = -1e30
DEEPNORM_ALPHA = (2 * DEPTH) ** 0.25
QK_SCALE = HEAD_DIM ** -0.5

SWA_BLOCK = 128
SWA_WINDOW = 128
SWA_BLOCKS_PER_STEP = 4
MOBA_BLOCK = 256
MOBA_BLOCKS_PER_TILE = 4
MOBA_TOPK = 3
MOBA_ROUTE_WIDTH = 1024
MOBA_PAST_PER_STEP = 4
MOBA_PV_LAG = 6
SB_BLOCK = 128
SB_WINDOW_BLKS = 3
SB_ROWS_PER_STEP = 2
SB_STREAMS = 2
SB_POW_CLAMP = 126.0
RET_CHUNK = 128
RET_HEADS = 4
OUT_SUBTILE = 256
RET_CHUNKS_PER_STEP = 8

BLK_A_Q, BLK_A_K, BLK_A_V = 0, 2, 3
BLK_B_Q, BLK_B_K, BLK_B_V = 4, 6, 8
BLK_C_Q, BLK_C_K, BLK_C_V = 10, 12, 14
BLK_D_Q, BLK_D_K, BLK_D_V, BLK_D_G = 16, 18, 20, 22
N_QKV_BLKS = 22
ROPED_BLKS = frozenset([0, 1, 2, 4, 5, 6, 7, 16, 17, 18, 19])
LOG2E = math.log2(math.e)
BLK_SCALE = {blk: QK_SCALE * LOG2E for blk in (0, 1, 4, 5, 10, 11)}
BLK_SCALE.update({18: QK_SCALE, 19: QK_SCALE})

SB_EXIT_LOG2 = -110.0 * LOG2E

VMEM_LIMIT = 56 * 1024 * 1024


def _dot(a, b, **kw):
    return jnp.dot(a, b, preferred_element_type=F32, **kw)


def _dot_nt(a, b, **kw):
    return lax.dot_general(a, b, (((1,), (1,)), ((), ())), preferred_element_type=F32, **kw)


def _lane_iota(shape):
    return lax.broadcasted_iota(jnp.int32, shape, len(shape) - 1)


def _row_iota(shape):
    return lax.broadcasted_iota(jnp.int32, shape, len(shape) - 2)


def _in_proj_kernel(x_ref, w_ref, cos_ref, sin_ref, qkv_ref, bq_ref, kmean_ref, g_ref, *, tm):
    xb = x_ref[0].astype(BF16)
    cos = cos_ref[...]
    sin = sin_ref[...]
    first_half = (_lane_iota((tm, LANES)) % HEAD_DIM) < (HEAD_DIM // 2)
    n_pairs = w_ref.shape[1] // (2 * LANES)
    for c in range(n_pairs):
        acc2 = _dot(xb, w_ref[:, c * 2 * LANES:(c + 1) * 2 * LANES])
        for half in range(2):
            blk = 2 * c + half
            acc = acc2[:, half * LANES:(half + 1) * LANES]
            if blk in ROPED_BLKS:
                partner = jnp.where(first_half,
                                    pltpu.roll(acc, LANES - HEAD_DIM // 2, 1),
                                    pltpu.roll(acc, HEAD_DIM // 2, 1))
                acc = acc * cos + partner * sin
            if blk in (BLK_B_Q, BLK_B_Q + 1):
                o = (blk - BLK_B_Q) * LANES
                bq_ref[0, :, o:o + LANES] = acc
            if blk in (BLK_B_K, BLK_B_K + 1):
                o = (blk - BLK_B_K) * LANES
                nblk = tm // MOBA_BLOCK
                means = [jnp.sum(acc[i * MOBA_BLOCK:(i + 1) * MOBA_BLOCK], axis=0, keepdims=True)
                         * (1.0 / MOBA_BLOCK) for i in range(nblk)]
                means.append(jnp.zeros((8 - nblk, LANES), F32))
                kmean_ref[0, :, o:o + LANES] = jnp.concatenate(means, axis=0)
            if blk in (BLK_D_G, BLK_D_G + 1):
                o = (blk - BLK_D_G) * LANES
                g_ref[0, :, o:o + LANES] = acc
            if blk < N_QKV_BLKS:
                if blk in BLK_SCALE:
                    acc = acc * BLK_SCALE[blk]
                qkv_ref[0, :, blk * LANES:(blk + 1) * LANES] = acc.astype(BF16)


def _in_proj(x, w_in_bf16, cos_t, sin_t, *, tm):
    b, s, d = x.shape
    nt = s // tm
    n_in = w_in_bf16.shape[1]
    kern = functools.partial(_in_proj_kernel, tm=tm)
    return pl.pallas_call(
        kern,
        grid=(b, nt),
        in_specs=[
            pl.BlockSpec((1, tm, d), lambda i, t: (i, t, 0)),
            pl.BlockSpec((d, n_in), lambda i, t: (0, 0), pipeline_mode=pl.Buffered(1)),
            pl.BlockSpec((tm, LANES), lambda i, t: (t, 0)),
            pl.BlockSpec((tm, LANES), lambda i, t: (t, 0)),
        ],
        out_specs=[
            pl.BlockSpec((1, tm, N_QKV_BLKS * LANES), lambda i, t: (i, t, 0)),
            pl.BlockSpec((1, tm, 2 * LANES), lambda i, t: (i, t, 0)),
            pl.BlockSpec((1, 8, 2 * LANES), lambda i, t: (i * nt + t, 0, 0)),
            pl.BlockSpec((1, tm, 2 * LANES), lambda i, t: (i, t, 0)),
        ],
        out_shape=[
            jax.ShapeDtypeStruct((b, s, N_QKV_BLKS * LANES), BF16),
            jax.ShapeDtypeStruct((b, s, 2 * LANES), F32),
            jax.ShapeDtypeStruct((b * nt, 8, 2 * LANES), F32),
            jax.ShapeDtypeStruct((b, s, 2 * LANES), F32),
        ],
        compiler_params=pltpu.CompilerParams(
            dimension_semantics=("arbitrary", "arbitrary"), vmem_limit_bytes=VMEM_LIMIT),
        name="in_proj",
    )(x, w_in_bf16, cos_t, sin_t)


def _swa_kernel(sinks_ref, q_ref, k_ref, v_ref, o_ref, qt_ref, vt_ref, bias_ref, *, seq):
    w = SWA_BLOCK
    nb = seq // w
    hd = HEAD_DIM
    n_kv = k_ref.shape[-1] // hd
    chunk = 2 * w

    def prep(c, _):
        rows = pl.ds(pl.multiple_of(c * chunk, chunk), chunk)
        qt_ref[:, rows] = q_ref[0, rows, :].astype(F32).T.astype(BF16)
        vt = v_ref[0, rows, :].astype(F32).T.astype(BF16)
        ones = jnp.ones((16, chunk), BF16)
        for kv in range(n_kv):
            vt_ref[kv, :, rows] = jnp.concatenate([vt[kv * hd:(kv + 1) * hd], ones], axis=0)
        return 0

    lax.fori_loop(0, seq // chunk, prep, 0)

    delta = (_lane_iota((2 * w, 2 * w)) % w) - _row_iota((2 * w, 2 * w))
    for t in range(2):
        dist = delta + t * w
        bias_ref[t] = jnp.where((dist >= 0) & (dist < SWA_WINDOW), 0.0, NEG)
    first_head = _lane_iota((1, 2 * w)) < w
    sink = [jnp.where(first_head, sinks_ref[2 * kv], sinks_ref[2 * kv + 1]) * LOG2E
            for kv in range(n_kv)]
    zeros = jnp.zeros((hd, 2 * w), BF16)

    def body(it, _):
        jobs = []
        for u in range(SWA_BLOCKS_PER_STEP):
            i = it * SWA_BLOCKS_PER_STEP + u
            qs = pl.multiple_of(i * w, w)
            ks = pl.multiple_of(jnp.maximum(i - 1, 0) * w, w)
            for kv in range(n_kv):
                qq = jnp.concatenate([qt_ref[(2 * kv) * hd:(2 * kv + 1) * hd, pl.ds(qs, w)],
                                      qt_ref[(2 * kv + 1) * hd:(2 * kv + 2) * hd, pl.ds(qs, w)]], axis=1)
                rhs = jnp.concatenate([qq, zeros] if kv == 0 else [zeros, qq], axis=0)
                jobs.append((qs, ks, jnp.minimum(i, 1), kv, rhs))
        s_all = [_dot(k_ref[0, pl.ds(ks, 2 * w), :], rhs) for _, ks, _, _, rhs in jobs]
        mp = []
        for s, (_, _, t, kv, _) in zip(s_all, jobs):
            s = s + bias_ref[t]
            m = jnp.maximum(jnp.max(s, axis=0, keepdims=True), sink[kv])
            mp.append((m, jnp.exp2((s - m).astype(BF16))))
        pv_all = [_dot(vt_ref[kv, :, pl.ds(ks, 2 * w)], p) for (_, p), (_, ks, _, kv, _) in zip(mp, jobs)]
        for pv, (m, _), (qs, _, _, kv, _) in zip(pv_all, mp, jobs):
            den = pv[hd:hd + 1] + jnp.exp2(sink[kv] - m)
            o = pv[:hd] * (1.0 / den)
            o_t = jnp.concatenate([o[:, :w], o[:, w:]], axis=0)
            o_ref[0, pl.ds(qs, w), kv * LANES:(kv + 1) * LANES] = o_t.T.astype(o_ref.dtype)
        return 0

    lax.fori_loop(0, nb // SWA_BLOCKS_PER_STEP, body, 0)


def _swa(qkv, sinks):
    b, s, _ = qkv.shape
    kern = functools.partial(_swa_kernel, seq=s)
    return pl.pallas_call(
        kern,
        grid=(b,),
        in_specs=[
            pl.BlockSpec(memory_space=pltpu.SMEM),
            pl.BlockSpec((1, s, 2 * LANES), lambda i: (i, 0, BLK_A_Q // 2)),
            pl.BlockSpec((1, s, LANES), lambda i: (i, 0, BLK_A_K)),
            pl.BlockSpec((1, s, LANES), lambda i: (i, 0, BLK_A_V)),
        ],
        out_specs=pl.BlockSpec((1, s, 2 * LANES), lambda i: (i, 0, 0)),
        out_shape=jax.ShapeDtypeStruct((b, s, 2 * LANES), BF16),
        scratch_shapes=[
            pltpu.VMEM((2 * LANES, s), BF16),
            pltpu.VMEM((2, HEAD_DIM + 16, s), BF16),
            pltpu.VMEM((2, 2 * SWA_BLOCK, 2 * SWA_BLOCK), F32),
        ],
        compiler_params=pltpu.CompilerParams(
            dimension_semantics=("arbitrary",), vmem_limit_bytes=VMEM_LIMIT),
        name="swa",
    )(sinks, qkv, qkv, qkv)


def _moba_kernel(q_ref, k_ref, v_ref, q32_ref, kmean_ref, o_ref, ka_ref, vt_ref, qt_ref, bias_ref, *, seq):
    lb = MOBA_BLOCK
    nb = seq // lb

    n_pairs = q_ref.shape[-1] // LANES
    n_heads = 2 * n_pairs
    hd = HEAD_DIM

    lane_lo_k = _lane_iota((nb, LANES)) < hd
    kmh = []
    for pr in range(n_pairs):
        km = kmean_ref[0, :, pr * LANES:(pr + 1) * LANES]
        kmh += [jnp.where(lane_lo_k, km, 0.0), jnp.where(lane_lo_k, 0.0, km)]

    def prep(c, _):
        rows = pl.ds(pl.multiple_of(c * lb, lb), lb)
        lane = _lane_iota((lb, LANES))
        ones = jnp.ones((16, lb), BF16)
        for pr in range(n_pairs):
            cols = slice(pr * LANES, (pr + 1) * LANES)
            kt = k_ref[0, rows, cols].astype(F32)
            ka_ref[2 * pr, rows, :] = jnp.where(lane < hd, kt, jnp.where(lane - hd == c, 1.0, 0.0)).astype(BF16)
            ka_ref[2 * pr + 1, rows, :] = jnp.where(lane >= hd, kt, jnp.where(lane == c, 1.0, 0.0)).astype(BF16)
            vt = v_ref[0, rows, cols].astype(F32).T.astype(BF16)
            vt_ref[2 * pr, :, rows] = jnp.concatenate([vt[:hd], ones], axis=0)
            vt_ref[2 * pr + 1, :, rows] = jnp.concatenate([vt[hd:], ones], axis=0)
            qt_ref[cols, rows] = q_ref[0, rows, cols].astype(F32).T.astype(BF16)
        return 0

    lax.fori_loop(0, nb, prep, 0)

    rw = MOBA_ROUTE_WIDTH
    blk_f = _row_iota((nb, rw)).astype(F32)
    own_blk = (_lane_iota((nb, rw)) // lb).astype(F32)
    gates = {}
    for sl in range(seq // rw):
        for pr in range(n_pairs):
            q32 = q32_ref[0, sl * rw:(sl + 1) * rw, pr * LANES:(pr + 1) * LANES]
            both = _dot_nt(jnp.concatenate([kmh[2 * pr], kmh[2 * pr + 1]], axis=0), q32,
                           precision=lax.Precision.HIGHEST)
            gates[(sl, 2 * pr)] = both[:nb]
            gates[(sl, 2 * pr + 1)] = both[nb:]
    for sl in range(seq // rw):
        valid = blk_f < own_blk + float(sl * (rw // lb))
        for h in range(n_heads):
            gm = jnp.where(valid, gates[(sl, h)], -jnp.inf)
            picked = jnp.zeros((nb, rw), F32)
            for _ in range(MOBA_TOPK):
                best = jnp.max(gm, axis=0, keepdims=True)
                first = jnp.min(jnp.where(gm == best, blk_f, float(nb)), axis=0, keepdims=True)
                hit = blk_f == first
                picked = jnp.where(hit, 1.0, picked)
                gm = jnp.where(hit, -jnp.inf, gm)
            routed = jnp.where(picked > 0.0, 0.0, NEG)
            bias_ref[h, :, sl * rw:(sl + 1) * rw] = jnp.where(valid, routed, 0.0).astype(BF16)
    causal_t = _row_iota((lb, lb)) <= _lane_iota((lb, lb))
    pad = jnp.zeros((hd - nb, lb), BF16)
    halves = range(MOBA_BLOCKS_PER_TILE)

    def tile(j, _):
        first = j * MOBA_BLOCKS_PER_TILE
        qs = [pl.multiple_of((first + u) * lb, lb) for u in halves]

        qa = {}
        for u in halves:
            bias_t = [bias_ref[h, :, pl.ds(qs[u], lb)] for h in range(n_heads)]
            for pr in range(n_pairs):
                qt = qt_ref[pr * LANES:(pr + 1) * LANES, pl.ds(qs[u], lb)]
                qa[(2 * pr, u)] = jnp.concatenate([qt[:hd], bias_t[2 * pr], pad], axis=0)
                qa[(2 * pr + 1, u)] = jnp.concatenate([bias_t[2 * pr + 1], pad, qt[hd:]], axis=0)
        chains = [(h, u) for h in range(n_heads) for u in halves]

        def local_many(jobs):
            out = [None] * len(jobs)
            s_all = [None] * len(jobs)
            for idx in range(len(jobs) + MOBA_PV_LAG):
                if idx < len(jobs):
                    h, u, ks, _ = jobs[idx]
                    s_all[idx] = _dot(ka_ref[h, pl.ds(ks, lb), :], qa[(h, u)])
                k = idx - MOBA_PV_LAG
                if k >= 0:
                    h, _, ks, masked = jobs[k]
                    s = jnp.where(causal_t, s_all[k], NEG) if masked else s_all[k]
                    m = jnp.max(s, axis=0, keepdims=True)
                    p = jnp.exp2((s - m).astype(BF16))
                    out[k] = (m, _dot(vt_ref[h, :, pl.ds(ks, lb)], p))
            return out

        def merge(parts):
            m_new = parts[0][0]
            for m, _ in parts[1:]:
                m_new = jnp.maximum(m_new, m)
            acc = None
            for m, pv in parts:
                term = pv * jnp.exp2(m - m_new)
                acc = term if acc is None else acc + term
            return m_new, acc

        jobs = [(h, u, qs[v], v == u) for h, u in chains for v in range(u + 1)]
        parts = local_many(jobs)
        state = []
        for h, u in chains:
            state += merge([part for part, job in zip(parts, jobs) if job[:2] == (h, u)])

        def past(n, st):
            ks = [pl.multiple_of((n * MOBA_PAST_PER_STEP + v) * lb, lb) for v in range(MOBA_PAST_PER_STEP)]
            parts = local_many([(h, u, ks[v], False) for v in range(MOBA_PAST_PER_STEP) for h, u in chains])
            out = []
            for c in range(len(chains)):
                out += merge([(st[2 * c], st[2 * c + 1])] + parts[c::len(chains)])
            return tuple(out)

        st = lax.fori_loop(0, first // MOBA_PAST_PER_STEP, past, tuple(state))
        for u in halves:
            accs = [st[2 * chains.index((h, u)) + 1] for h in range(n_heads)]
            o_t = jnp.concatenate([a[:hd] * (1.0 / a[hd:hd + 1]) for a in accs], axis=0)
            o_ref[0, pl.ds(qs[u], lb), :] = o_t.T.astype(o_ref.dtype)
        return 0

    lax.fori_loop(0, nb // MOBA_BLOCKS_PER_TILE, tile, 0)


def _moba(qkv, bq32, kmean):
    b, s, _ = qkv.shape
    nb = s // MOBA_BLOCK
    kern = functools.partial(_moba_kernel, seq=s)
    return pl.pallas_call(
        kern,
        grid=(b,),
        in_specs=[
            pl.BlockSpec((1, s, 2 * LANES), lambda i: (i, 0, BLK_B_Q // 2)),
            pl.BlockSpec((1, s, 2 * LANES), lambda i: (i, 0, BLK_B_K // 2)),
            pl.BlockSpec((1, s, 2 * LANES), lambda i: (i, 0, BLK_B_V // 2)),
            pl.BlockSpec((1, s, 2 * LANES), lambda i: (i, 0, 0)),
            pl.BlockSpec((1, nb, 2 * LANES), lambda i: (i, 0, 0)),
        ],
        out_specs=pl.BlockSpec((1, s, 2 * LANES), lambda i: (i, 0, 0)),
        out_shape=jax.ShapeDtypeStruct((b, s, 2 * LANES), BF16),
        scratch_shapes=[
            pltpu.VMEM((4, s, LANES), BF16),
            pltpu.VMEM((4, HEAD_DIM + 16, s), BF16),
            pltpu.VMEM((2 * LANES, s), BF16),
            pltpu.VMEM((4, nb, s), BF16),
        ],
        compiler_params=pltpu.CompilerParams(
            dimension_semantics=("arbitrary",), vmem_limit_bytes=VMEM_LIMIT),
        name="moba",
    )(qkv, qkv, qkv, bq32, kmean)


def _sb_kernel(q_ref, k_ref, v_ref, o_ref, tri_ref, carry_ref, acc_ref, *, seq):
    w = SB_BLOCK
    nb = seq // w
    win = SB_WINDOW_BLKS * w
    n_rows = q_ref.shape[0]
    n_pairs = q_ref.shape[-1] // LANES
    n_heads = 2 * n_pairs
    chains = [(t, b, h) for t in range(SB_STREAMS) for b in range(n_rows) for h in range(n_heads)]

    tri_r = _row_iota((2 * w, 2 * w)) % w
    tri_c = _lane_iota((2 * w, 2 * w))
    tri_ref[...] = jnp.where((tri_c >= w) | (tri_r >= tri_c), 1.0, 0.0).astype(BF16)
    lane_lo = _lane_iota((w, LANES)) < HEAD_DIM
    row = _row_iota((w, win))
    kcol = _lane_iota((w, win))

    def tile(i, _):
        tiles = [i + t * (nb // SB_STREAMS) for t in range(SB_STREAMS)]
        qs = [pl.multiple_of(ti * w, w) for ti in tiles]
        qpos = [q0 + row for q0 in qs]
        carry_ref[...] = jnp.zeros_like(carry_ref)
        acc_ref[...] = jnp.zeros_like(acc_ref)
        qm = []
        for t, b, h in chains:
            q2 = q_ref[b, pl.ds(qs[t], w), (h // 2) * LANES:(h // 2 + 1) * LANES].astype(F32)
            qm.append((jnp.where(lane_lo, q2, 0.0) if h % 2 == 0 else jnp.where(lane_lo, 0.0, q2)).astype(BF16))

        def cond(st):
            return st[0] > 0

        def body(st):
            tops = st[1:]
            bottoms = [jnp.maximum(top - SB_WINDOW_BLKS, 0) for top in tops]
            ws = [pl.multiple_of(bottom * w, w) for bottom in bottoms]
            bias = [jnp.where((ws[t] + kcol) < jnp.minimum(qpos[t], tops[t] * w), 0.0, NEG)
                    for t in range(SB_STREAMS)]
            tri = tri_ref[...]
            zm = [_dot_nt(qm[ch], k_ref[b, pl.ds(ws[t], win), (h // 2) * LANES:(h // 2 + 1) * LANES]) + bias[t]
                  for ch, (t, b, h) in enumerate(chains)]
            split = []
            for ch in range(len(chains)):
                neg_log_fail = jnp.maximum(zm[ch], jnp.log2(1.0 + jnp.exp2(jnp.minimum(zm[ch], SB_POW_CLAMP))))
                for c in range(SB_WINDOW_BLKS):
                    nlf = neg_log_fail[:, c * w:(c + 1) * w]
                    hi = pltpu.bitcast(pltpu.bitcast(nlf, jnp.uint32) & jnp.uint32(0xFFFF0000), F32)
                    split.append(jnp.concatenate([hi.astype(BF16), (nlf - hi).astype(BF16)], axis=1))
            run = [carry_ref[ch] for ch in range(len(chains))]
            expo = {}
            for c in reversed(range(SB_WINDOW_BLKS)):
                for ch in range(len(chains)):
                    rc = _dot(split[ch * SB_WINDOW_BLKS + c], tri) + jnp.concatenate([run[ch], run[ch]], axis=1)
                    expo[(ch, c)] = rc[:, :w]
                    run[ch] = rc[:, w:]
            a_all = []
            for ch in range(len(chains)):
                carry_ref[ch] = run[ch]
                a = [jnp.exp2(zm[ch][:, c * w:(c + 1) * w] - expo[(ch, c)]) for c in range(SB_WINDOW_BLKS)]
                a_all.append(jnp.concatenate(a, axis=1).astype(BF16))
            low = None
            for ch, (t, b, h) in enumerate(chains):
                vv = v_ref[b, pl.ds(ws[t], win), (h // 2) * LANES:(h // 2 + 1) * LANES]
                acc_ref[ch] = acc_ref[ch] + _dot(a_all[ch], vv)
                low_ch = jnp.where(bottoms[t] > 0, jnp.min(carry_ref[ch]), -SB_EXIT_LOG2)
                low = low_ch if low is None else jnp.minimum(low, low_ch)
            return (jnp.where(low < -SB_EXIT_LOG2, 1, 0), *bottoms)

        lax.while_loop(cond, body, (jnp.int32(1), *[ti + 1 for ti in tiles]))
        for ch, (t, b, h) in enumerate(chains):
            if h % 2 == 0:
                o_ref[b, pl.ds(qs[t], w), (h // 2) * LANES:(h // 2 + 1) * LANES] = jnp.where(
                    lane_lo, acc_ref[ch], acc_ref[ch + 1]).astype(o_ref.dtype)
        return 0

    lax.fori_loop(0, nb // SB_STREAMS, tile, 0)


def _sb(qkv):
    b, s, _ = qkv.shape
    kern = functools.partial(_sb_kernel, seq=s)
    return pl.pallas_call(
        kern,
        grid=(b // SB_ROWS_PER_STEP,),
        in_specs=[
            pl.BlockSpec((SB_ROWS_PER_STEP, s, 2 * LANES), lambda i: (i, 0, BLK_C_Q // 2)),
            pl.BlockSpec((SB_ROWS_PER_STEP, s, 2 * LANES), lambda i: (i, 0, BLK_C_K // 2)),
            pl.BlockSpec((SB_ROWS_PER_STEP, s, 2 * LANES), lambda i: (i, 0, BLK_C_V // 2)),
        ],
        out_specs=pl.BlockSpec((SB_ROWS_PER_STEP, s, 2 * LANES), lambda i: (i, 0, 0)),
        out_shape=jax.ShapeDtypeStruct((b, s, 2 * LANES), BF16),
        scratch_shapes=[
            pltpu.VMEM((2 * SB_BLOCK, 2 * SB_BLOCK), BF16),
            pltpu.VMEM((4 * SB_ROWS_PER_STEP * SB_STREAMS, SB_BLOCK, SB_BLOCK), F32),
            pltpu.VMEM((4 * SB_ROWS_PER_STEP * SB_STREAMS, SB_BLOCK, LANES), F32),
        ],
        compiler_params=pltpu.CompilerParams(
            dimension_semantics=("arbitrary",), vmem_limit_bytes=VMEM_LIMIT),
        name="stick_breaking",
    )(qkv, qkv, qkv)


def _ret_kernel(q_ref, k_ref, v_ref, g_ref, decay_ref, qw_ref, kw_ref, gc_ref, o_ref, state_ref, avg_ref, *, seq):
    c = RET_CHUNK
    n = seq // c
    lane_lo = _lane_iota((c, LANES)) < HEAD_DIM
    state_ref[...] = jnp.zeros_like(state_ref)

    def split_heads(t):
        return jnp.concatenate([jnp.where(lane_lo, t, 0.0), jnp.where(lane_lo, 0.0, t)], axis=0).astype(BF16)

    avg_r = (_row_iota((2 * LANES, LANES)) % LANES) // HEAD_DIM
    avg_c = _lane_iota((2 * LANES, LANES)) // HEAD_DIM
    avg_ref[...] = jnp.where(avg_r == avg_c, 1.0 / HEAD_DIM, 0.0).astype(BF16)

    def half_stat(t):
        hi = pltpu.bitcast(pltpu.bitcast(t, jnp.uint32) & jnp.uint32(0xFFFF0000), F32)
        return _dot(jnp.concatenate([hi.astype(BF16), (t - hi).astype(BF16)], axis=1), avg_ref[...])

    def step(it, _):
        rows = [pl.ds(pl.multiple_of((it * RET_CHUNKS_PER_STEP + u) * c, c), c) for u in range(RET_CHUNKS_PER_STEP)]
        cols = [slice(p * LANES, (p + 1) * LANES) for p in range(2)]
        q2, v2, scores, kv = {}, {}, {}, {}
        st = [state_ref[p] for p in range(2)]
        jobs = [(u, p) for u in range(RET_CHUNKS_PER_STEP) for p in range(2)]
        for u, p in jobs:
            j = (u, p)
            q2[j] = q_ref[0, rows[u], cols[p]]
            v2[j] = v_ref[0, rows[u], cols[p]]
            k2f = k_ref[0, rows[u], cols[p]].astype(F32)
            scores[j] = _dot_nt(q2[j], split_heads(k2f))
            kv[j] = _dot((k2f * kw_ref[p]).T.astype(BF16), v2[j])
        y = {}
        for u, p in jobs:
            j = (u, p)
            inner = _dot((scores[j] * decay_ref[p]).astype(BF16), split_heads(v2[j].astype(F32)))
            cross = _dot(q2[j], st[p].astype(BF16))
            gcm = gc_ref[p]
            st[p] = st[p] * gcm + jnp.where(gcm > 0.0, kv[j], 0.0)
            y[j] = inner + cross * qw_ref[p]
        mu = {j: half_stat(y[j]) for j in jobs}
        yc = {j: y[j] - mu[j] for j in jobs}
        var = {j: half_stat(yc[j] * yc[j]) for j in jobs}
        for u, p in jobs:
            yn = yc[(u, p)] * lax.rsqrt(var[(u, p)] + LN_EPS)
            g = g_ref[0, rows[u], cols[p]]
            gate = g * (1.0 / (1.0 + jnp.exp(-g)))
            o_ref[0, rows[u], cols[p]] = (yn * gate).astype(o_ref.dtype)
        for p in range(2):
            state_ref[p] = st[p]
        return 0

    lax.fori_loop(0, n // RET_CHUNKS_PER_STEP, step, 0)


def _ret(qkv, g, decay, qw, kw, gcm):
    b, s, _ = qkv.shape
    kern = functools.partial(_ret_kernel, seq=s)
    full3 = lambda a: pl.BlockSpec(a.shape, lambda i: (0, 0, 0))
    return pl.pallas_call(
        kern,
        grid=(b,),
        in_specs=[
            pl.BlockSpec((1, s, 2 * LANES), lambda i: (i, 0, BLK_D_Q // 2)),
            pl.BlockSpec((1, s, 2 * LANES), lambda i: (i, 0, BLK_D_K // 2)),
            pl.BlockSpec((1, s, 2 * LANES), lambda i: (i, 0, BLK_D_V // 2)),
            pl.BlockSpec((1, s, 2 * LANES), lambda i: (i, 0, 0)),
            full3(decay), full3(qw), full3(kw), full3(gcm),
        ],
        out_specs=pl.BlockSpec((1, s, 2 * LANES), lambda i: (i, 0, 0)),
        out_shape=jax.ShapeDtypeStruct((b, s, 2 * LANES), BF16),
        scratch_shapes=[pltpu.VMEM((2, LANES, LANES), F32), pltpu.VMEM((2 * LANES, LANES), BF16)],
        compiler_params=pltpu.CompilerParams(
            dimension_semantics=("arbitrary",), vmem_limit_bytes=VMEM_LIMIT),
        name="retention",
    )(qkv, qkv, qkv, g, decay, qw, kw, gcm)


def _layer_norm(t, w, b):
    mu = jnp.mean(t, axis=-1, keepdims=True)
    tc = t - mu
    var = jnp.mean(tc * tc, axis=-1, keepdims=True)
    return tc * lax.rsqrt(var + LN_EPS) * w + b


def _out_mlp_kernel(ya_ref, yb_ref, yc_ref, yd_ref, x_ref, wo_ref, l1w_ref, l1b_ref,
                    wu_ref, wd_ref, l2w_ref, l2b_ref, o_ref):
    width = ya_ref.shape[-1]
    ff = wu_ref.shape[1]
    step = 1024
    n_sub = x_ref.shape[1] // OUT_SUBTILE
    rows = [slice(i * OUT_SUBTILE, (i + 1) * OUT_SUBTILE) for i in range(n_sub)]

    def out_proj(r):
        h = None
        for idx, y_ref in enumerate((ya_ref, yb_ref, yc_ref, yd_ref)):
            part = _dot(y_ref[0, r, :], wo_ref[idx * width:(idx + 1) * width, :])
            h = part if h is None else h + part
        return h

    def mlp(x1b):
        acc = None
        for c in range(ff // step):
            hid = _dot(x1b, wu_ref[:, c * step:(c + 1) * step])
            hid = jnp.square(jnp.maximum(hid, 0.0)).astype(BF16)
            part = _dot(hid, wd_ref[c * step:(c + 1) * step, :])
            acc = part if acc is None else acc + part
        return acc

    h = [out_proj(r) for r in rows]
    x1 = [None] * n_sub
    acc = [None] * n_sub
    for i in range(n_sub):
        x1[i] = _layer_norm(DEEPNORM_ALPHA * x_ref[0, rows[i], :] + h[i], l1w_ref[...], l1b_ref[...])
        acc[i] = mlp(x1[i].astype(BF16))
        if i > 0:
            o_ref[0, rows[i - 1], :] = _layer_norm(DEEPNORM_ALPHA * x1[i - 1] + acc[i - 1], l2w_ref[...], l2b_ref[...])
    o_ref[0, rows[-1], :] = _layer_norm(DEEPNORM_ALPHA * x1[-1] + acc[-1], l2w_ref[...], l2b_ref[...])


def _out_mlp(ya, yb, yc, yd, x, wo, l1w, l1b, wu, wd, l2w, l2b, *, tm):
    b, s, d = x.shape
    nt = s // tm
    ytile = pl.BlockSpec((1, tm, ya.shape[-1]), lambda i, t: (i, t, 0))
    const = lambda a: pl.BlockSpec(a.shape, lambda i, t: (0, 0), pipeline_mode=pl.Buffered(1))
    return pl.pallas_call(
        _out_mlp_kernel,
        grid=(b, nt),
        in_specs=[ytile, ytile, ytile, ytile,
                  pl.BlockSpec((1, tm, d), lambda i, t: (i, t, 0)),
                  const(wo), const(l1w), const(l1b), const(wu), const(wd), const(l2w), const(l2b)],
        out_specs=pl.BlockSpec((1, tm, d), lambda i, t: (i, t, 0)),
        out_shape=jax.ShapeDtypeStruct((b, s, d), F32),
        compiler_params=pltpu.CompilerParams(
            dimension_semantics=("arbitrary", "arbitrary"), vmem_limit_bytes=VMEM_LIMIT),
        name="out_mlp",
    )(ya, yb, yc, yd, x, wo, l1w, l1b, wu, wd, l2w, l2b)


def _rope_tables(s):
    half = HEAD_DIM // 2
    inv = 1.0 / (ROPE_THETA ** (jnp.arange(0, HEAD_DIM, 2, dtype=F32) / HEAD_DIM))
    ang = jnp.arange(s, dtype=F32)[:, None] * inv[None, :]
    cos = jnp.cos(ang)
    sin = jnp.sin(ang)
    cos_t = jnp.tile(cos, (1, LANES // half))
    sin_t = jnp.tile(jnp.concatenate([-sin, sin], axis=1), (1, LANES // HEAD_DIM))
    return cos_t, sin_t


def _retention_tables():
    c = RET_CHUNK
    h = RET_HEADS
    log_g = jnp.log(1.0 - 2.0 ** (-5.0 - jnp.arange(h, dtype=F32)))
    idx = jnp.arange(c, dtype=F32)
    diff = idx[:, None] - idx[None, :]
    decay = jnp.where(diff >= 0, jnp.exp(jnp.maximum(diff, 0.0)[None] * log_g[:, None, None]), 0.0)
    k_w = jnp.exp((c - 1.0 - idx)[:, None] * log_g[None, :])
    q_w = jnp.exp((idx + 1.0)[:, None] * log_g[None, :])
    g_c = jnp.exp(c * log_g)
    per_lane = lambda t: jnp.repeat(t, HEAD_DIM, axis=1).reshape(c, h // 2, LANES).transpose(1, 0, 2)
    head_of = jnp.arange(LANES) // HEAD_DIM
    same = head_of[:, None] == head_of[None, :]
    gcm = jnp.stack([jnp.where(same, g_c[2 * p + head_of][None, :], 0.0) for p in range(h // 2)])
    decay_pairs = jnp.concatenate([decay[0::2], decay[1::2]], axis=2)
    return decay_pairs, per_lane(q_w), per_lane(k_w), gcm


def kernel(x, w_in, swa_sinks, w_out, ln1_w, ln1_b, w_up, w_down, ln2_w, ln2_b):
    b, s, d = x.shape
    cos_t, sin_t = _rope_tables(s)
    decay, qw, kw, gcm = _retention_tables()
    row = lambda t: t.reshape(1, -1)
    tm_in = 1024
    tm_out = 2 * OUT_SUBTILE
    for layer in range(w_in.shape[0]):
        qkv, bq32, kmean8, g = _in_proj(x, w_in[layer].astype(BF16), cos_t, sin_t, tm=tm_in)
        nblk = tm_in // MOBA_BLOCK
        kmean = kmean8.reshape(b, s // tm_in, 8, 2 * LANES)[:, :, :nblk].reshape(b, s // MOBA_BLOCK, 2 * LANES)
        ya = _swa(qkv, swa_sinks[layer])
        yb = _moba(qkv, bq32, kmean)
        yc = _sb(qkv)
        yd = _ret(qkv, g, decay, qw, kw, gcm)
        x = _out_mlp(ya, yb, yc, yd, x, w_out[layer].astype(BF16), row(ln1_w[layer]), row(ln1_b[layer]),
                     w_up[layer].astype(BF16), w_down[layer].astype(BF16), row(ln2_w[layer]), row(ln2_b[layer]),
                     tm=tm_out)
    return x
```

```python
import functools
import math

import jax
import jax.numpy as jnp
from jax import lax
from jax.experimental import pallas as pl
from jax.experimental.pallas import tpu as pltpu

F32 = jnp.float32
BF16 = jnp.bfloat16

D_MODEL = 1024
HEAD_DIM = 64
LANES = 128
F32_ROWS = 8
BF16_ROWS = 16
D_FF = 4 * D_MODEL
DEPTH = 2
ROPE_THETA = 10000.0
LN_EPS = 1e-5
NEG = -1e30
DEEPNORM_ALPHA = (2 * DEPTH) ** 0.25
QK_SCALE = HEAD_DIM ** -0.5

SWA_BLOCK = 128
SWA_WINDOW = 128
SWA_BLOCKS_PER_STEP = 4
SWA_ROWS_PER_STEP = 2
MOBA_BLOCK = 256
MOBA_BLOCKS_PER_TILE = 4
MOBA_TOPK = 3
MOBA_PAST_PER_STEP = 4
MOBA_PV_LAG = 6
SB_BLOCK = 128
SB_WINDOW_BLKS = 3
SB_ROWS_PER_STEP = 2
SB_STREAMS = 2
SB_POW_CLAMP = 126.0
RET_CHUNK = 128
RET_HEADS = 4
RET_CHUNKS_PER_STEP = 8
IN_PROJ_TILE = 1024
OUT_SUBTILE = 256
OUT_TILE = 2 * OUT_SUBTILE

BLK_A_Q, BLK_A_K, BLK_A_V = 0, 2, 3
BLK_B_Q, BLK_B_K, BLK_B_V = 4, 6, 8
BLK_C_Q, BLK_C_K, BLK_C_V = 10, 12, 14
BLK_D_Q, BLK_D_K, BLK_D_V, BLK_D_G = 16, 18, 20, 22
N_QKV_BLKS = 22
ROPED_BLKS = frozenset([0, 1, 2, 4, 5, 6, 7, 16, 17, 18, 19])
LOG2E = math.log2(math.e)
BLK_SCALE = {blk: QK_SCALE * LOG2E for blk in (0, 1, 4, 5, 10, 11)}
BLK_SCALE.update({18: QK_SCALE, 19: QK_SCALE})

SB_EXIT_LOG2 = -110.0 * LOG2E

VMEM_LIMIT = 56 * 1024 * 1024


def _dot(a, b, **kw):
    return jnp.dot(a, b, preferred_element_type=F32, **kw)


def _dot_nt(a, b, **kw):
    return lax.dot_general(a, b, (((1,), (1,)), ((), ())), preferred_element_type=F32, **kw)


def _lane_iota(shape):
    return lax.broadcasted_iota(jnp.int32, shape, len(shape) - 1)


def _row_iota(shape):
    return lax.broadcasted_iota(jnp.int32, shape, len(shape) - 2)


def _in_proj_kernel(x_ref, w_ref, cos_ref, sin_ref, qkv_ref, bq_ref, kmean_ref, g_ref, *, tm):
    xb = x_ref[0].astype(BF16)
    cos = cos_ref[...]
    sin = sin_ref[...]
    first_half = (_lane_iota((tm, LANES)) % HEAD_DIM) < (HEAD_DIM // 2)
    n_pairs = w_ref.shape[1] // (2 * LANES)
    for c in range(n_pairs):
        acc2 = _dot(xb, w_ref[:, c * 2 * LANES:(c + 1) * 2 * LANES])
        for half in range(2):
            blk = 2 * c + half
            acc = acc2[:, half * LANES:(half + 1) * LANES]
            if blk in ROPED_BLKS:
                partner = jnp.where(first_half,
                                    pltpu.roll(acc, LANES - HEAD_DIM // 2, 1),
                                    pltpu.roll(acc, HEAD_DIM // 2, 1))
                acc = acc * cos + partner * sin
            if blk in (BLK_B_Q, BLK_B_Q + 1):
                o = (blk - BLK_B_Q) * LANES
                bq_ref[0, :, o:o + LANES] = acc
            if blk in (BLK_B_K, BLK_B_K + 1):
                o = (blk - BLK_B_K) * LANES
                nblk = tm // MOBA_BLOCK
                means = [jnp.sum(acc[i * MOBA_BLOCK:(i + 1) * MOBA_BLOCK], axis=0, keepdims=True)
                         * (1.0 / MOBA_BLOCK) for i in range(nblk)]
                means.append(jnp.zeros((F32_ROWS - nblk, LANES), F32))
                kmean_ref[0, :, o:o + LANES] = jnp.concatenate(means, axis=0)
            if blk in (BLK_D_G, BLK_D_G + 1):
                o = (blk - BLK_D_G) * LANES
                g_ref[0, :, o:o + LANES] = acc
            if blk < N_QKV_BLKS:
                if blk in BLK_SCALE:
                    acc = acc * BLK_SCALE[blk]
                qkv_ref[0, :, blk * LANES:(blk + 1) * LANES] = acc.astype(BF16)


def _in_proj(x, w_in_bf16, cos_t, sin_t, *, tm):
    b, s, d = x.shape
    nt = s // tm
    n_in = w_in_bf16.shape[1]
    kern = functools.partial(_in_proj_kernel, tm=tm)
    return pl.pallas_call(
        kern,
        grid=(b, nt),
        in_specs=[
            pl.BlockSpec((1, tm, d), lambda i, t: (i, t, 0)),
            pl.BlockSpec((d, n_in), lambda i, t: (0, 0), pipeline_mode=pl.Buffered(1)),
            pl.BlockSpec((tm, LANES), lambda i, t: (t, 0)),
            pl.BlockSpec((tm, LANES), lambda i, t: (t, 0)),
        ],
        out_specs=[
            pl.BlockSpec((1, tm, N_QKV_BLKS * LANES), lambda i, t: (i, t, 0)),
            pl.BlockSpec((1, tm, 2 * LANES), lambda i, t: (i, t, 0)),
            pl.BlockSpec((1, F32_ROWS, 2 * LANES), lambda i, t: (i * nt + t, 0, 0)),
            pl.BlockSpec((1, tm, 2 * LANES), lambda i, t: (i, t, 0)),
        ],
        out_shape=[
            jax.ShapeDtypeStruct((b, s, N_QKV_BLKS * LANES), BF16),
            jax.ShapeDtypeStruct((b, s, 2 * LANES), F32),
            jax.ShapeDtypeStruct((b * nt, F32_ROWS, 2 * LANES), F32),
            jax.ShapeDtypeStruct((b, s, 2 * LANES), F32),
        ],
        compiler_params=pltpu.CompilerParams(
            dimension_semantics=("arbitrary", "arbitrary"), vmem_limit_bytes=VMEM_LIMIT),
        name="in_proj",
    )(x, w_in_bf16, cos_t, sin_t)


def _swa_kernel(sinks_ref, q_ref, k_ref, v_ref, o_ref, qt_ref, vt_ref, bias_ref, *, seq):
    w = SWA_BLOCK
    nb = seq // w
    hd = HEAD_DIM
    n_kv = k_ref.shape[-1] // hd
    n_rows = q_ref.shape[0]
    chunk = 2 * w

    def prep(c, _):
        rows = pl.ds(pl.multiple_of(c * chunk, chunk), chunk)
        ones = jnp.ones((BF16_ROWS, chunk), BF16)
        for r in range(n_rows):
            qt_ref[r, :, rows] = q_ref[r, rows, :].astype(F32).T.astype(BF16)
            vt = v_ref[r, rows, :].astype(F32).T.astype(BF16)
            for kv in range(n_kv):
                vt_ref[r * n_kv + kv, :, rows] = jnp.concatenate([vt[kv * hd:(kv + 1) * hd], ones], axis=0)
        return 0

    lax.fori_loop(0, seq // chunk, prep, 0)

    delta = (_lane_iota((2 * w, 2 * w)) % w) - _row_iota((2 * w, 2 * w))
    for t in range(2):
        dist = delta + t * w
        bias_ref[t] = jnp.where((dist >= 0) & (dist < SWA_WINDOW), 0.0, NEG)
    first_head = _lane_iota((1, 2 * w)) < w
    sink = [jnp.where(first_head, sinks_ref[2 * kv], sinks_ref[2 * kv + 1]) * LOG2E
            for kv in range(n_kv)]
    zeros = jnp.zeros((hd, 2 * w), BF16)

    def body(it, _):
        jobs = []
        for u in range(SWA_BLOCKS_PER_STEP):
            i = it * SWA_BLOCKS_PER_STEP + u
            qs = pl.multiple_of(i * w, w)
            ks = pl.multiple_of(jnp.maximum(i - 1, 0) * w, w)
            for r in range(n_rows):
                for kv in range(n_kv):
                    qq = jnp.concatenate([qt_ref[r, (2 * kv) * hd:(2 * kv + 1) * hd, pl.ds(qs, w)],
                                          qt_ref[r, (2 * kv + 1) * hd:(2 * kv + 2) * hd, pl.ds(qs, w)]], axis=1)
                    rhs = jnp.concatenate([qq, zeros] if kv == 0 else [zeros, qq], axis=0)
                    jobs.append((r, qs, ks, jnp.minimum(i, 1), kv, rhs))
        s_all = [_dot(k_ref[r, pl.ds(ks, 2 * w), :], rhs) for r, _, ks, _, _, rhs in jobs]
        mp = []
        for s, (_, _, _, t, kv, _) in zip(s_all, jobs):
            s = s + bias_ref[t]
            m = jnp.maximum(jnp.max(s, axis=0, keepdims=True), sink[kv])
            mp.append((m, jnp.exp2((s - m).astype(BF16))))
        pv_all = [_dot(vt_ref[r * n_kv + kv, :, pl.ds(ks, 2 * w)], p)
                  for (_, p), (r, _, ks, _, kv, _) in zip(mp, jobs)]
        for pv, (m, _), (r, qs, _, _, kv, _) in zip(pv_all, mp, jobs):
            den = pv[hd:hd + 1] + jnp.exp2(sink[kv] - m)
            o = pv[:hd] * (1.0 / den)
            o_t = jnp.concatenate([o[:, :w], o[:, w:]], axis=0)
            o_ref[r, pl.ds(qs, w), kv * LANES:(kv + 1) * LANES] = o_t.T.astype(o_ref.dtype)
        return 0

    lax.fori_loop(0, nb // SWA_BLOCKS_PER_STEP, body, 0)


def _swa(qkv, sinks):
    b, s, _ = qkv.shape
    kern = functools.partial(_swa_kernel, seq=s)
    return pl.pallas_call(
        kern,
        grid=(b // SWA_ROWS_PER_STEP,),
        in_specs=[
            pl.BlockSpec(memory_space=pltpu.SMEM),
            pl.BlockSpec((SWA_ROWS_PER_STEP, s, 2 * LANES), lambda i: (i, 0, BLK_A_Q // 2)),
            pl.BlockSpec((SWA_ROWS_PER_STEP, s, LANES), lambda i: (i, 0, BLK_A_K)),
            pl.BlockSpec((SWA_ROWS_PER_STEP, s, LANES), lambda i: (i, 0, BLK_A_V)),
        ],
        out_specs=pl.BlockSpec((SWA_ROWS_PER_STEP, s, 2 * LANES), lambda i: (i, 0, 0)),
        out_shape=jax.ShapeDtypeStruct((b, s, 2 * LANES), BF16),
        scratch_shapes=[
            pltpu.VMEM((SWA_ROWS_PER_STEP, 2 * LANES, s), BF16),
            pltpu.VMEM((2 * SWA_ROWS_PER_STEP, HEAD_DIM + BF16_ROWS, s), BF16),
            pltpu.VMEM((2, 2 * SWA_BLOCK, 2 * SWA_BLOCK), F32),
        ],
        compiler_params=pltpu.CompilerParams(
            dimension_semantics=("arbitrary",), vmem_limit_bytes=VMEM_LIMIT),
        name="swa",
    )(sinks, qkv, qkv, qkv)


def _moba_kernel(q_ref, k_ref, v_ref, q32_ref, kmean_ref, o_ref, ka_ref, vt_ref, qt_ref, bias_ref, *, seq):
    lb = MOBA_BLOCK
    nb = seq // lb

    n_pairs = q_ref.shape[-1] // LANES
    n_heads = 2 * n_pairs
    hd = HEAD_DIM

    lane_lo_k = _lane_iota((nb, LANES)) < hd
    kmh = []
    for pr in range(n_pairs):
        km = kmean_ref[0, :, pr * LANES:(pr + 1) * LANES]
        kmh += [jnp.where(lane_lo_k, km, 0.0), jnp.where(lane_lo_k, 0.0, km)]

    def prep(c):
        rows = slice(c * lb, (c + 1) * lb)
        lane = _lane_iota((lb, LANES))
        ones = jnp.ones((BF16_ROWS, lb), BF16)
        for pr in range(n_pairs):
            cols = slice(pr * LANES, (pr + 1) * LANES)
            kt = k_ref[0, rows, cols].astype(F32)
            ka_ref[2 * pr, rows, :] = jnp.where(lane < hd, kt, jnp.where(lane - hd == c, 1.0, 0.0)).astype(BF16)
            ka_ref[2 * pr + 1, rows, :] = jnp.where(lane >= hd, kt, jnp.where(lane == c, 1.0, 0.0)).astype(BF16)
            vt = v_ref[0, rows, cols].astype(F32).T.astype(BF16)
            vt_ref[2 * pr, :, rows] = jnp.concatenate([vt[:hd], ones], axis=0)
            vt_ref[2 * pr + 1, :, rows] = jnp.concatenate([vt[hd:], ones], axis=0)
            qt_ref[cols, rows] = q_ref[0, rows, cols].astype(F32).T.astype(BF16)

    rw = MOBA_BLOCKS_PER_TILE * lb
    blk_f = _row_iota((nb, rw)).astype(F32)
    own_blk = (_lane_iota((nb, rw)) // lb).astype(F32)

    def route(sl):
        valid = blk_f < own_blk + float(sl * MOBA_BLOCKS_PER_TILE)
        for pr in range(n_pairs):
            q32 = q32_ref[0, sl * rw:(sl + 1) * rw, pr * LANES:(pr + 1) * LANES]
            both = _dot_nt(jnp.concatenate([kmh[2 * pr], kmh[2 * pr + 1]], axis=0), q32,
                           precision=lax.Precision.HIGHEST)
            for h, gate in ((2 * pr, both[:nb]), (2 * pr + 1, both[nb:])):
                gm = jnp.where(valid, gate, -jnp.inf)
                picked = jnp.zeros((nb, rw), F32)
                for _ in range(MOBA_TOPK):
                    best = jnp.max(gm, axis=0, keepdims=True)
                    first = jnp.min(jnp.where(gm == best, blk_f, float(nb)), axis=0, keepdims=True)
                    hit = blk_f == first
                    picked = jnp.where(hit, 1.0, picked)
                    gm = jnp.where(hit, -jnp.inf, gm)
                routed = jnp.where(picked > 0.0, 0.0, NEG)
                bias_ref[h, :, sl * rw:(sl + 1) * rw] = jnp.where(valid, routed, 0.0).astype(BF16)

    def prep_tile(j):
        for c in range(j * MOBA_BLOCKS_PER_TILE, (j + 1) * MOBA_BLOCKS_PER_TILE):
            prep(c)
        route(j)

    causal_t = _row_iota((lb, lb)) <= _lane_iota((lb, lb))
    pad = jnp.zeros((hd - nb, lb), BF16)
    halves = range(MOBA_BLOCKS_PER_TILE)
    n_tiles = nb // MOBA_BLOCKS_PER_TILE

    def tile(j):
        first = j * MOBA_BLOCKS_PER_TILE
        qs = [(first + u) * lb for u in halves]
        if j + 1 < n_tiles:
            prep_tile(j + 1)

        qa = {}
        for u in halves:
            bias_t = [bias_ref[h, :, pl.ds(qs[u], lb)] for h in range(n_heads)]
            for pr in range(n_pairs):
                qt = qt_ref[pr * LANES:(pr + 1) * LANES, pl.ds(qs[u], lb)]
                qa[(2 * pr, u)] = jnp.concatenate([qt[:hd], bias_t[2 * pr], pad], axis=0)
                qa[(2 * pr + 1, u)] = jnp.concatenate([bias_t[2 * pr + 1], pad, qt[hd:]], axis=0)
        chains = [(h, u) for h in range(n_heads) for u in halves]

        def local_many(jobs):
            out = [None] * len(jobs)
            s_all = [None] * len(jobs)
            for idx in range(len(jobs) + MOBA_PV_LAG):
                if idx < len(jobs):
                    h, u, ks, _ = jobs[idx]
                    s_all[idx] = _dot(ka_ref[h, pl.ds(ks, lb), :], qa[(h, u)])
                k = idx - MOBA_PV_LAG
                if k >= 0:
                    h, _, ks, masked = jobs[k]
                    s = jnp.where(causal_t, s_all[k], NEG) if masked else s_all[k]
                    m = jnp.max(s, axis=0, keepdims=True)
                    p = jnp.exp2((s - m).astype(BF16))
                    out[k] = (m, _dot(vt_ref[h, :, pl.ds(ks, lb)], p))
            return out

        def merge(parts):
            m_new = parts[0][0]
            for m, _ in parts[1:]:
                m_new = jnp.maximum(m_new, m)
            acc = None
            for m, pv in parts:
                term = pv * jnp.exp2(m - m_new)
                acc = term if acc is None else acc + term
            return m_new, acc

        jobs = [(h, u, qs[v], v == u) for h, u in chains for v in range(u + 1)]
        parts = local_many(jobs)
        state = []
        for h, u in chains:
            state += merge([part for part, job in zip(parts, jobs) if job[:2] == (h, u)])

        def past(n, st):
            ks = [pl.multiple_of((n * MOBA_PAST_PER_STEP + v) * lb, lb) for v in range(MOBA_PAST_PER_STEP)]
            parts = local_many([(h, u, ks[v], False) for v in range(MOBA_PAST_PER_STEP) for h, u in chains])
            out = []
            for c in range(len(chains)):
                out += merge([(st[2 * c], st[2 * c + 1])] + parts[c::len(chains)])
            return tuple(out)

        st = lax.fori_loop(0, first // MOBA_PAST_PER_STEP, past, tuple(state))
        for u in halves:
            accs = [st[2 * chains.index((h, u)) + 1] for h in range(n_heads)]
            o_t = jnp.concatenate([a[:hd] * (1.0 / a[hd:hd + 1]) for a in accs], axis=0)
            o_ref[0, pl.ds(qs[u], lb), :] = o_t.T.astype(o_ref.dtype)

    prep_tile(0)
    for j in range(n_tiles):
        tile(j)


def _moba(qkv, bq32, kmean):
    b, s, _ = qkv.shape
    nb = s // MOBA_BLOCK
    kern = functools.partial(_moba_kernel, seq=s)
    return pl.pallas_call(
        kern,
        grid=(b,),
        in_specs=[
            pl.BlockSpec((1, s, 2 * LANES), lambda i: (i, 0, BLK_B_Q // 2)),
            pl.BlockSpec((1, s, 2 * LANES), lambda i: (i, 0, BLK_B_K // 2)),
            pl.BlockSpec((1, s, 2 * LANES), lambda i: (i, 0, BLK_B_V // 2)),
            pl.BlockSpec((1, s, 2 * LANES), lambda i: (i, 0, 0)),
            pl.BlockSpec((1, nb, 2 * LANES), lambda i: (i, 0, 0)),
        ],
        out_specs=pl.BlockSpec((1, s, 2 * LANES), lambda i: (i, 0, 0)),
        out_shape=jax.ShapeDtypeStruct((b, s, 2 * LANES), BF16),
        scratch_shapes=[
            pltpu.VMEM((4, s, LANES), BF16),
            pltpu.VMEM((4, HEAD_DIM + BF16_ROWS, s), BF16),
            pltpu.VMEM((2 * LANES, s), BF16),
            pltpu.VMEM((4, nb, s), BF16),
        ],
        compiler_params=pltpu.CompilerParams(
            dimension_semantics=("arbitrary",), vmem_limit_bytes=VMEM_LIMIT),
        name="moba",
    )(qkv, qkv, qkv, bq32, kmean)


def _sb_kernel(q_ref, k_ref, v_ref, o_ref, tri_ref, carry_ref, acc_ref, *, seq):
    w = SB_BLOCK
    nb = seq // w
    win = SB_WINDOW_BLKS * w
    n_rows = q_ref.shape[0]
    n_pairs = q_ref.shape[-1] // LANES
    n_heads = 2 * n_pairs
    chains = [(t, b, h) for t in range(SB_STREAMS) for b in range(n_rows) for h in range(n_heads)]

    tri_r = _row_iota((2 * w, 2 * w)) % w
    tri_c = _lane_iota((2 * w, 2 * w))
    tri_ref[...] = jnp.where((tri_c >= w) | (tri_r >= tri_c), 1.0, 0.0).astype(BF16)
    lane_lo = _lane_iota((w, LANES)) < HEAD_DIM
    row = _row_iota((w, win))
    kcol = _lane_iota((w, win))

    def tile(i, _):
        tiles = [i + t * (nb // SB_STREAMS) for t in range(SB_STREAMS)]
        qs = [pl.multiple_of(ti * w, w) for ti in tiles]
        qpos = [q0 + row for q0 in qs]
        carry_ref[...] = jnp.zeros_like(carry_ref)
        acc_ref[...] = jnp.zeros_like(acc_ref)
        qm = []
        for t, b, h in chains:
            q2 = q_ref[b, pl.ds(qs[t], w), (h // 2) * LANES:(h // 2 + 1) * LANES].astype(F32)
            qm.append((jnp.where(lane_lo, q2, 0.0) if h % 2 == 0 else jnp.where(lane_lo, 0.0, q2)).astype(BF16))

        def cond(st):
            return st[0] > 0

        def body(st):
            tops = st[1:]
            bottoms = [jnp.maximum(top - SB_WINDOW_BLKS, 0) for top in tops]
            ws = [pl.multiple_of(bottom * w, w) for bottom in bottoms]
            bias = [jnp.where((ws[t] + kcol) < jnp.minimum(qpos[t], tops[t] * w), 0.0, NEG)
                    for t in range(SB_STREAMS)]
            tri = tri_ref[...]
            zm = [_dot_nt(qm[ch], k_ref[b, pl.ds(ws[t], win), (h // 2) * LANES:(h // 2 + 1) * LANES]) + bias[t]
                  for ch, (t, b, h) in enumerate(chains)]
            split = []
            for ch in range(len(chains)):
                neg_log_fail = jnp.maximum(zm[ch], jnp.log2(1.0 + jnp.exp2(jnp.minimum(zm[ch], SB_POW_CLAMP))))
                for c in range(SB_WINDOW_BLKS):
                    nlf = neg_log_fail[:, c * w:(c + 1) * w]
                    hi = pltpu.bitcast(pltpu.bitcast(nlf, jnp.uint32) & jnp.uint32(0xFFFF0000), F32)
                    split.append(jnp.concatenate([hi.astype(BF16), (nlf - hi).astype(BF16)], axis=1))
            run = [carry_ref[ch] for ch in range(len(chains))]
            expo = {}
            for c in reversed(range(SB_WINDOW_BLKS)):
                for ch in range(len(chains)):
                    rc = _dot(split[ch * SB_WINDOW_BLKS + c], tri) + jnp.concatenate([run[ch], run[ch]], axis=1)
                    expo[(ch, c)] = rc[:, :w]
                    run[ch] = rc[:, w:]
            a_all = []
            for ch in range(len(chains)):
                carry_ref[ch] = run[ch]
                a = [jnp.exp2(zm[ch][:, c * w:(c + 1) * w] - expo[(ch, c)]) for c in range(SB_WINDOW_BLKS)]
                a_all.append(jnp.concatenate(a, axis=1).astype(BF16))
            low = None
            for ch, (t, b, h) in enumerate(chains):
                vv = v_ref[b, pl.ds(ws[t], win), (h // 2) * LANES:(h // 2 + 1) * LANES]
                acc_ref[ch] = acc_ref[ch] + _dot(a_all[ch], vv)
                low_ch = jnp.where(bottoms[t] > 0, jnp.min(carry_ref[ch]), -SB_EXIT_LOG2)
                low = low_ch if low is None else jnp.minimum(low, low_ch)
            return (jnp.where(low < -SB_EXIT_LOG2, 1, 0), *bottoms)

        lax.while_loop(cond, body, (jnp.int32(1), *[ti + 1 for ti in tiles]))
        for ch, (t, b, h) in enumerate(chains):
            if h % 2 == 0:
                o_ref[b, pl.ds(qs[t], w), (h // 2) * LANES:(h // 2 + 1) * LANES] = jnp.where(
                    lane_lo, acc_ref[ch], acc_ref[ch + 1]).astype(o_ref.dtype)
        return 0

    lax.fori_loop(0, nb // SB_STREAMS, tile, 0)


def _sb(qkv):
    b, s, _ = qkv.shape
    kern = functools.partial(_sb_kernel, seq=s)
    return pl.pallas_call(
        kern,
        grid=(b // SB_ROWS_PER_STEP,),
        in_specs=[
            pl.BlockSpec((SB_ROWS_PER_STEP, s, 2 * LANES), lambda i: (i, 0, BLK_C_Q // 2)),
            pl.BlockSpec((SB_ROWS_PER_STEP, s, 2 * LANES), lambda i: (i, 0, BLK_C_K // 2)),
            pl.BlockSpec((SB_ROWS_PER_STEP, s, 2 * LANES), lambda i: (i, 0, BLK_C_V // 2)),
        ],
        out_specs=pl.BlockSpec((SB_ROWS_PER_STEP, s, 2 * LANES), lambda i: (i, 0, 0)),
        out_shape=jax.ShapeDtypeStruct((b, s, 2 * LANES), BF16),
        scratch_shapes=[
            pltpu.VMEM((2 * SB_BLOCK, 2 * SB_BLOCK), BF16),
            pltpu.VMEM((4 * SB_ROWS_PER_STEP * SB_STREAMS, SB_BLOCK, SB_BLOCK), F32),
            pltpu.VMEM((4 * SB_ROWS_PER_STEP * SB_STREAMS, SB_BLOCK, LANES), F32),
        ],
        compiler_params=pltpu.CompilerParams(
            dimension_semantics=("arbitrary",), vmem_limit_bytes=VMEM_LIMIT),
        name="stick_breaking",
    )(qkv, qkv, qkv)


def _ret_kernel(q_ref, k_ref, v_ref, g_ref, decay_ref, qw_ref, kw_ref, gc_ref, o_ref, state_ref, avg_ref, *, seq):
    c = RET_CHUNK
    n = seq // c
    lane_lo = _lane_iota((c, LANES)) < HEAD_DIM
    state_ref[...] = jnp.zeros_like(state_ref)

    def split_heads(t):
        return jnp.concatenate([jnp.where(lane_lo, t, 0.0), jnp.where(lane_lo, 0.0, t)], axis=0).astype(BF16)

    avg_r = (_row_iota((2 * LANES, LANES)) % LANES) // HEAD_DIM
    avg_c = _lane_iota((2 * LANES, LANES)) // HEAD_DIM
    avg_ref[...] = jnp.where(avg_r == avg_c, 1.0 / HEAD_DIM, 0.0).astype(BF16)

    def half_stat(t):
        hi = pltpu.bitcast(pltpu.bitcast(t, jnp.uint32) & jnp.uint32(0xFFFF0000), F32)
        return _dot(jnp.concatenate([hi.astype(BF16), (t - hi).astype(BF16)], axis=1), avg_ref[...])

    def step(it, _):
        rows = [pl.ds(pl.multiple_of((it * RET_CHUNKS_PER_STEP + u) * c, c), c) for u in range(RET_CHUNKS_PER_STEP)]
        cols = [slice(p * LANES, (p + 1) * LANES) for p in range(2)]
        q2, v2, scores, kv = {}, {}, {}, {}
        st = [state_ref[p] for p in range(2)]
        jobs = [(u, p) for u in range(RET_CHUNKS_PER_STEP) for p in range(2)]
        for u, p in jobs:
            j = (u, p)
            q2[j] = q_ref[0, rows[u], cols[p]]
            v2[j] = v_ref[0, rows[u], cols[p]]
            k2f = k_ref[0, rows[u], cols[p]].astype(F32)
            scores[j] = _dot_nt(q2[j], split_heads(k2f))
            kv[j] = _dot((k2f * kw_ref[p]).T.astype(BF16), v2[j])
        y = {}
        for u, p in jobs:
            j = (u, p)
            inner = _dot((scores[j] * decay_ref[p]).astype(BF16), split_heads(v2[j].astype(F32)))
            cross = _dot(q2[j], st[p].astype(BF16))
            gcm = gc_ref[p]
            st[p] = st[p] * gcm + jnp.where(gcm > 0.0, kv[j], 0.0)
            y[j] = inner + cross * qw_ref[p]
        mu = {j: half_stat(y[j]) for j in jobs}
        yc = {j: y[j] - mu[j] for j in jobs}
        var = {j: half_stat(yc[j] * yc[j]) for j in jobs}
        for u, p in jobs:
            yn = yc[(u, p)] * lax.rsqrt(var[(u, p)] + LN_EPS)
            g = g_ref[0, rows[u], cols[p]]
            gate = g * (1.0 / (1.0 + jnp.exp(-g)))
            o_ref[0, rows[u], cols[p]] = (yn * gate).astype(o_ref.dtype)
        for p in range(2):
            state_ref[p] = st[p]
        return 0

    lax.fori_loop(0, n // RET_CHUNKS_PER_STEP, step, 0)


def _ret(qkv, g, decay, qw, kw, gcm):
    b, s, _ = qkv.shape
    kern = functools.partial(_ret_kernel, seq=s)
    full3 = lambda a: pl.BlockSpec(a.shape, lambda i: (0, 0, 0))
    return pl.pallas_call(
        kern,
        grid=(b,),
        in_specs=[
            pl.BlockSpec((1, s, 2 * LANES), lambda i: (i, 0, BLK_D_Q // 2)),
            pl.BlockSpec((1, s, 2 * LANES), lambda i: (i, 0, BLK_D_K // 2)),
            pl.BlockSpec((1, s, 2 * LANES), lambda i: (i, 0, BLK_D_V // 2)),
            pl.BlockSpec((1, s, 2 * LANES), lambda i: (i, 0, 0)),
            full3(decay), full3(qw), full3(kw), full3(gcm),
        ],
        out_specs=pl.BlockSpec((1, s, 2 * LANES), lambda i: (i, 0, 0)),
        out_shape=jax.ShapeDtypeStruct((b, s, 2 * LANES), BF16),
        scratch_shapes=[pltpu.VMEM((2, LANES, LANES), F32), pltpu.VMEM((2 * LANES, LANES), BF16)],
        compiler_params=pltpu.CompilerParams(
            dimension_semantics=("arbitrary",), vmem_limit_bytes=VMEM_LIMIT),
        name="retention",
    )(qkv, qkv, qkv, g, decay, qw, kw, gcm)


def _layer_norm(t, w, b):
    mu = jnp.mean(t, axis=-1, keepdims=True)
    tc = t - mu
    var = jnp.mean(tc * tc, axis=-1, keepdims=True)
    return tc * lax.rsqrt(var + LN_EPS) * w + b


def _out_mlp_kernel(ya_ref, yb_ref, yc_ref, yd_ref, x_ref, wo_ref, l1w_ref, l1b_ref,
                    wu_ref, wd_ref, l2w_ref, l2b_ref, o_ref):
    width = ya_ref.shape[-1]
    ff = wu_ref.shape[1]
    step = 1024
    n_sub = x_ref.shape[1] // OUT_SUBTILE
    rows = [slice(i * OUT_SUBTILE, (i + 1) * OUT_SUBTILE) for i in range(n_sub)]

    def out_proj(r):
        h = None
        for idx, y_ref in enumerate((ya_ref, yb_ref, yc_ref, yd_ref)):
            part = _dot(y_ref[0, r, :], wo_ref[idx * width:(idx + 1) * width, :])
            h = part if h is None else h + part
        return h

    def mlp(x1b):
        acc = None
        for c in range(ff // step):
            hid = _dot(x1b, wu_ref[:, c * step:(c + 1) * step])
            hid = jnp.square(jnp.maximum(hid, 0.0)).astype(BF16)
            part = _dot(hid, wd_ref[c * step:(c + 1) * step, :])
            acc = part if acc is None else acc + part
        return acc

    h_next = out_proj(rows[0])
    prev = None
    for i in range(n_sub):
        h = h_next
        if i + 1 < n_sub:
            h_next = out_proj(rows[i + 1])
        x1 = _layer_norm(DEEPNORM_ALPHA * x_ref[0, rows[i], :] + h, l1w_ref[...], l1b_ref[...])
        acc = mlp(x1.astype(BF16))
        if prev is not None:
            o_ref[0, rows[i - 1], :] = _layer_norm(DEEPNORM_ALPHA * prev[0] + prev[1], l2w_ref[...], l2b_ref[...])
        prev = (x1, acc)
    o_ref[0, rows[-1], :] = _layer_norm(DEEPNORM_ALPHA * prev[0] + prev[1], l2w_ref[...], l2b_ref[...])


def _out_mlp(ya, yb, yc, yd, x, wo, l1w, l1b, wu, wd, l2w, l2b, *, tm):
    b, s, d = x.shape
    nt = s // tm
    ytile = pl.BlockSpec((1, tm, ya.shape[-1]), lambda i, t: (i, t, 0))
    const = lambda a: pl.BlockSpec(a.shape, lambda i, t: (0, 0), pipeline_mode=pl.Buffered(1))
    return pl.pallas_call(
        _out_mlp_kernel,
        grid=(b, nt),
        in_specs=[ytile, ytile, ytile, ytile,
                  pl.BlockSpec((1, tm, d), lambda i, t: (i, t, 0)),
                  const(wo), const(l1w), const(l1b), const(wu), const(wd), const(l2w), const(l2b)],
        out_specs=pl.BlockSpec((1, tm, d), lambda i, t: (i, t, 0)),
        out_shape=jax.ShapeDtypeStruct((b, s, d), F32),
        compiler_params=pltpu.CompilerParams(
            dimension_semantics=("arbitrary", "arbitrary"), vmem_limit_bytes=VMEM_LIMIT),
        name="out_mlp",
    )(ya, yb, yc, yd, x, wo, l1w, l1b, wu, wd, l2w, l2b)


def _rope_tables(s):
    half = HEAD_DIM // 2
    inv = 1.0 / (ROPE_THETA ** (jnp.arange(0, HEAD_DIM, 2, dtype=F32) / HEAD_DIM))
    ang = jnp.arange(s, dtype=F32)[:, None] * inv[None, :]
    cos = jnp.cos(ang)
    sin = jnp.sin(ang)
    cos_t = jnp.tile(cos, (1, LANES // half))
    sin_t = jnp.tile(jnp.concatenate([-sin, sin], axis=1), (1, LANES // HEAD_DIM))
    return cos_t, sin_t


def _retention_tables():
    c = RET_CHUNK
    h = RET_HEADS
    log_g = jnp.log(1.0 - 2.0 ** (-5.0 - jnp.arange(h, dtype=F32)))
    idx = jnp.arange(c, dtype=F32)
    diff = idx[:, None] - idx[None, :]
    decay = jnp.where(diff >= 0, jnp.exp(jnp.maximum(diff, 0.0)[None] * log_g[:, None, None]), 0.0)
    k_w = jnp.exp((c - 1.0 - idx)[:, None] * log_g[None, :])
    q_w = jnp.exp((idx + 1.0)[:, None] * log_g[None, :])
    g_c = jnp.exp(c * log_g)
    per_lane = lambda t: jnp.repeat(t, HEAD_DIM, axis=1).reshape(c, h // 2, LANES).transpose(1, 0, 2)
    head_of = jnp.arange(LANES) // HEAD_DIM
    same = head_of[:, None] == head_of[None, :]
    gcm = jnp.stack([jnp.where(same, g_c[2 * p + head_of][None, :], 0.0) for p in range(h // 2)])
    decay_pairs = jnp.concatenate([decay[0::2], decay[1::2]], axis=2)
    return decay_pairs, per_lane(q_w), per_lane(k_w), gcm


def kernel(x, w_in, swa_sinks, w_out, ln1_w, ln1_b, w_up, w_down, ln2_w, ln2_b):
    b, s, d = x.shape
    assert d == D_MODEL and w_in.shape[0] == DEPTH, (x.shape, w_in.shape)
    assert s % IN_PROJ_TILE == 0 and s % OUT_TILE == 0, s
    assert s % (MOBA_BLOCK * MOBA_BLOCKS_PER_TILE) == 0 and MOBA_BLOCKS_PER_TILE % MOBA_PAST_PER_STEP == 0
    assert s % (SB_BLOCK * SB_STREAMS) == 0 and s >= SB_BLOCK * SB_WINDOW_BLKS
    assert s % (SWA_BLOCK * SWA_BLOCKS_PER_STEP) == 0 and s % (RET_CHUNK * RET_CHUNKS_PER_STEP) == 0
    assert b % SB_ROWS_PER_STEP == 0 and b % SWA_ROWS_PER_STEP == 0, b
    cos_t, sin_t = _rope_tables(s)
    decay, qw, kw, gcm = _retention_tables()
    row = lambda t: t.reshape(1, -1)
    tm_in = IN_PROJ_TILE
    tm_out = OUT_TILE
    for layer in range(w_in.shape[0]):
        qkv, bq32, kmean8, g = _in_proj(x, w_in[layer].astype(BF16), cos_t, sin_t, tm=tm_in)
        nblk = tm_in // MOBA_BLOCK
        kmean = kmean8.reshape(b, s // tm_in, F32_ROWS, 2 * LANES)[:, :, :nblk].reshape(b, s // MOBA_BLOCK, 2 * LANES)
        ya = _swa(qkv, swa_sinks[layer])
        yb = _moba(qkv, bq32, kmean)
        yc = _sb(qkv)
        yd = _ret(qkv, g, decay, qw, kw, gcm)
        x = _out_mlp(ya, yb, yc, yd, x, w_out[layer].astype(BF16), row(ln1_w[layer]), row(ln1_b[layer]),
                     w_up[layer].astype(BF16), w_down[layer].astype(BF16), row(ln2_w[layer]), row(ln2_b[layer]),
                     tm=tm_out)
    return x
```

```python
import functools
import math

import jax
import jax.numpy as jnp
from jax import lax
from jax.experimental import pallas as pl
from jax.experimental.pallas import tpu as pltpu

F32 = jnp.float32
BF16 = jnp.bfloat16

D_MODEL = 1024
HEAD_DIM = 64
LANES = 128
F32_ROWS = 8
BF16_ROWS = 16
D_FF = 4 * D_MODEL
DEPTH = 2
ROPE_THETA = 10000.0
LN_EPS = 1e-5
NEG = -1e30
DEEPNORM_ALPHA = (2 * DEPTH) ** 0.25
QK_SCALE = HEAD_DIM ** -0.5

SWA_BLOCK = 128
SWA_WINDOW = 128
SWA_BLOCKS_PER_STEP = 4
SWA_ROWS_PER_STEP = 2
MOBA_BLOCK = 256
MOBA_BLOCKS_PER_TILE = 4
MOBA_TOPK = 3
MOBA_PAST_PER_STEP = 4
MOBA_PV_LAG = 6
SB_BLOCK = 128
SB_WINDOW_BLKS = 3
SB_ROWS_PER_STEP = 2
SB_STREAMS = 2
SB_POW_CLAMP = 126.0
RET_CHUNK = 128
RET_HEADS = 4
RET_CHUNKS_PER_STEP = 8
IN_PROJ_TILE = 1024
OUT_SUBTILE = 256
OUT_TILE = 2 * OUT_SUBTILE

BLK_A_Q, BLK_A_K, BLK_A_V = 0, 2, 3
BLK_B_Q, BLK_B_K, BLK_B_V = 4, 6, 8
BLK_C_Q, BLK_C_K, BLK_C_V = 10, 12, 14
BLK_D_Q, BLK_D_K, BLK_D_V, BLK_D_G = 16, 18, 20, 22
N_QKV_BLKS = 22
ROPED_BLKS = frozenset([0, 1, 2, 4, 5, 6, 7, 16, 17, 18, 19])
LOG2E = math.log2(math.e)
BLK_SCALE = {blk: QK_SCALE * LOG2E for blk in (0, 1, 4, 5, 10, 11)}
BLK_SCALE.update({18: QK_SCALE, 19: QK_SCALE})

SB_EXIT_LOG2 = -110.0 * LOG2E

VMEM_LIMIT = 56 * 1024 * 1024


def _dot(a, b, **kw):
    return jnp.dot(a, b, preferred_element_type=F32, **kw)


def _dot_nt(a, b, **kw):
    return lax.dot_general(a, b, (((1,), (1,)), ((), ())), preferred_element_type=F32, **kw)


def _lane_iota(shape):
    return lax.broadcasted_iota(jnp.int32, shape, len(shape) - 1)


def _row_iota(shape):
    return lax.broadcasted_iota(jnp.int32, shape, len(shape) - 2)


def _in_proj_kernel(x_ref, w_ref, cos_ref, sin_ref, qkv_ref, bq_ref, kmean_ref, g_ref, *, tm):
    xb = x_ref[0].astype(BF16)
    cos = cos_ref[...]
    sin = sin_ref[...]
    first_half = (_lane_iota((tm, LANES)) % HEAD_DIM) < (HEAD_DIM // 2)
    n_pairs = w_ref.shape[1] // (2 * LANES)
    for c in range(n_pairs):
        acc2 = _dot(xb, w_ref[:, c * 2 * LANES:(c + 1) * 2 * LANES])
        for half in range(2):
            blk = 2 * c + half
            acc = acc2[:, half * LANES:(half + 1) * LANES]
            if blk in ROPED_BLKS:
                partner = jnp.where(first_half,
                                    pltpu.roll(acc, LANES - HEAD_DIM // 2, 1),
                                    pltpu.roll(acc, HEAD_DIM // 2, 1))
                acc = acc * cos + partner * sin
            if blk in (BLK_B_Q, BLK_B_Q + 1):
                o = (blk - BLK_B_Q) * LANES
                bq_ref[0, :, o:o + LANES] = acc
            if blk in (BLK_B_K, BLK_B_K + 1):
                o = (blk - BLK_B_K) * LANES
                nblk = tm // MOBA_BLOCK
                means = [jnp.sum(acc[i * MOBA_BLOCK:(i + 1) * MOBA_BLOCK], axis=0, keepdims=True)
                         * (1.0 / MOBA_BLOCK) for i in range(nblk)]
                means.append(jnp.zeros((F32_ROWS - nblk, LANES), F32))
                kmean_ref[0, :, o:o + LANES] = jnp.concatenate(means, axis=0)
            if blk in (BLK_D_G, BLK_D_G + 1):
                o = (blk - BLK_D_G) * LANES
                g_ref[0, :, o:o + LANES] = acc
            if blk < N_QKV_BLKS:
                if blk in BLK_SCALE:
                    acc = acc * BLK_SCALE[blk]
                qkv_ref[0, :, blk * LANES:(blk + 1) * LANES] = acc.astype(BF16)


def _in_proj(x, w_in_bf16, cos_t, sin_t, *, tm):
    b, s, d = x.shape
    nt = s // tm
    n_in = w_in_bf16.shape[1]
    kern = functools.partial(_in_proj_kernel, tm=tm)
    return pl.pallas_call(
        kern,
        grid=(b, nt),
        in_specs=[
            pl.BlockSpec((1, tm, d), lambda i, t: (i, t, 0)),
            pl.BlockSpec((d, n_in), lambda i, t: (0, 0), pipeline_mode=pl.Buffered(1)),
            pl.BlockSpec((tm, LANES), lambda i, t: (t, 0)),
            pl.BlockSpec((tm, LANES), lambda i, t: (t, 0)),
        ],
        out_specs=[
            pl.BlockSpec((1, tm, N_QKV_BLKS * LANES), lambda i, t: (i, t, 0)),
            pl.BlockSpec((1, tm, 2 * LANES), lambda i, t: (i, t, 0)),
            pl.BlockSpec((1, F32_ROWS, 2 * LANES), lambda i, t: (i * nt + t, 0, 0)),
            pl.BlockSpec((1, tm, 2 * LANES), lambda i, t: (i, t, 0)),
        ],
        out_shape=[
            jax.ShapeDtypeStruct((b, s, N_QKV_BLKS * LANES), BF16),
            jax.ShapeDtypeStruct((b, s, 2 * LANES), F32),
            jax.ShapeDtypeStruct((b * nt, F32_ROWS, 2 * LANES), F32),
            jax.ShapeDtypeStruct((b, s, 2 * LANES), F32),
        ],
        compiler_params=pltpu.CompilerParams(
            dimension_semantics=("arbitrary", "arbitrary"), vmem_limit_bytes=VMEM_LIMIT),
        name="in_proj",
    )(x, w_in_bf16, cos_t, sin_t)


def _swa_kernel(sinks_ref, q_ref, k_ref, v_ref, o_ref, qt_ref, vt_ref, bias_ref, *, seq):
    w = SWA_BLOCK
    nb = seq // w
    hd = HEAD_DIM
    n_kv = k_ref.shape[-1] // hd
    n_rows = q_ref.shape[0]
    chunk = 2 * w

    def prep(c):
        rows = slice(c * chunk, (c + 1) * chunk)
        ones = jnp.ones((BF16_ROWS, chunk), BF16)
        for r in range(n_rows):
            qt_ref[r, :, rows] = q_ref[r, rows, :].astype(F32).T.astype(BF16)
            vt = v_ref[r, rows, :].astype(F32).T.astype(BF16)
            for kv in range(n_kv):
                vt_ref[r * n_kv + kv, :, rows] = jnp.concatenate([vt[kv * hd:(kv + 1) * hd], ones], axis=0)

    chunks_per_step = SWA_BLOCKS_PER_STEP * w // chunk

    def prep_step(it):
        for c in range(it * chunks_per_step, (it + 1) * chunks_per_step):
            prep(c)

    delta = (_lane_iota((2 * w, 2 * w)) % w) - _row_iota((2 * w, 2 * w))
    for t in range(2):
        dist = delta + t * w
        bias_ref[t] = jnp.where((dist >= 0) & (dist < SWA_WINDOW), 0.0, NEG)
    first_head = _lane_iota((1, 2 * w)) < w
    sink = [jnp.where(first_head, sinks_ref[2 * kv], sinks_ref[2 * kv + 1]) * LOG2E
            for kv in range(n_kv)]
    zeros = jnp.zeros((hd, 2 * w), BF16)

    n_steps = nb // SWA_BLOCKS_PER_STEP

    def body(it):
        if it + 1 < n_steps:
            prep_step(it + 1)
        jobs = []
        for u in range(SWA_BLOCKS_PER_STEP):
            i = it * SWA_BLOCKS_PER_STEP + u
            qs = i * w
            ks = max(i - 1, 0) * w
            for r in range(n_rows):
                for kv in range(n_kv):
                    qq = jnp.concatenate([qt_ref[r, (2 * kv) * hd:(2 * kv + 1) * hd, pl.ds(qs, w)],
                                          qt_ref[r, (2 * kv + 1) * hd:(2 * kv + 2) * hd, pl.ds(qs, w)]], axis=1)
                    rhs = jnp.concatenate([qq, zeros] if kv == 0 else [zeros, qq], axis=0)
                    jobs.append((r, qs, ks, min(i, 1), kv, rhs))
        s_all = [_dot(k_ref[r, pl.ds(ks, 2 * w), :], rhs) for r, _, ks, _, _, rhs in jobs]
        mp = []
        for s, (_, _, _, t, kv, _) in zip(s_all, jobs):
            s = s + bias_ref[t]
            m = jnp.maximum(jnp.max(s, axis=0, keepdims=True), sink[kv])
            mp.append((m, jnp.exp2((s - m).astype(BF16))))
        pv_all = [_dot(vt_ref[r * n_kv + kv, :, pl.ds(ks, 2 * w)], p)
                  for (_, p), (r, _, ks, _, kv, _) in zip(mp, jobs)]
        for pv, (m, _), (r, qs, _, _, kv, _) in zip(pv_all, mp, jobs):
            den = pv[hd:hd + 1] + jnp.exp2(sink[kv] - m)
            o = pv[:hd] * (1.0 / den)
            o_t = jnp.concatenate([o[:, :w], o[:, w:]], axis=0)
            o_ref[r, pl.ds(qs, w), kv * LANES:(kv + 1) * LANES] = o_t.T.astype(o_ref.dtype)

    prep_step(0)
    for it in range(n_steps):
        body(it)


def _swa(qkv, sinks):
    b, s, _ = qkv.shape
    kern = functools.partial(_swa_kernel, seq=s)
    return pl.pallas_call(
        kern,
        grid=(b // SWA_ROWS_PER_STEP,),
        in_specs=[
            pl.BlockSpec(memory_space=pltpu.SMEM),
            pl.BlockSpec((SWA_ROWS_PER_STEP, s, 2 * LANES), lambda i: (i, 0, BLK_A_Q // 2)),
            pl.BlockSpec((SWA_ROWS_PER_STEP, s, LANES), lambda i: (i, 0, BLK_A_K)),
            pl.BlockSpec((SWA_ROWS_PER_STEP, s, LANES), lambda i: (i, 0, BLK_A_V)),
        ],
        out_specs=pl.BlockSpec((SWA_ROWS_PER_STEP, s, 2 * LANES), lambda i: (i, 0, 0)),
        out_shape=jax.ShapeDtypeStruct((b, s, 2 * LANES), BF16),
        scratch_shapes=[
            pltpu.VMEM((SWA_ROWS_PER_STEP, 2 * LANES, s), BF16),
            pltpu.VMEM((2 * SWA_ROWS_PER_STEP, HEAD_DIM + BF16_ROWS, s), BF16),
            pltpu.VMEM((2, 2 * SWA_BLOCK, 2 * SWA_BLOCK), F32),
        ],
        compiler_params=pltpu.CompilerParams(
            dimension_semantics=("arbitrary",), vmem_limit_bytes=VMEM_LIMIT),
        name="swa",
    )(sinks, qkv, qkv, qkv)


def _moba_kernel(q_ref, k_ref, v_ref, q32_ref, kmean_ref, o_ref, ka_ref, vt_ref, qt_ref, bias_ref, *, seq):
    lb = MOBA_BLOCK
    nb = seq // lb

    n_pairs = q_ref.shape[-1] // LANES
    n_heads = 2 * n_pairs
    hd = HEAD_DIM

    lane_lo_k = _lane_iota((nb, LANES)) < hd
    kmh = []
    for pr in range(n_pairs):
        km = kmean_ref[0, :, pr * LANES:(pr + 1) * LANES]
        kmh += [jnp.where(lane_lo_k, km, 0.0), jnp.where(lane_lo_k, 0.0, km)]

    def prep(c):
        rows = slice(c * lb, (c + 1) * lb)
        lane = _lane_iota((lb, LANES))
        ones = jnp.ones((BF16_ROWS, lb), BF16)
        for pr in range(n_pairs):
            cols = slice(pr * LANES, (pr + 1) * LANES)
            kt = k_ref[0, rows, cols].astype(F32)
            ka_ref[2 * pr, rows, :] = jnp.where(lane < hd, kt, jnp.where(lane - hd == c, 1.0, 0.0)).astype(BF16)
            ka_ref[2 * pr + 1, rows, :] = jnp.where(lane >= hd, kt, jnp.where(lane == c, 1.0, 0.0)).astype(BF16)
            vt = v_ref[0, rows, cols].astype(F32).T.astype(BF16)
            vt_ref[2 * pr, :, rows] = jnp.concatenate([vt[:hd], ones], axis=0)
            vt_ref[2 * pr + 1, :, rows] = jnp.concatenate([vt[hd:], ones], axis=0)
            qt_ref[cols, rows] = q_ref[0, rows, cols].astype(F32).T.astype(BF16)

    rw = MOBA_BLOCKS_PER_TILE * lb
    blk_f = _row_iota((nb, rw)).astype(F32)
    own_blk = (_lane_iota((nb, rw)) // lb).astype(F32)

    def route(sl):
        valid = blk_f < own_blk + float(sl * MOBA_BLOCKS_PER_TILE)
        for pr in range(n_pairs):
            q32 = q32_ref[0, sl * rw:(sl + 1) * rw, pr * LANES:(pr + 1) * LANES]
            both = _dot_nt(jnp.concatenate([kmh[2 * pr], kmh[2 * pr + 1]], axis=0), q32,
                           precision=lax.Precision.HIGHEST)
            for h, gate in ((2 * pr, both[:nb]), (2 * pr + 1, both[nb:])):
                gm = jnp.where(valid, gate, -jnp.inf)
                picked = jnp.zeros((nb, rw), F32)
                for _ in range(MOBA_TOPK):
                    best = jnp.max(gm, axis=0, keepdims=True)
                    first = jnp.min(jnp.where(gm == best, blk_f, float(nb)), axis=0, keepdims=True)
                    hit = blk_f == first
                    picked = jnp.where(hit, 1.0, picked)
                    gm = jnp.where(hit, -jnp.inf, gm)
                routed = jnp.where(picked > 0.0, 0.0, NEG)
                bias_ref[h, :, sl * rw:(sl + 1) * rw] = jnp.where(valid, routed, 0.0).astype(BF16)

    def prep_tile(j):
        for c in range(j * MOBA_BLOCKS_PER_TILE, (j + 1) * MOBA_BLOCKS_PER_TILE):
            prep(c)
        route(j)

    causal_t = _row_iota((lb, lb)) <= _lane_iota((lb, lb))
    pad = jnp.zeros((hd - nb, lb), BF16)
    halves = range(MOBA_BLOCKS_PER_TILE)
    n_tiles = nb // MOBA_BLOCKS_PER_TILE

    def tile(j):
        first = j * MOBA_BLOCKS_PER_TILE
        qs = [(first + u) * lb for u in halves]
        if j + 1 < n_tiles:
            prep_tile(j + 1)

        qa = {}
        for u in halves:
            bias_t = [bias_ref[h, :, pl.ds(qs[u], lb)] for h in range(n_heads)]
            for pr in range(n_pairs):
                qt = qt_ref[pr * LANES:(pr + 1) * LANES, pl.ds(qs[u], lb)]
                qa[(2 * pr, u)] = jnp.concatenate([qt[:hd], bias_t[2 * pr], pad], axis=0)
                qa[(2 * pr + 1, u)] = jnp.concatenate([bias_t[2 * pr + 1], pad, qt[hd:]], axis=0)
        chains = [(h, u) for h in range(n_heads) for u in halves]

        def local_many(jobs):
            out = [None] * len(jobs)
            s_all = [None] * len(jobs)
            for idx in range(len(jobs) + MOBA_PV_LAG):
                if idx < len(jobs):
                    h, u, ks, _ = jobs[idx]
                    s_all[idx] = _dot(ka_ref[h, pl.ds(ks, lb), :], qa[(h, u)])
                k = idx - MOBA_PV_LAG
                if k >= 0:
                    h, _, ks, masked = jobs[k]
                    s = jnp.where(causal_t, s_all[k], NEG) if masked else s_all[k]
                    m = jnp.max(s, axis=0, keepdims=True)
                    p = jnp.exp2((s - m).astype(BF16))
                    out[k] = (m, _dot(vt_ref[h, :, pl.ds(ks, lb)], p))
            return out

        def merge(parts):
            m_new = parts[0][0]
            for m, _ in parts[1:]:
                m_new = jnp.maximum(m_new, m)
            acc = None
            for m, pv in parts:
                term = pv * jnp.exp2(m - m_new)
                acc = term if acc is None else acc + term
            return m_new, acc

        jobs = [(h, u, qs[v], v == u) for h, u in chains for v in range(u + 1)]
        parts = local_many(jobs)
        state = []
        for h, u in chains:
            state += merge([part for part, job in zip(parts, jobs) if job[:2] == (h, u)])

        def past(n, st):
            ks = [pl.multiple_of((n * MOBA_PAST_PER_STEP + v) * lb, lb) for v in range(MOBA_PAST_PER_STEP)]
            parts = local_many([(h, u, ks[v], False) for v in range(MOBA_PAST_PER_STEP) for h, u in chains])
            out = []
            for c in range(len(chains)):
                out += merge([(st[2 * c], st[2 * c + 1])] + parts[c::len(chains)])
            return tuple(out)

        st = lax.fori_loop(0, first // MOBA_PAST_PER_STEP, past, tuple(state))
        for u in halves:
            accs = [st[2 * chains.index((h, u)) + 1] for h in range(n_heads)]
            o_t = jnp.concatenate([a[:hd] * (1.0 / a[hd:hd + 1]) for a in accs], axis=0)
            o_ref[0, pl.ds(qs[u], lb), :] = o_t.T.astype(o_ref.dtype)

    prep_tile(0)
    for j in range(n_tiles):
        tile(j)


def _moba(qkv, bq32, kmean):
    b, s, _ = qkv.shape
    nb = s // MOBA_BLOCK
    kern = functools.partial(_moba_kernel, seq=s)
    return pl.pallas_call(
        kern,
        grid=(b,),
        in_specs=[
            pl.BlockSpec((1, s, 2 * LANES), lambda i: (i, 0, BLK_B_Q // 2)),
            pl.BlockSpec((1, s, 2 * LANES), lambda i: (i, 0, BLK_B_K // 2)),
            pl.BlockSpec((1, s, 2 * LANES), lambda i: (i, 0, BLK_B_V // 2)),
            pl.BlockSpec((1, s, 2 * LANES), lambda i: (i, 0, 0)),
            pl.BlockSpec((1, nb, 2 * LANES), lambda i: (i, 0, 0)),
        ],
        out_specs=pl.BlockSpec((1, s, 2 * LANES), lambda i: (i, 0, 0)),
        out_shape=jax.ShapeDtypeStruct((b, s, 2 * LANES), BF16),
        scratch_shapes=[
            pltpu.VMEM((4, s, LANES), BF16),
            pltpu.VMEM((4, HEAD_DIM + BF16_ROWS, s), BF16),
            pltpu.VMEM((2 * LANES, s), BF16),
            pltpu.VMEM((4, nb, s), BF16),
        ],
        compiler_params=pltpu.CompilerParams(
            dimension_semantics=("arbitrary",), vmem_limit_bytes=VMEM_LIMIT),
        name="moba",
    )(qkv, qkv, qkv, bq32, kmean)


def _sb_kernel(q_ref, k_ref, v_ref, o_ref, tri_ref, carry_ref, acc_ref, *, seq):
    w = SB_BLOCK
    nb = seq // w
    win = SB_WINDOW_BLKS * w
    n_rows = q_ref.shape[0]
    n_pairs = q_ref.shape[-1] // LANES
    n_heads = 2 * n_pairs
    chains = [(t, b, h) for t in range(SB_STREAMS) for b in range(n_rows) for h in range(n_heads)]

    tri_r = _row_iota((2 * w, 2 * w)) % w
    tri_c = _lane_iota((2 * w, 2 * w))
    tri_ref[...] = jnp.where((tri_c >= w) | (tri_r >= tri_c), 1.0, 0.0).astype(BF16)
    lane_lo = _lane_iota((w, LANES)) < HEAD_DIM
    row = _row_iota((w, win))
    kcol = _lane_iota((w, win))

    def tile(i, _):
        tiles = [i + t * (nb // SB_STREAMS) for t in range(SB_STREAMS)]
        qs = [pl.multiple_of(ti * w, w) for ti in tiles]
        qpos = [q0 + row for q0 in qs]
        carry_ref[...] = jnp.zeros_like(carry_ref)
        acc_ref[...] = jnp.zeros_like(acc_ref)
        qm = []
        for t, b, h in chains:
            q2 = q_ref[b, pl.ds(qs[t], w), (h // 2) * LANES:(h // 2 + 1) * LANES].astype(F32)
            qm.append((jnp.where(lane_lo, q2, 0.0) if h % 2 == 0 else jnp.where(lane_lo, 0.0, q2)).astype(BF16))

        def cond(st):
            return st[0] > 0

        def body(st):
            tops = st[1:]
            bottoms = [jnp.maximum(top - SB_WINDOW_BLKS, 0) for top in tops]
            ws = [pl.multiple_of(bottom * w, w) for bottom in bottoms]
            bias = [jnp.where((ws[t] + kcol) < jnp.minimum(qpos[t], tops[t] * w), 0.0, NEG)
                    for t in range(SB_STREAMS)]
            tri = tri_ref[...]
            zm = [_dot_nt(qm[ch], k_ref[b, pl.ds(ws[t], win), (h // 2) * LANES:(h // 2 + 1) * LANES]) + bias[t]
                  for ch, (t, b, h) in enumerate(chains)]
            split = []
            for ch in range(len(chains)):
                neg_log_fail = jnp.maximum(zm[ch], jnp.log2(1.0 + jnp.exp2(jnp.minimum(zm[ch], SB_POW_CLAMP))))
                for c in range(SB_WINDOW_BLKS):
                    nlf = neg_log_fail[:, c * w:(c + 1) * w]
                    hi = pltpu.bitcast(pltpu.bitcast(nlf, jnp.uint32) & jnp.uint32(0xFFFF0000), F32)
                    split.append(jnp.concatenate([hi.astype(BF16), (nlf - hi).astype(BF16)], axis=1))
            run = [carry_ref[ch] for ch in range(len(chains))]
            expo = {}
            for c in reversed(range(SB_WINDOW_BLKS)):
                for ch in range(len(chains)):
                    rc = _dot(split[ch * SB_WINDOW_BLKS + c], tri) + jnp.concatenate([run[ch], run[ch]], axis=1)
                    expo[(ch, c)] = rc[:, :w]
                    run[ch] = rc[:, w:]
            a_all = []
            for ch in range(len(chains)):
                carry_ref[ch] = run[ch]
                a = [jnp.exp2(zm[ch][:, c * w:(c + 1) * w] - expo[(ch, c)]) for c in range(SB_WINDOW_BLKS)]
                a_all.append(jnp.concatenate(a, axis=1).astype(BF16))
            low = None
            for ch, (t, b, h) in enumerate(chains):
                vv = v_ref[b, pl.ds(ws[t], win), (h // 2) * LANES:(h // 2 + 1) * LANES]
                acc_ref[ch] = acc_ref[ch] + _dot(a_all[ch], vv)
                low_ch = jnp.where(bottoms[t] > 0, jnp.min(carry_ref[ch]), -SB_EXIT_LOG2)
                low = low_ch if low is None else jnp.minimum(low, low_ch)
            return (jnp.where(low < -SB_EXIT_LOG2, 1, 0), *bottoms)

        lax.while_loop(cond, body, (jnp.int32(1), *[ti + 1 for ti in tiles]))
        for ch, (t, b, h) in enumerate(chains):
            if h % 2 == 0:
                o_ref[b, pl.ds(qs[t], w), (h // 2) * LANES:(h // 2 + 1) * LANES] = jnp.where(
                    lane_lo, acc_ref[ch], acc_ref[ch + 1]).astype(o_ref.dtype)
        return 0

    lax.fori_loop(0, nb // SB_STREAMS, tile, 0)


def _sb(qkv):
    b, s, _ = qkv.shape
    kern = functools.partial(_sb_kernel, seq=s)
    return pl.pallas_call(
        kern,
        grid=(b // SB_ROWS_PER_STEP,),
        in_specs=[
            pl.BlockSpec((SB_ROWS_PER_STEP, s, 2 * LANES), lambda i: (i, 0, BLK_C_Q // 2)),
            pl.BlockSpec((SB_ROWS_PER_STEP, s, 2 * LANES), lambda i: (i, 0, BLK_C_K // 2)),
            pl.BlockSpec((SB_ROWS_PER_STEP, s, 2 * LANES), lambda i: (i, 0, BLK_C_V // 2)),
        ],
        out_specs=pl.BlockSpec((SB_ROWS_PER_STEP, s, 2 * LANES), lambda i: (i, 0, 0)),
        out_shape=jax.ShapeDtypeStruct((b, s, 2 * LANES), BF16),
        scratch_shapes=[
            pltpu.VMEM((2 * SB_BLOCK, 2 * SB_BLOCK), BF16),
            pltpu.VMEM((4 * SB_ROWS_PER_STEP * SB_STREAMS, SB_BLOCK, SB_BLOCK), F32),
            pltpu.VMEM((4 * SB_ROWS_PER_STEP * SB_STREAMS, SB_BLOCK, LANES), F32),
        ],
        compiler_params=pltpu.CompilerParams(
            dimension_semantics=("arbitrary",), vmem_limit_bytes=VMEM_LIMIT),
        name="stick_breaking",
    )(qkv, qkv, qkv)


def _ret_kernel(q_ref, k_ref, v_ref, g_ref, decay_ref, qw_ref, kw_ref, gc_ref, o_ref, state_ref, avg_ref, *, seq):
    c = RET_CHUNK
    n = seq // c
    lane_lo = _lane_iota((c, LANES)) < HEAD_DIM
    state_ref[...] = jnp.zeros_like(state_ref)

    def split_heads(t):
        return jnp.concatenate([jnp.where(lane_lo, t, 0.0), jnp.where(lane_lo, 0.0, t)], axis=0).astype(BF16)

    avg_r = (_row_iota((2 * LANES, LANES)) % LANES) // HEAD_DIM
    avg_c = _lane_iota((2 * LANES, LANES)) // HEAD_DIM
    avg_ref[...] = jnp.where(avg_r == avg_c, 1.0 / HEAD_DIM, 0.0).astype(BF16)

    def half_stat(t):
        hi = pltpu.bitcast(pltpu.bitcast(t, jnp.uint32) & jnp.uint32(0xFFFF0000), F32)
        return _dot(jnp.concatenate([hi.astype(BF16), (t - hi).astype(BF16)], axis=1), avg_ref[...])

    def step(it, _):
        rows = [pl.ds(pl.multiple_of((it * RET_CHUNKS_PER_STEP + u) * c, c), c) for u in range(RET_CHUNKS_PER_STEP)]
        cols = [slice(p * LANES, (p + 1) * LANES) for p in range(2)]
        q2, v2, scores, kv = {}, {}, {}, {}
        st = [state_ref[p] for p in range(2)]
        jobs = [(u, p) for u in range(RET_CHUNKS_PER_STEP) for p in range(2)]
        for u, p in jobs:
            j = (u, p)
            q2[j] = q_ref[0, rows[u], cols[p]]
            v2[j] = v_ref[0, rows[u], cols[p]]
            k2f = k_ref[0, rows[u], cols[p]].astype(F32)
            scores[j] = _dot_nt(q2[j], split_heads(k2f))
            kv[j] = _dot((k2f * kw_ref[p]).T.astype(BF16), v2[j])
        y = {}
        for u, p in jobs:
            j = (u, p)
            inner = _dot((scores[j] * decay_ref[p]).astype(BF16), split_heads(v2[j].astype(F32)))
            cross = _dot(q2[j], st[p].astype(BF16))
            gcm = gc_ref[p]
            st[p] = st[p] * gcm + jnp.where(gcm > 0.0, kv[j], 0.0)
            y[j] = inner + cross * qw_ref[p]
        mu = {j: half_stat(y[j]) for j in jobs}
        yc = {j: y[j] - mu[j] for j in jobs}
        var = {j: half_stat(yc[j] * yc[j]) for j in jobs}
        for u, p in jobs:
            yn = yc[(u, p)] * lax.rsqrt(var[(u, p)] + LN_EPS)
            g = g_ref[0, rows[u], cols[p]]
            gate = g * (1.0 / (1.0 + jnp.exp(-g)))
            o_ref[0, rows[u], cols[p]] = (yn * gate).astype(o_ref.dtype)
        for p in range(2):
            state_ref[p] = st[p]
        return 0

    lax.fori_loop(0, n // RET_CHUNKS_PER_STEP, step, 0)


def _ret(qkv, g, decay, qw, kw, gcm):
    b, s, _ = qkv.shape
    kern = functools.partial(_ret_kernel, seq=s)
    full3 = lambda a: pl.BlockSpec(a.shape, lambda i: (0, 0, 0))
    return pl.pallas_call(
        kern,
        grid=(b,),
        in_specs=[
            pl.BlockSpec((1, s, 2 * LANES), lambda i: (i, 0, BLK_D_Q // 2)),
            pl.BlockSpec((1, s, 2 * LANES), lambda i: (i, 0, BLK_D_K // 2)),
            pl.BlockSpec((1, s, 2 * LANES), lambda i: (i, 0, BLK_D_V // 2)),
            pl.BlockSpec((1, s, 2 * LANES), lambda i: (i, 0, 0)),
            full3(decay), full3(qw), full3(kw), full3(gcm),
        ],
        out_specs=pl.BlockSpec((1, s, 2 * LANES), lambda i: (i, 0, 0)),
        out_shape=jax.ShapeDtypeStruct((b, s, 2 * LANES), BF16),
        scratch_shapes=[pltpu.VMEM((2, LANES, LANES), F32), pltpu.VMEM((2 * LANES, LANES), BF16)],
        compiler_params=pltpu.CompilerParams(
            dimension_semantics=("arbitrary",), vmem_limit_bytes=VMEM_LIMIT),
        name="retention",
    )(qkv, qkv, qkv, g, decay, qw, kw, gcm)


def _layer_norm(t, w, b):
    mu = jnp.mean(t, axis=-1, keepdims=True)
    tc = t - mu
    var = jnp.mean(tc * tc, axis=-1, keepdims=True)
    return tc * lax.rsqrt(var + LN_EPS) * w + b


def _out_mlp_kernel(ya_ref, yb_ref, yc_ref, yd_ref, x_ref, wo_ref, l1w_ref, l1b_ref,
                    wu_ref, wd_ref, l2w_ref, l2b_ref, o_ref):
    width = ya_ref.shape[-1]
    ff = wu_ref.shape[1]
    step = 1024
    n_sub = x_ref.shape[1] // OUT_SUBTILE
    rows = [slice(i * OUT_SUBTILE, (i + 1) * OUT_SUBTILE) for i in range(n_sub)]

    def out_proj(r):
        h = None
        for idx, y_ref in enumerate((ya_ref, yb_ref, yc_ref, yd_ref)):
            part = _dot(y_ref[0, r, :], wo_ref[idx * width:(idx + 1) * width, :])
            h = part if h is None else h + part
        return h

    def mlp(x1b):
        acc = None
        for c in range(ff // step):
            hid = _dot(x1b, wu_ref[:, c * step:(c + 1) * step])
            hid = jnp.square(jnp.maximum(hid, 0.0)).astype(BF16)
            part = _dot(hid, wd_ref[c * step:(c + 1) * step, :])
            acc = part if acc is None else acc + part
        return acc

    h_next = out_proj(rows[0])
    prev = None
    for i in range(n_sub):
        h = h_next
        if i + 1 < n_sub:
            h_next = out_proj(rows[i + 1])
        x1 = _layer_norm(DEEPNORM_ALPHA * x_ref[0, rows[i], :] + h, l1w_ref[...], l1b_ref[...])
        acc = mlp(x1.astype(BF16))
        if prev is not None:
            o_ref[0, rows[i - 1], :] = _layer_norm(DEEPNORM_ALPHA * prev[0] + prev[1], l2w_ref[...], l2b_ref[...])
        prev = (x1, acc)
    o_ref[0, rows[-1], :] = _layer_norm(DEEPNORM_ALPHA * prev[0] + prev[1], l2w_ref[...], l2b_ref[...])


def _out_mlp(ya, yb, yc, yd, x, wo, l1w, l1b, wu, wd, l2w, l2b, *, tm):
    b, s, d = x.shape
    nt = s // tm
    ytile = pl.BlockSpec((1, tm, ya.shape[-1]), lambda i, t: (i, t, 0))
    const = lambda a: pl.BlockSpec(a.shape, lambda i, t: (0, 0), pipeline_mode=pl.Buffered(1))
    return pl.pallas_call(
        _out_mlp_kernel,
        grid=(b, nt),
        in_specs=[ytile, ytile, ytile, ytile,
                  pl.BlockSpec((1, tm, d), lambda i, t: (i, t, 0)),
                  const(wo), const(l1w), const(l1b), const(wu), const(wd), const(l2w), const(l2b)],
        out_specs=pl.BlockSpec((1, tm, d), lambda i, t: (i, t, 0)),
        out_shape=jax.ShapeDtypeStruct((b, s, d), F32),
        compiler_params=pltpu.CompilerParams(
            dimension_semantics=("arbitrary", "arbitrary"), vmem_limit_bytes=VMEM_LIMIT),
        name="out_mlp",
    )(ya, yb, yc, yd, x, wo, l1w, l1b, wu, wd, l2w, l2b)


def _rope_tables(s):
    half = HEAD_DIM // 2
    inv = 1.0 / (ROPE_THETA ** (jnp.arange(0, HEAD_DIM, 2, dtype=F32) / HEAD_DIM))
    ang = jnp.arange(s, dtype=F32)[:, None] * inv[None, :]
    cos = jnp.cos(ang)
    sin = jnp.sin(ang)
    cos_t = jnp.tile(cos, (1, LANES // half))
    sin_t = jnp.tile(jnp.concatenate([-sin, sin], axis=1), (1, LANES // HEAD_DIM))
    return cos_t, sin_t


def _retention_tables():
    c = RET_CHUNK
    h = RET_HEADS
    log_g = jnp.log(1.0 - 2.0 ** (-5.0 - jnp.arange(h, dtype=F32)))
    idx = jnp.arange(c, dtype=F32)
    diff = idx[:, None] - idx[None, :]
    decay = jnp.where(diff >= 0, jnp.exp(jnp.maximum(diff, 0.0)[None] * log_g[:, None, None]), 0.0)
    k_w = jnp.exp((c - 1.0 - idx)[:, None] * log_g[None, :])
    q_w = jnp.exp((idx + 1.0)[:, None] * log_g[None, :])
    g_c = jnp.exp(c * log_g)
    per_lane = lambda t: jnp.repeat(t, HEAD_DIM, axis=1).reshape(c, h // 2, LANES).transpose(1, 0, 2)
    head_of = jnp.arange(LANES) // HEAD_DIM
    same = head_of[:, None] == head_of[None, :]
    gcm = jnp.stack([jnp.where(same, g_c[2 * p + head_of][None, :], 0.0) for p in range(h // 2)])
    decay_pairs = jnp.concatenate([decay[0::2], decay[1::2]], axis=2)
    return decay_pairs, per_lane(q_w), per_lane(k_w), gcm


def kernel(x, w_in, swa_sinks, w_out, ln1_w, ln1_b, w_up, w_down, ln2_w, ln2_b):
    b, s, d = x.shape
    assert d == D_MODEL and w_in.shape[0] == DEPTH, (x.shape, w_in.shape)
    assert s % IN_PROJ_TILE == 0 and s % OUT_TILE == 0, s
    assert s % (MOBA_BLOCK * MOBA_BLOCKS_PER_TILE) == 0 and MOBA_BLOCKS_PER_TILE % MOBA_PAST_PER_STEP == 0
    assert s % (SB_BLOCK * SB_STREAMS) == 0 and s >= SB_BLOCK * SB_WINDOW_BLKS
    assert s % (SWA_BLOCK * SWA_BLOCKS_PER_STEP) == 0 and s % (RET_CHUNK * RET_CHUNKS_PER_STEP) == 0
    assert b % SB_ROWS_PER_STEP == 0 and b % SWA_ROWS_PER_STEP == 0, b
    cos_t, sin_t = _rope_tables(s)
    decay, qw, kw, gcm = _retention_tables()
    row = lambda t: t.reshape(1, -1)
    tm_in = IN_PROJ_TILE
    tm_out = OUT_TILE
    for layer in range(w_in.shape[0]):
        qkv, bq32, kmean8, g = _in_proj(x, w_in[layer].astype(BF16), cos_t, sin_t, tm=tm_in)
        nblk = tm_in // MOBA_BLOCK
        kmean = kmean8.reshape(b, s // tm_in, F32_ROWS, 2 * LANES)[:, :, :nblk].reshape(b, s // MOBA_BLOCK, 2 * LANES)
        ya = _swa(qkv, swa_sinks[layer])
        yb = _moba(qkv, bq32, kmean)
        yc = _sb(qkv)
        yd = _ret(qkv, g, decay, qw, kw, gcm)
        x = _out_mlp(ya, yb, yc, yd, x, w_out[layer].astype(BF16), row(ln1_w[layer]), row(ln1_b[layer]),
                     w_up[layer].astype(BF16), w_down[layer].astype(BF16), row(ln2_w[layer]), row(ln2_b[layer]),
                     tm=tm_out)
    return x
```

```python
import functools
import math

import jax
import jax.numpy as jnp
from jax import lax
from jax.experimental import pallas as pl
from jax.experimental.pallas import tpu as pltpu

F32 = jnp.float32
BF16 = jnp.bfloat16

D_MODEL = 1024
HEAD_DIM = 64
LANES = 128
F32_ROWS = 8
BF16_ROWS = 16
D_FF = 4 * D_MODEL
DEPTH = 2
ROPE_THETA = 10000.0
LN_EPS = 1e-5
NEG = -1e30
DEEPNORM_ALPHA = (2 * DEPTH) ** 0.25
QK_SCALE = HEAD_DIM ** -0.5

SWA_BLOCK = 128
SWA_WINDOW = 128
SWA_BLOCKS_PER_STEP = 4
SWA_ROWS_PER_STEP = 2
MOBA_BLOCK = 256
MOBA_BLOCKS_PER_TILE = 4
MOBA_TOPK = 3
MOBA_PAST_PER_STEP = 4
MOBA_PV_LAG = 6
SB_BLOCK = 128
SB_WINDOW_BLKS = 3
SB_ROWS_PER_STEP = 2
SB_STREAMS = 2
SB_POW_CLAMP = 126.0
RET_CHUNK = 128
RET_HEADS = 4
RET_CHUNKS_PER_STEP = 8
IN_PROJ_TILE = 1024
OUT_SUBTILE = 256
OUT_TILE = 2 * OUT_SUBTILE

BLK_A_Q, BLK_A_K, BLK_A_V = 0, 2, 3
BLK_B_Q, BLK_B_K, BLK_B_V = 4, 6, 8
BLK_C_Q, BLK_C_K, BLK_C_V = 10, 12, 14
BLK_D_Q, BLK_D_K, BLK_D_V, BLK_D_G = 16, 18, 20, 22
N_QKV_BLKS = 22
ROPED_BLKS = frozenset([0, 1, 2, 4, 5, 6, 7, 16, 17, 18, 19])
LOG2E = math.log2(math.e)
BLK_SCALE = {blk: QK_SCALE * LOG2E for blk in (0, 1, 4, 5, 10, 11)}
BLK_SCALE.update({18: QK_SCALE, 19: QK_SCALE})

SB_EXIT_LOG2 = -110.0 * LOG2E

VMEM_LIMIT = 56 * 1024 * 1024


def _dot(a, b, **kw):
    return jnp.dot(a, b, preferred_element_type=F32, **kw)


def _dot_nt(a, b, **kw):
    return lax.dot_general(a, b, (((1,), (1,)), ((), ())), preferred_element_type=F32, **kw)


def _lane_iota(shape):
    return lax.broadcasted_iota(jnp.int32, shape, len(shape) - 1)


def _row_iota(shape):
    return lax.broadcasted_iota(jnp.int32, shape, len(shape) - 2)


def _in_proj_kernel(x_ref, w_ref, cos_ref, sin_ref, qkv_ref, bq_ref, kmean_ref, g_ref, *, tm):
    xb = x_ref[0].astype(BF16)
    cos = cos_ref[...]
    sin = sin_ref[...]
    first_half = (_lane_iota((tm, LANES)) % HEAD_DIM) < (HEAD_DIM // 2)
    n_pairs = w_ref.shape[1] // (2 * LANES)
    for c in range(n_pairs):
        acc2 = _dot(xb, w_ref[:, c * 2 * LANES:(c + 1) * 2 * LANES])
        for half in range(2):
            blk = 2 * c + half
            acc = acc2[:, half * LANES:(half + 1) * LANES]
            if blk in ROPED_BLKS:
                partner = jnp.where(first_half,
                                    pltpu.roll(acc, LANES - HEAD_DIM // 2, 1),
                                    pltpu.roll(acc, HEAD_DIM // 2, 1))
                acc = acc * cos + partner * sin
            if blk in (BLK_B_Q, BLK_B_Q + 1):
                o = (blk - BLK_B_Q) * LANES
                bq_ref[0, :, o:o + LANES] = acc
            if blk in (BLK_B_K, BLK_B_K + 1):
                o = (blk - BLK_B_K) * LANES
                nblk = tm // MOBA_BLOCK
                means = [jnp.sum(acc[i * MOBA_BLOCK:(i + 1) * MOBA_BLOCK], axis=0, keepdims=True)
                         * (1.0 / MOBA_BLOCK) for i in range(nblk)]
                means.append(jnp.zeros((F32_ROWS - nblk, LANES), F32))
                kmean_ref[0, :, o:o + LANES] = jnp.concatenate(means, axis=0)
            if blk in (BLK_D_G, BLK_D_G + 1):
                o = (blk - BLK_D_G) * LANES
                g_ref[0, :, o:o + LANES] = acc
            if blk < N_QKV_BLKS:
                if blk in BLK_SCALE:
                    acc = acc * BLK_SCALE[blk]
                qkv_ref[0, :, blk * LANES:(blk + 1) * LANES] = acc.astype(BF16)


def _in_proj(x, w_in_bf16, cos_t, sin_t, *, tm):
    b, s, d = x.shape
    nt = s // tm
    n_in = w_in_bf16.shape[1]
    kern = functools.partial(_in_proj_kernel, tm=tm)
    return pl.pallas_call(
        kern,
        grid=(b, nt),
        in_specs=[
            pl.BlockSpec((1, tm, d), lambda i, t: (i, t, 0)),
            pl.BlockSpec((d, n_in), lambda i, t: (0, 0), pipeline_mode=pl.Buffered(1)),
            pl.BlockSpec((tm, LANES), lambda i, t: (t, 0)),
            pl.BlockSpec((tm, LANES), lambda i, t: (t, 0)),
        ],
        out_specs=[
            pl.BlockSpec((1, tm, N_QKV_BLKS * LANES), lambda i, t: (i, t, 0)),
            pl.BlockSpec((1, tm, 2 * LANES), lambda i, t: (i, t, 0)),
            pl.BlockSpec((1, F32_ROWS, 2 * LANES), lambda i, t: (i * nt + t, 0, 0)),
            pl.BlockSpec((1, tm, 2 * LANES), lambda i, t: (i, t, 0)),
        ],
        out_shape=[
            jax.ShapeDtypeStruct((b, s, N_QKV_BLKS * LANES), BF16),
            jax.ShapeDtypeStruct((b, s, 2 * LANES), F32),
            jax.ShapeDtypeStruct((b * nt, F32_ROWS, 2 * LANES), F32),
            jax.ShapeDtypeStruct((b, s, 2 * LANES), F32),
        ],
        compiler_params=pltpu.CompilerParams(
            dimension_semantics=("arbitrary", "arbitrary"), vmem_limit_bytes=VMEM_LIMIT),
        name="in_proj",
    )(x, w_in_bf16, cos_t, sin_t)


def _swa_kernel(sinks_ref, q_ref, k_ref, v_ref, o_ref, qt_ref, vt_ref, bias_ref, *, seq):
    w = SWA_BLOCK
    nb = seq // w
    hd = HEAD_DIM
    n_kv = k_ref.shape[-1] // hd
    n_rows = q_ref.shape[0]
    chunk = 2 * w

    def prep(c):
        rows = slice(c * chunk, (c + 1) * chunk)
        ones = jnp.ones((BF16_ROWS, chunk), BF16)
        for r in range(n_rows):
            qt_ref[r, :, rows] = q_ref[r, rows, :].astype(F32).T.astype(BF16)
            vt = v_ref[r, rows, :].astype(F32).T.astype(BF16)
            for kv in range(n_kv):
                vt_ref[r * n_kv + kv, :, rows] = jnp.concatenate([vt[kv * hd:(kv + 1) * hd], ones], axis=0)

    chunks_per_step = SWA_BLOCKS_PER_STEP * w // chunk

    def prep_step(it):
        for c in range(it * chunks_per_step, (it + 1) * chunks_per_step):
            prep(c)

    delta = (_lane_iota((2 * w, 2 * w)) % w) - _row_iota((2 * w, 2 * w))
    for t in range(2):
        dist = delta + t * w
        bias_ref[t] = jnp.where((dist >= 0) & (dist < SWA_WINDOW), 0.0, NEG)
    first_head = _lane_iota((1, 2 * w)) < w
    sink = [jnp.where(first_head, sinks_ref[2 * kv], sinks_ref[2 * kv + 1]) * LOG2E
            for kv in range(n_kv)]
    zeros = jnp.zeros((hd, 2 * w), BF16)

    n_steps = nb // SWA_BLOCKS_PER_STEP

    def body(it):
        if it + 1 < n_steps:
            prep_step(it + 1)
        jobs = []
        for u in range(SWA_BLOCKS_PER_STEP):
            i = it * SWA_BLOCKS_PER_STEP + u
            qs = i * w
            ks = max(i - 1, 0) * w
            for r in range(n_rows):
                for kv in range(n_kv):
                    qq = jnp.concatenate([qt_ref[r, (2 * kv) * hd:(2 * kv + 1) * hd, pl.ds(qs, w)],
                                          qt_ref[r, (2 * kv + 1) * hd:(2 * kv + 2) * hd, pl.ds(qs, w)]], axis=1)
                    rhs = jnp.concatenate([qq, zeros] if kv == 0 else [zeros, qq], axis=0)
                    jobs.append((r, qs, ks, min(i, 1), kv, rhs))
        s_all = [_dot(k_ref[r, pl.ds(ks, 2 * w), :], rhs) for r, _, ks, _, _, rhs in jobs]
        mp = []
        for s, (_, _, _, t, kv, _) in zip(s_all, jobs):
            s = s + bias_ref[t]
            m = jnp.maximum(jnp.max(s, axis=0, keepdims=True), sink[kv])
            mp.append((m, jnp.exp2((s - m).astype(BF16))))
        pv_all = [_dot(vt_ref[r * n_kv + kv, :, pl.ds(ks, 2 * w)], p)
                  for (_, p), (r, _, ks, _, kv, _) in zip(mp, jobs)]
        for pv, (m, _), (r, qs, _, _, kv, _) in zip(pv_all, mp, jobs):
            den = pv[hd:hd + 1] + jnp.exp2(sink[kv] - m)
            o = pv[:hd] * (1.0 / den)
            o_t = jnp.concatenate([o[:, :w], o[:, w:]], axis=0)
            o_ref[r, pl.ds(qs, w), kv * LANES:(kv + 1) * LANES] = o_t.T.astype(o_ref.dtype)

    prep_step(0)
    for it in range(n_steps):
        body(it)


def _swa(qkv, sinks):
    b, s, _ = qkv.shape
    kern = functools.partial(_swa_kernel, seq=s)
    return pl.pallas_call(
        kern,
        grid=(b // SWA_ROWS_PER_STEP,),
        in_specs=[
            pl.BlockSpec(memory_space=pltpu.SMEM),
            pl.BlockSpec((SWA_ROWS_PER_STEP, s, 2 * LANES), lambda i: (i, 0, BLK_A_Q // 2)),
            pl.BlockSpec((SWA_ROWS_PER_STEP, s, LANES), lambda i: (i, 0, BLK_A_K)),
            pl.BlockSpec((SWA_ROWS_PER_STEP, s, LANES), lambda i: (i, 0, BLK_A_V)),
        ],
        out_specs=pl.BlockSpec((SWA_ROWS_PER_STEP, s, 2 * LANES), lambda i: (i, 0, 0)),
        out_shape=jax.ShapeDtypeStruct((b, s, 2 * LANES), BF16),
        scratch_shapes=[
            pltpu.VMEM((SWA_ROWS_PER_STEP, 2 * LANES, s), BF16),
            pltpu.VMEM((2 * SWA_ROWS_PER_STEP, HEAD_DIM + BF16_ROWS, s), BF16),
            pltpu.VMEM((2, 2 * SWA_BLOCK, 2 * SWA_BLOCK), F32),
        ],
        compiler_params=pltpu.CompilerParams(
            dimension_semantics=("arbitrary",), vmem_limit_bytes=VMEM_LIMIT),
        name="swa",
    )(sinks, qkv, qkv, qkv)


def _moba_kernel(q_ref, k_ref, v_ref, q32_ref, kmean_ref, o_ref, ka_ref, vt_ref, qt_ref, bias_ref, *, seq):
    lb = MOBA_BLOCK
    nb = seq // lb

    n_pairs = q_ref.shape[-1] // LANES
    n_heads = 2 * n_pairs
    hd = HEAD_DIM

    lane_lo_k = _lane_iota((nb, LANES)) < hd
    kmh = []
    for pr in range(n_pairs):
        km = kmean_ref[0, :, pr * LANES:(pr + 1) * LANES]
        kmh += [jnp.where(lane_lo_k, km, 0.0), jnp.where(lane_lo_k, 0.0, km)]

    def prep(c):
        rows = slice(c * lb, (c + 1) * lb)
        lane = _lane_iota((lb, LANES))
        ones = jnp.ones((BF16_ROWS, lb), BF16)
        for pr in range(n_pairs):
            cols = slice(pr * LANES, (pr + 1) * LANES)
            kt = k_ref[0, rows, cols].astype(F32)
            ka_ref[2 * pr, rows, :] = jnp.where(lane < hd, kt, jnp.where(lane - hd == c, 1.0, 0.0)).astype(BF16)
            ka_ref[2 * pr + 1, rows, :] = jnp.where(lane >= hd, kt, jnp.where(lane == c, 1.0, 0.0)).astype(BF16)
            vt = v_ref[0, rows, cols].astype(F32).T.astype(BF16)
            vt_ref[2 * pr, :, rows] = jnp.concatenate([vt[:hd], ones], axis=0)
            vt_ref[2 * pr + 1, :, rows] = jnp.concatenate([vt[hd:], ones], axis=0)
            qt_ref[cols, rows] = q_ref[0, rows, cols].astype(F32).T.astype(BF16)

    rw = MOBA_BLOCKS_PER_TILE * lb
    blk_f = _row_iota((nb, rw)).astype(F32)
    own_blk = (_lane_iota((nb, rw)) // lb).astype(F32)

    def route(sl):
        valid = blk_f < own_blk + float(sl * MOBA_BLOCKS_PER_TILE)
        for pr in range(n_pairs):
            q32 = q32_ref[0, sl * rw:(sl + 1) * rw, pr * LANES:(pr + 1) * LANES]
            both = _dot_nt(jnp.concatenate([kmh[2 * pr], kmh[2 * pr + 1]], axis=0), q32,
                           precision=lax.Precision.HIGHEST)
            for h, gate in ((2 * pr, both[:nb]), (2 * pr + 1, both[nb:])):
                gm = jnp.where(valid, gate, -jnp.inf)
                picked = jnp.zeros((nb, rw), F32)
                for _ in range(MOBA_TOPK):
                    best = jnp.max(gm, axis=0, keepdims=True)
                    first = jnp.min(jnp.where(gm == best, blk_f, float(nb)), axis=0, keepdims=True)
                    hit = blk_f == first
                    picked = jnp.where(hit, 1.0, picked)
                    gm = jnp.where(hit, -jnp.inf, gm)
                routed = jnp.where(picked > 0.0, 0.0, NEG)
                bias_ref[h, :, sl * rw:(sl + 1) * rw] = jnp.where(valid, routed, 0.0).astype(BF16)

    def prep_tile(j):
        for c in range(j * MOBA_BLOCKS_PER_TILE, (j + 1) * MOBA_BLOCKS_PER_TILE):
            prep(c)
        route(j)

    causal_t = _row_iota((lb, lb)) <= _lane_iota((lb, lb))
    pad = jnp.zeros((hd - nb, lb), BF16)
    halves = range(MOBA_BLOCKS_PER_TILE)
    n_tiles = nb // MOBA_BLOCKS_PER_TILE

    def tile(j):
        first = j * MOBA_BLOCKS_PER_TILE
        qs = [(first + u) * lb for u in halves]
        if j + 1 < n_tiles:
            prep_tile(j + 1)

        qa = {}
        for u in halves:
            bias_t = [bias_ref[h, :, pl.ds(qs[u], lb)] for h in range(n_heads)]
            for pr in range(n_pairs):
                qt = qt_ref[pr * LANES:(pr + 1) * LANES, pl.ds(qs[u], lb)]
                qa[(2 * pr, u)] = jnp.concatenate([qt[:hd], bias_t[2 * pr], pad], axis=0)
                qa[(2 * pr + 1, u)] = jnp.concatenate([bias_t[2 * pr + 1], pad, qt[hd:]], axis=0)
        chains = [(h, u) for h in range(n_heads) for u in halves]

        def local_many(jobs):
            out = [None] * len(jobs)
            s_all = [None] * len(jobs)
            for idx in range(len(jobs) + MOBA_PV_LAG):
                if idx < len(jobs):
                    h, u, ks, _ = jobs[idx]
                    s_all[idx] = _dot(ka_ref[h, pl.ds(ks, lb), :], qa[(h, u)])
                k = idx - MOBA_PV_LAG
                if k >= 0:
                    h, _, ks, masked = jobs[k]
                    s = jnp.where(causal_t, s_all[k], NEG) if masked else s_all[k]
                    m = jnp.max(s, axis=0, keepdims=True)
                    p = jnp.exp2((s - m).astype(BF16))
                    out[k] = (m, _dot(vt_ref[h, :, pl.ds(ks, lb)], p))
            return out

        def merge(parts):
            m_new = parts[0][0]
            for m, _ in parts[1:]:
                m_new = jnp.maximum(m_new, m)
            acc = None
            for m, pv in parts:
                term = pv * jnp.exp2(m - m_new)
                acc = term if acc is None else acc + term
            return m_new, acc

        jobs = [(h, u, qs[v], v == u) for h, u in chains for v in range(u + 1)]
        parts = local_many(jobs)
        state = []
        for h, u in chains:
            state += merge([part for part, job in zip(parts, jobs) if job[:2] == (h, u)])

        def past(n, st):
            ks = [pl.multiple_of((n * MOBA_PAST_PER_STEP + v) * lb, lb) for v in range(MOBA_PAST_PER_STEP)]
            parts = local_many([(h, u, ks[v], False) for v in range(MOBA_PAST_PER_STEP) for h, u in chains])
            out = []
            for c in range(len(chains)):
                out += merge([(st[2 * c], st[2 * c + 1])] + parts[c::len(chains)])
            return tuple(out)

        st = lax.fori_loop(0, first // MOBA_PAST_PER_STEP, past, tuple(state))
        for u in halves:
            accs = [st[2 * chains.index((h, u)) + 1] for h in range(n_heads)]
            o_t = jnp.concatenate([a[:hd] * (1.0 / a[hd:hd + 1]) for a in accs], axis=0)
            o_ref[0, pl.ds(qs[u], lb), :] = o_t.T.astype(o_ref.dtype)

    prep_tile(0)
    for j in range(n_tiles):
        tile(j)


def _moba(qkv, bq32, kmean):
    b, s, _ = qkv.shape
    nb = s // MOBA_BLOCK
    kern = functools.partial(_moba_kernel, seq=s)
    return pl.pallas_call(
        kern,
        grid=(b,),
        in_specs=[
            pl.BlockSpec((1, s, 2 * LANES), lambda i: (i, 0, BLK_B_Q // 2)),
            pl.BlockSpec((1, s, 2 * LANES), lambda i: (i, 0, BLK_B_K // 2)),
            pl.BlockSpec((1, s, 2 * LANES), lambda i: (i, 0, BLK_B_V // 2)),
            pl.BlockSpec((1, s, 2 * LANES), lambda i: (i, 0, 0)),
            pl.BlockSpec((1, nb, 2 * LANES), lambda i: (i, 0, 0)),
        ],
        out_specs=pl.BlockSpec((1, s, 2 * LANES), lambda i: (i, 0, 0)),
        out_shape=jax.ShapeDtypeStruct((b, s, 2 * LANES), BF16),
        scratch_shapes=[
            pltpu.VMEM((4, s, LANES), BF16),
            pltpu.VMEM((4, HEAD_DIM + BF16_ROWS, s), BF16),
            pltpu.VMEM((2 * LANES, s), BF16),
            pltpu.VMEM((4, nb, s), BF16),
        ],
        compiler_params=pltpu.CompilerParams(
            dimension_semantics=("arbitrary",), vmem_limit_bytes=VMEM_LIMIT),
        name="moba",
    )(qkv, qkv, qkv, bq32, kmean)


def _sb_kernel(q_ref, k_ref, v_ref, o_ref, tri_ref, carry_ref, acc_ref, *, seq):
    w = SB_BLOCK
    nb = seq // w
    win = SB_WINDOW_BLKS * w
    n_rows = q_ref.shape[0]
    n_pairs = q_ref.shape[-1] // LANES
    n_heads = 2 * n_pairs
    chains = [(t, b, h) for t in range(SB_STREAMS) for b in range(n_rows) for h in range(n_heads)]

    tri_r = _row_iota((2 * w, 2 * w)) % w
    tri_c = _lane_iota((2 * w, 2 * w))
    tri_ref[...] = jnp.where((tri_c >= w) | (tri_r >= tri_c), 1.0, 0.0).astype(BF16)
    lane_lo = _lane_iota((w, LANES)) < HEAD_DIM
    row = _row_iota((w, win))
    kcol = _lane_iota((w, win))

    def tile(i, _):
        tiles = [i + t * (nb // SB_STREAMS) for t in range(SB_STREAMS)]
        qs = [pl.multiple_of(ti * w, w) for ti in tiles]
        qpos = [q0 + row for q0 in qs]
        carry_ref[...] = jnp.zeros_like(carry_ref)
        acc_ref[...] = jnp.zeros_like(acc_ref)
        qm = []
        for t, b, h in chains:
            q2 = q_ref[b, pl.ds(qs[t], w), (h // 2) * LANES:(h // 2 + 1) * LANES].astype(F32)
            qm.append((jnp.where(lane_lo, q2, 0.0) if h % 2 == 0 else jnp.where(lane_lo, 0.0, q2)).astype(BF16))

        def cond(st):
            return st[0] > 0

        def body(st):
            tops = st[1:]
            bottoms = [jnp.maximum(top - SB_WINDOW_BLKS, 0) for top in tops]
            ws = [pl.multiple_of(bottom * w, w) for bottom in bottoms]
            bias = [jnp.where((ws[t] + kcol) < jnp.minimum(qpos[t], tops[t] * w), 0.0, NEG)
                    for t in range(SB_STREAMS)]
            tri = tri_ref[...]
            zm = [_dot_nt(qm[ch], k_ref[b, pl.ds(ws[t], win), (h // 2) * LANES:(h // 2 + 1) * LANES]) + bias[t]
                  for ch, (t, b, h) in enumerate(chains)]
            split = []
            for ch in range(len(chains)):
                neg_log_fail = jnp.maximum(zm[ch], jnp.log2(1.0 + jnp.exp2(jnp.minimum(zm[ch], SB_POW_CLAMP))))
                for c in range(SB_WINDOW_BLKS):
                    nlf = neg_log_fail[:, c * w:(c + 1) * w]
                    hi = pltpu.bitcast(pltpu.bitcast(nlf, jnp.uint32) & jnp.uint32(0xFFFF0000), F32)
                    split.append(jnp.concatenate([hi.astype(BF16), (nlf - hi).astype(BF16)], axis=1))
            run = [carry_ref[ch] for ch in range(len(chains))]
            expo = {}
            for c in reversed(range(SB_WINDOW_BLKS)):
                for ch in range(len(chains)):
                    rc = _dot(split[ch * SB_WINDOW_BLKS + c], tri) + jnp.concatenate([run[ch], run[ch]], axis=1)
                    expo[(ch, c)] = rc[:, :w]
                    run[ch] = rc[:, w:]
            a_all = []
            for ch in range(len(chains)):
                carry_ref[ch] = run[ch]
                a = [jnp.exp2(zm[ch][:, c * w:(c + 1) * w] - expo[(ch, c)]) for c in range(SB_WINDOW_BLKS)]
                a_all.append(jnp.concatenate(a, axis=1).astype(BF16))
            low = None
            for ch, (t, b, h) in enumerate(chains):
                vv = v_ref[b, pl.ds(ws[t], win), (h // 2) * LANES:(h // 2 + 1) * LANES]
                acc_ref[ch] = acc_ref[ch] + _dot(a_all[ch], vv)
                low_ch = jnp.where(bottoms[t] > 0, jnp.min(carry_ref[ch]), -SB_EXIT_LOG2)
                low = low_ch if low is None else jnp.minimum(low, low_ch)
            return (jnp.where(low < -SB_EXIT_LOG2, 1, 0), *bottoms)

        lax.while_loop(cond, body, (jnp.int32(1), *[ti + 1 for ti in tiles]))
        for ch, (t, b, h) in enumerate(chains):
            if h % 2 == 0:
                o_ref[b, pl.ds(qs[t], w), (h // 2) * LANES:(h // 2 + 1) * LANES] = jnp.where(
                    lane_lo, acc_ref[ch], acc_ref[ch + 1]).astype(o_ref.dtype)
        return 0

    lax.fori_loop(0, nb // SB_STREAMS, tile, 0)


def _sb(qkv):
    b, s, _ = qkv.shape
    kern = functools.partial(_sb_kernel, seq=s)
    return pl.pallas_call(
        kern,
        grid=(b // SB_ROWS_PER_STEP,),
        in_specs=[
            pl.BlockSpec((SB_ROWS_PER_STEP, s, 2 * LANES), lambda i: (i, 0, BLK_C_Q // 2)),
            pl.BlockSpec((SB_ROWS_PER_STEP, s, 2 * LANES), lambda i: (i, 0, BLK_C_K // 2)),
            pl.BlockSpec((SB_ROWS_PER_STEP, s, 2 * LANES), lambda i: (i, 0, BLK_C_V // 2)),
        ],
        out_specs=pl.BlockSpec((SB_ROWS_PER_STEP, s, 2 * LANES), lambda i: (i, 0, 0)),
        out_shape=jax.ShapeDtypeStruct((b, s, 2 * LANES), BF16),
        scratch_shapes=[
            pltpu.VMEM((2 * SB_BLOCK, 2 * SB_BLOCK), BF16),
            pltpu.VMEM((4 * SB_ROWS_PER_STEP * SB_STREAMS, SB_BLOCK, SB_BLOCK), F32),
            pltpu.VMEM((4 * SB_ROWS_PER_STEP * SB_STREAMS, SB_BLOCK, LANES), F32),
        ],
        compiler_params=pltpu.CompilerParams(
            dimension_semantics=("arbitrary",), vmem_limit_bytes=VMEM_LIMIT),
        name="stick_breaking",
    )(qkv, qkv, qkv)


def _ret_kernel(q_ref, k_ref, v_ref, g_ref, decay_ref, qw_ref, kw_ref, gc_ref, o_ref, state_ref, avg_ref, *, seq):
    c = RET_CHUNK
    n = seq // c
    lane_lo = _lane_iota((c, LANES)) < HEAD_DIM
    state_ref[...] = jnp.zeros_like(state_ref)

    def split_heads(t):
        return jnp.concatenate([jnp.where(lane_lo, t, 0.0), jnp.where(lane_lo, 0.0, t)], axis=0).astype(BF16)

    avg_r = (_row_iota((2 * LANES, LANES)) % LANES) // HEAD_DIM
    avg_c = _lane_iota((2 * LANES, LANES)) // HEAD_DIM
    avg_ref[...] = jnp.where(avg_r == avg_c, 1.0 / HEAD_DIM, 0.0).astype(BF16)

    def half_stat(t):
        hi = pltpu.bitcast(pltpu.bitcast(t, jnp.uint32) & jnp.uint32(0xFFFF0000), F32)
        return _dot(jnp.concatenate([hi.astype(BF16), (t - hi).astype(BF16)], axis=1), avg_ref[...])

    def step(it):
        rows = [pl.ds((it * RET_CHUNKS_PER_STEP + u) * c, c) for u in range(RET_CHUNKS_PER_STEP)]
        cols = [slice(p * LANES, (p + 1) * LANES) for p in range(2)]
        q2, v2, scores, kv = {}, {}, {}, {}
        st = [state_ref[p] for p in range(2)]
        jobs = [(u, p) for u in range(RET_CHUNKS_PER_STEP) for p in range(2)]
        for u, p in jobs:
            j = (u, p)
            q2[j] = q_ref[0, rows[u], cols[p]]
            v2[j] = v_ref[0, rows[u], cols[p]]
            k2f = k_ref[0, rows[u], cols[p]].astype(F32)
            scores[j] = _dot_nt(q2[j], split_heads(k2f))
            kv[j] = _dot((k2f * kw_ref[p]).T.astype(BF16), v2[j])
        y = {}
        for u, p in jobs:
            j = (u, p)
            inner = _dot((scores[j] * decay_ref[p]).astype(BF16), split_heads(v2[j].astype(F32)))
            cross = _dot(q2[j], st[p].astype(BF16))
            gcm = gc_ref[p]
            st[p] = st[p] * gcm + jnp.where(gcm > 0.0, kv[j], 0.0)
            y[j] = inner + cross * qw_ref[p]
        mu = {j: half_stat(y[j]) for j in jobs}
        yc = {j: y[j] - mu[j] for j in jobs}
        var = {j: half_stat(yc[j] * yc[j]) for j in jobs}
        for u, p in jobs:
            yn = yc[(u, p)] * lax.rsqrt(var[(u, p)] + LN_EPS)
            g = g_ref[0, rows[u], cols[p]]
            gate = g * (1.0 / (1.0 + jnp.exp(-g)))
            o_ref[0, rows[u], cols[p]] = (yn * gate).astype(o_ref.dtype)
        for p in range(2):
            state_ref[p] = st[p]

    for it in range(n // RET_CHUNKS_PER_STEP):
        step(it)


def _ret(qkv, g, decay, qw, kw, gcm):
    b, s, _ = qkv.shape
    kern = functools.partial(_ret_kernel, seq=s)
    full3 = lambda a: pl.BlockSpec(a.shape, lambda i: (0, 0, 0))
    return pl.pallas_call(
        kern,
        grid=(b,),
        in_specs=[
            pl.BlockSpec((1, s, 2 * LANES), lambda i: (i, 0, BLK_D_Q // 2)),
            pl.BlockSpec((1, s, 2 * LANES), lambda i: (i, 0, BLK_D_K // 2)),
            pl.BlockSpec((1, s, 2 * LANES), lambda i: (i, 0, BLK_D_V // 2)),
            pl.BlockSpec((1, s, 2 * LANES), lambda i: (i, 0, 0)),
            full3(decay), full3(qw), full3(kw), full3(gcm),
        ],
        out_specs=pl.BlockSpec((1, s, 2 * LANES), lambda i: (i, 0, 0)),
        out_shape=jax.ShapeDtypeStruct((b, s, 2 * LANES), BF16),
        scratch_shapes=[pltpu.VMEM((2, LANES, LANES), F32), pltpu.VMEM((2 * LANES, LANES), BF16)],
        compiler_params=pltpu.CompilerParams(
            dimension_semantics=("arbitrary",), vmem_limit_bytes=VMEM_LIMIT),
        name="retention",
    )(qkv, qkv, qkv, g, decay, qw, kw, gcm)


def _layer_norm(t, w, b):
    mu = jnp.mean(t, axis=-1, keepdims=True)
    tc = t - mu
    var = jnp.mean(tc * tc, axis=-1, keepdims=True)
    return tc * lax.rsqrt(var + LN_EPS) * w + b


def _out_mlp_kernel(ya_ref, yb_ref, yc_ref, yd_ref, x_ref, wo_ref, l1w_ref, l1b_ref,
                    wu_ref, wd_ref, l2w_ref, l2b_ref, o_ref):
    width = ya_ref.shape[-1]
    ff = wu_ref.shape[1]
    step = 1024
    n_sub = x_ref.shape[1] // OUT_SUBTILE
    rows = [slice(i * OUT_SUBTILE, (i + 1) * OUT_SUBTILE) for i in range(n_sub)]

    def out_proj(r):
        h = None
        for idx, y_ref in enumerate((ya_ref, yb_ref, yc_ref, yd_ref)):
            part = _dot(y_ref[0, r, :], wo_ref[idx * width:(idx + 1) * width, :])
            h = part if h is None else h + part
        return h

    def mlp(x1b):
        acc = None
        for c in range(ff // step):
            hid = _dot(x1b, wu_ref[:, c * step:(c + 1) * step])
            hid = jnp.square(jnp.maximum(hid, 0.0)).astype(BF16)
            part = _dot(hid, wd_ref[c * step:(c + 1) * step, :])
            acc = part if acc is None else acc + part
        return acc

    h_next = out_proj(rows[0])
    prev = None
    for i in range(n_sub):
        h = h_next
        if i + 1 < n_sub:
            h_next = out_proj(rows[i + 1])
        x1 = _layer_norm(DEEPNORM_ALPHA * x_ref[0, rows[i], :] + h, l1w_ref[...], l1b_ref[...])
        acc = mlp(x1.astype(BF16))
        if prev is not None:
            o_ref[0, rows[i - 1], :] = _layer_norm(DEEPNORM_ALPHA * prev[0] + prev[1], l2w_ref[...], l2b_ref[...])
        prev = (x1, acc)
    o_ref[0, rows[-1], :] = _layer_norm(DEEPNORM_ALPHA * prev[0] + prev[1], l2w_ref[...], l2b_ref[...])


def _out_mlp(ya, yb, yc, yd, x, wo, l1w, l1b, wu, wd, l2w, l2b, *, tm):
    b, s, d = x.shape
    nt = s // tm
    ytile = pl.BlockSpec((1, tm, ya.shape[-1]), lambda i, t: (i, t, 0))
    const = lambda a: pl.BlockSpec(a.shape, lambda i, t: (0, 0), pipeline_mode=pl.Buffered(1))
    return pl.pallas_call(
        _out_mlp_kernel,
        grid=(b, nt),
        in_specs=[ytile, ytile, ytile, ytile,
                  pl.BlockSpec((1, tm, d), lambda i, t: (i, t, 0)),
                  const(wo), const(l1w), const(l1b), const(wu), const(wd), const(l2w), const(l2b)],
        out_specs=pl.BlockSpec((1, tm, d), lambda i, t: (i, t, 0)),
        out_shape=jax.ShapeDtypeStruct((b, s, d), F32),
        compiler_params=pltpu.CompilerParams(
            dimension_semantics=("arbitrary", "arbitrary"), vmem_limit_bytes=VMEM_LIMIT),
        name="out_mlp",
    )(ya, yb, yc, yd, x, wo, l1w, l1b, wu, wd, l2w, l2b)


def _rope_tables(s):
    half = HEAD_DIM // 2
    inv = 1.0 / (ROPE_THETA ** (jnp.arange(0, HEAD_DIM, 2, dtype=F32) / HEAD_DIM))
    ang = jnp.arange(s, dtype=F32)[:, None] * inv[None, :]
    cos = jnp.cos(ang)
    sin = jnp.sin(ang)
    cos_t = jnp.tile(cos, (1, LANES // half))
    sin_t = jnp.tile(jnp.concatenate([-sin, sin], axis=1), (1, LANES // HEAD_DIM))
    return cos_t, sin_t


def _retention_tables():
    c = RET_CHUNK
    h = RET_HEADS
    log_g = jnp.log(1.0 - 2.0 ** (-5.0 - jnp.arange(h, dtype=F32)))
    idx = jnp.arange(c, dtype=F32)
    diff = idx[:, None] - idx[None, :]
    decay = jnp.where(diff >= 0, jnp.exp(jnp.maximum(diff, 0.0)[None] * log_g[:, None, None]), 0.0)
    k_w = jnp.exp((c - 1.0 - idx)[:, None] * log_g[None, :])
    q_w = jnp.exp((idx + 1.0)[:, None] * log_g[None, :])
    g_c = jnp.exp(c * log_g)
    per_lane = lambda t: jnp.repeat(t, HEAD_DIM, axis=1).reshape(c, h // 2, LANES).transpose(1, 0, 2)
    head_of = jnp.arange(LANES) // HEAD_DIM
    same = head_of[:, None] == head_of[None, :]
    gcm = jnp.stack([jnp.where(same, g_c[2 * p + head_of][None, :], 0.0) for p in range(h // 2)])
    decay_pairs = jnp.concatenate([decay[0::2], decay[1::2]], axis=2)
    return decay_pairs, per_lane(q_w), per_lane(k_w), gcm


def kernel(x, w_in, swa_sinks, w_out, ln1_w, ln1_b, w_up, w_down, ln2_w, ln2_b):
    b, s, d = x.shape
    assert d == D_MODEL and w_in.shape[0] == DEPTH, (x.shape, w_in.shape)
    assert s % IN_PROJ_TILE == 0 and s % OUT_TILE == 0, s
    assert s % (MOBA_BLOCK * MOBA_BLOCKS_PER_TILE) == 0 and MOBA_BLOCKS_PER_TILE % MOBA_PAST_PER_STEP == 0
    assert s % (SB_BLOCK * SB_STREAMS) == 0 and s >= SB_BLOCK * SB_WINDOW_BLKS
    assert s % (SWA_BLOCK * SWA_BLOCKS_PER_STEP) == 0 and s % (RET_CHUNK * RET_CHUNKS_PER_STEP) == 0
    assert b % SB_ROWS_PER_STEP == 0 and b % SWA_ROWS_PER_STEP == 0, b
    cos_t, sin_t = _rope_tables(s)
    decay, qw, kw, gcm = _retention_tables()
    row = lambda t: t.reshape(1, -1)
    tm_in = IN_PROJ_TILE
    tm_out = OUT_TILE
    for layer in range(w_in.shape[0]):
        qkv, bq32, kmean8, g = _in_proj(x, w_in[layer].astype(BF16), cos_t, sin_t, tm=tm_in)
        nblk = tm_in // MOBA_BLOCK
        kmean = kmean8.reshape(b, s // tm_in, F32_ROWS, 2 * LANES)[:, :, :nblk].reshape(b, s // MOBA_BLOCK, 2 * LANES)
        ya = _swa(qkv, swa_sinks[layer])
        yb = _moba(qkv, bq32, kmean)
        yc = _sb(qkv)
        yd = _ret(qkv, g, decay, qw, kw, gcm)
        x = _out_mlp(ya, yb, yc, yd, x, w_out[layer].astype(BF16), row(ln1_w[layer]), row(ln1_b[layer]),
                     w_up[layer].astype(BF16), w_down[layer].astype(BF16), row(ln2_w[layer]), row(ln2_b[layer]),
                     tm=tm_out)
    return x
```
